```python
import jax, jax.numpy as jnp
from jax import lax
import numpy as np

D_MODEL = 4096
BATCH = 1
SEQ = 8192
DEPTH = 4

N_A = DEPTH // 2
N_B = DEPTH - N_A
RWKV_HEAD = 64
RWKV_WIDTH = 3 * D_MODEL // 4
RWKV_HEADS = RWKV_WIDTH // RWKV_HEAD
LORA_DECAY = max(32, int(round(1.8 * D_MODEL ** 0.5 / 32)) * 32)
LORA_AAA = max(32, int(round(1.8 * D_MODEL ** 0.5 / 32)) * 32)
LORA_MV = max(32, int(round(1.3 * D_MODEL ** 0.5 / 32)) * 32)
LORA_GATE = max(32, int(round(0.6 * D_MODEL ** 0.8 / 32)) * 32)
LNX_EPS = 64e-5
N_MEM = 256
MEM_HEADS = 4
MEM_WIDTH = D_MODEL // 4
MEM_HEAD_DIM = MEM_WIDTH // MEM_HEADS
ATT_HEAD_DIM = 128
DIL_PATTERNS = ((128, 1), (512, 4), (2048, 16))
N_DIL_GROUPS = len(DIL_PATTERNS)
DIL_Q_WIDTH = 3 * D_MODEL // 4
DIL_HEADS = DIL_Q_WIDTH // (N_DIL_GROUPS * ATT_HEAD_DIM)
KV_WIDTH = DIL_HEADS * ATT_HEAD_DIM
ATT_BLOCK = 128
DIL_SPAN = max(d for _, d in DIL_PATTERNS) * ATT_BLOCK
A_IN = 3 * RWKV_WIDTH + LORA_DECAY + LORA_AAA + LORA_GATE + MEM_WIDTH
A_SPLITS = tuple(int(c) for c in np.cumsum([RWKV_WIDTH, RWKV_WIDTH, RWKV_WIDTH, LORA_DECAY, LORA_AAA, LORA_GATE, MEM_WIDTH]))
N_GROUPS = 4
EXPERTS_PER_GROUP = 8
N_EXPERTS = N_GROUPS * EXPERTS_PER_GROUP
TOP_K = 2
D_FF = 3 * D_MODEL // 32
MOE_BLOCK = 128
NORM_EPS = 1e-6

kernel_name = 'yoco_rwkv7_dilated_alibi_hmoe'


def _rms(x, g):
    xf = x.astype(jnp.float32)
    y = xf * lax.rsqrt(jnp.mean(xf * xf, axis=-1, keepdims=True) + NORM_EPS)
    return (y * g.astype(jnp.float32)).astype(x.dtype)


def _alibi_slopes(n):
    return jnp.exp2(-8.0 * jnp.arange(1, n + 1, dtype=jnp.float32) / n)


def _token_shift(y, mu):
    y_prev = jnp.pad(y, ((0, 0), (1, 0), (0, 0)))[:, :-1]
    return y + mu * (y_prev - y)


def _rwkv7_scan(r, w, k, v, a, b):
    Bsz, S, H, N = r.shape

    def step(state, inp):
        r_t, w_t, k_t, v_t, a_t, b_t = inp
        sa = jnp.einsum('bhvk,bhk->bhv', state, a_t)
        state = state * w_t[:, :, None, :] + sa[..., None] * b_t[:, :, None, :] + v_t[..., None] * k_t[:, :, None, :]
        return state, jnp.einsum('bhvk,bhk->bhv', state, r_t)

    xs = tuple(jnp.moveaxis(t, 1, 0) for t in (r, w, k, v, a, b))
    _, ys = lax.scan(step, jnp.zeros((Bsz, H, N, N), jnp.float32), xs)
    return jnp.moveaxis(ys, 0, 1)


def _rwkv7_mix(r, k, v, xw, xa, xg, xv, v_first, w0, w_up, a0, a_up, g_up, v0, v_up, k_k, k_a, r_k, lnx_g, lnx_b):
    f32 = jnp.float32
    r, k, v, xw, xa, xg = (t.astype(f32) for t in (r, k, v, xw, xa, xg))
    Bsz, S, _ = r.shape
    H, N = RWKV_HEADS, RWKV_HEAD
    w_log = -jax.nn.softplus(-(w0 + jnp.einsum('bsl,lc->bsc', jnp.tanh(xw), w_up.astype(f32)))) - 0.5
    decay = jnp.exp(-jnp.exp(w_log))
    a = jax.nn.sigmoid(a0 + jnp.einsum('bsl,lc->bsc', xa, a_up.astype(f32)))
    g = jnp.einsum('bsl,lc->bsc', jax.nn.sigmoid(xg), g_up.astype(f32))
    if xv is not None:
        mix = jax.nn.sigmoid(v0 + jnp.einsum('bsl,lc->bsc', xv.astype(f32), v_up.astype(f32)))
        v = v + (v_first - v) * mix
    heads = lambda t: t.reshape(Bsz, S, H, N)
    kk = heads(k * k_k)
    kk = kk / jnp.maximum(jnp.sqrt(jnp.sum(kk * kk, axis=-1, keepdims=True)), 1e-12)
    k = heads(k * (1.0 + (a - 1.0) * k_a))
    r, v, a, decay = heads(r), heads(v), heads(a), heads(decay)
    y = _rwkv7_scan(r, decay, k, v, -kk, kk * a)
    mean = jnp.mean(y, axis=-1, keepdims=True)
    var = jnp.mean(jnp.square(y - mean), axis=-1, keepdims=True)
    y = ((y - mean) * lax.rsqrt(var + LNX_EPS)).reshape(Bsz, S, RWKV_WIDTH) * lnx_g + lnx_b
    y = y + (jnp.sum(r * k * r_k, axis=-1, keepdims=True) * v).reshape(Bsz, S, RWKV_WIDTH)
    return y * g, v.reshape(Bsz, S, RWKV_WIDTH)


def _mem_kv(mem, g, w_kv, k_gain):
    Bsz, M, _ = mem.shape
    kv = jnp.einsum('bmd,dc->bmc', _rms(mem, g), w_kv)
    k, v = jnp.split(kv, 2, axis=-1)
    k = _rms(k.reshape(Bsz, M, MEM_HEADS, MEM_HEAD_DIM), k_gain)
    return k, v.reshape(Bsz, M, MEM_HEADS, MEM_HEAD_DIM)


def _mem_attend(q, mk, mv, q_gain):
    Bsz, S, _ = q.shape
    q = _rms(q.reshape(Bsz, S, MEM_HEADS, MEM_HEAD_DIM), q_gain)
    s = jnp.einsum('bshd,bmhd->bhsm', q, mk, preferred_element_type=jnp.float32) * (MEM_HEAD_DIM ** -0.5)
    p = jax.nn.softmax(s, axis=-1)
    o = jnp.einsum('bhsm,bmhd->bshd', p, mv.astype(jnp.float32))
    return o.reshape(Bsz, S, MEM_WIDTH).astype(q.dtype)


def _shared_kv(h, g, w_kv, k_gain, s_pad):
    Bsz, S, _ = h.shape
    kv = jnp.einsum('bsd,dc->bsc', _rms(h, g), w_kv)
    k, v = jnp.split(kv, 2, axis=-1)
    k = _rms(k.reshape(Bsz, S, DIL_HEADS, ATT_HEAD_DIM), k_gain)
    v = v.reshape(Bsz, S, DIL_HEADS, ATT_HEAD_DIM)
    pad = ((0, 0), (0, s_pad - S), (0, 0), (0, 0))
    return jnp.pad(k, pad), jnp.pad(v, pad)


def _dilated_group(q, k, v, slopes, window, dil):
    f32 = jnp.float32
    Bsz, Sp, H, Dh = q.shape
    L = Sp // dil
    nb = L // ATT_BLOCK
    n_back = window // dil
    n_prev = -(-n_back // ATT_BLOCK)

    def to_res(t):
        return t.reshape(Bsz, L, dil, H, Dh).transpose(0, 2, 3, 1, 4).reshape(Bsz, dil, H, nb, ATT_BLOCK, Dh)

    qb, kb, vb = to_res(q), to_res(k), to_res(v)
    pad = ((0, 0), (0, 0), (0, 0), (n_prev, 0), (0, 0), (0, 0))
    kp, vp = jnp.pad(kb, pad), jnp.pad(vb, pad)
    kw = jnp.concatenate([kp[:, :, :, o:o + nb] for o in range(n_prev + 1)], axis=4)
    vw = jnp.concatenate([vp[:, :, :, o:o + nb] for o in range(n_prev + 1)], axis=4)
    s = jnp.einsum('brhnqd,brhnkd->brhnqk', qb, kw, preferred_element_type=f32) * (Dh ** -0.5)
    qi = jnp.arange(ATT_BLOCK)[:, None]
    ki = jnp.arange((n_prev + 1) * ATT_BLOCK)[None, :]
    j = n_prev * ATT_BLOCK + qi - ki
    key_idx = jnp.arange(nb)[:, None, None] * ATT_BLOCK + ki[None] - n_prev * ATT_BLOCK
    valid = (j >= 0) & (j <= n_back) & (key_idx >= 0)
    bias = -slopes[:, None, None, None] * (j * dil).astype(f32)
    s = jnp.where(valid, s + bias, -jnp.inf)
    m = jnp.max(s, axis=-1, keepdims=True)
    p = jnp.exp(s - m)
    l = jnp.sum(p, axis=-1, keepdims=True)
    o = jnp.einsum('brhnqk,brhnkd->brhnqd', p, vw.astype(f32)) / l
    lse = (m + jnp.log(l))[..., 0]
    o = o.reshape(Bsz, dil, H, L, Dh).transpose(0, 3, 1, 2, 4).reshape(Bsz, Sp, H, Dh)
    lse = lse.reshape(Bsz, dil, H, L).transpose(0, 3, 1, 2).reshape(Bsz, Sp, H)
    return o, lse


def _dilated_mix(q, k, v, q_gain):
    Bsz, S, _ = q.shape
    Sp = k.shape[1]
    q = _rms(q.reshape(Bsz, S, N_DIL_GROUPS, DIL_HEADS, ATT_HEAD_DIM), q_gain)
    q = jnp.pad(q, ((0, 0), (0, Sp - S), (0, 0), (0, 0), (0, 0)))
    slopes = _alibi_slopes(N_DIL_GROUPS * DIL_HEADS).reshape(N_DIL_GROUPS, DIL_HEADS)
    outs, lses = [], []
    for gi, (window, dil) in enumerate(DIL_PATTERNS):
        o, lse = _dilated_group(q[:, :, gi], k, v, slopes[gi], window, dil)
        outs.append(o)
        lses.append(lse)
    wts = jax.nn.softmax(jnp.stack(lses), axis=0)
    out = jnp.sum(wts[..., None] * jnp.stack(outs), axis=0)[:, :S]
    return out.reshape(Bsz, S, KV_WIDTH).astype(k.dtype)


def _hier_moe(xt, w_rg, w_re, w_gate, w_up, w_down):
    f32 = jnp.float32
    T, D = xt.shape
    gl = jnp.einsum('td,dg->tg', xt, w_rg, preferred_element_type=f32)
    g_sel = jnp.argmax(gl, axis=-1)
    p_group = jnp.take_along_axis(jax.nn.softmax(gl, axis=-1), g_sel[:, None], axis=-1)
    el = jnp.einsum('td,de->te', xt, w_re, preferred_element_type=f32).reshape(T, N_GROUPS, EXPERTS_PER_GROUP)
    el = jnp.take_along_axis(el, g_sel[:, None, None], axis=1)[:, 0]
    top_v, top_i = lax.top_k(el, TOP_K)
    gate = p_group * jax.nn.softmax(top_v, axis=-1)
    expert = (g_sel[:, None] * EXPERTS_PER_GROUP + top_i).reshape(-1).astype(jnp.int32)
    n_assign = T * TOP_K
    tok = jnp.arange(n_assign, dtype=jnp.int32) // TOP_K
    order = jnp.argsort(expert)
    e_sorted = expert[order]
    counts = jnp.bincount(expert, length=N_EXPERTS)
    starts = jnp.cumsum(counts) - counts
    padded = (counts + MOE_BLOCK - 1) // MOE_BLOCK * MOE_BLOCK
    p_end = jnp.cumsum(padded)
    p_start = p_end - padded
    dest = p_start[e_sorted] + jnp.arange(n_assign, dtype=jnp.int32) - starts[e_sorted]
    n_blk = -(-n_assign // MOE_BLOCK) + N_EXPERTS
    slot_tok = jnp.zeros((n_blk * MOE_BLOCK,), jnp.int32).at[dest].set(tok[order])
    slot_gate = jnp.zeros((n_blk * MOE_BLOCK,), f32).at[dest].set(gate.reshape(-1)[order])
    blk_expert = jnp.minimum(jnp.searchsorted(p_end, jnp.arange(n_blk) * MOE_BLOCK, side='right'), N_EXPERTS - 1)

    def expert_block(args):
        idx, e = args
        xb = xt[idx]
        hb = jax.nn.silu(xb @ w_gate[e]) * (xb @ w_up[e])
        return hb @ w_down[e]

    ys = lax.map(expert_block, (slot_tok.reshape(n_blk, MOE_BLOCK), blk_expert))
    out = jnp.zeros((T, D), f32).at[slot_tok].add(ys.reshape(-1, D).astype(f32) * slot_gate[:, None])
    return out.astype(xt.dtype)


def setup_inputs(seed: int = 0) -> dict:
    key = jax.random.key(seed)
    keys = iter(jax.random.split(key, 48))
    f32 = jnp.float32
    D = D_MODEL
    RW = RWKV_WIDTH
    out_scale = (2.0 * DEPTH) ** -0.5

    def nrm(shape, scale):
        return jax.random.normal(next(keys), shape, f32) * scale

    def gain(shape):
        return 1.0 + nrm(shape, 0.02)

    def unif(shape, lo, hi):
        return jax.random.uniform(next(keys), shape, f32, lo, hi)

    return {
        'x': nrm((BATCH, SEQ, D), 1.0),
        'mem': nrm((BATCH, N_MEM, D), 1.0),
        'a_ln1': gain((N_A, D)),
        'a_w_in': nrm((N_A, D, A_IN), D ** -0.5),
        'a_w_vdown': nrm((N_A - 1, D, LORA_MV), D ** -0.5),
        'a_mu': unif((N_A, A_IN), 0.0, 1.0),
        'a_mu_vres': unif((N_A - 1, LORA_MV), 0.0, 1.0),
        'a_w0': unif((N_A, RW), -6.0, -0.5),
        'a_w_up': nrm((N_A, LORA_DECAY, RW), 0.5 * LORA_DECAY ** -0.5),
        'a_a0': unif((N_A, RW), -1.0, 1.0),
        'a_a_up': nrm((N_A, LORA_AAA, RW), LORA_AAA ** -0.5),
        'a_g_up': nrm((N_A, LORA_GATE, RW), LORA_GATE ** -0.5),
        'a_v0': unif((N_A - 1, RW), -1.0, 1.0),
        'a_v_up': nrm((N_A - 1, LORA_MV, RW), LORA_MV ** -0.5),
        'a_k_k': 0.85 + nrm((N_A, RW), 0.05),
        'a_k_a': 1.0 + nrm((N_A, RW), 0.05),
        'a_r_k': nrm((N_A, RWKV_HEADS, RWKV_HEAD), 0.3),
        'a_lnx_g': gain((N_A, RW)),
        'a_lnx_b': nrm((N_A, RW), 0.02),
        'a_w_out': nrm((N_A, RW + MEM_WIDTH, D), (RW + MEM_WIDTH) ** -0.5 * out_scale),
        'b_ln1': gain((N_B, D)),
        'b_w_in': nrm((N_B, D, DIL_Q_WIDTH + MEM_WIDTH), D ** -0.5),
        'b_q_norm': gain((N_B, ATT_HEAD_DIM)),
        'b_w_out': nrm((N_B, KV_WIDTH + MEM_WIDTH, D), (KV_WIDTH + MEM_WIDTH) ** -0.5 * out_scale),
        's_kv_norm': gain((D,)),
        's_w_kv': nrm((D, 2 * KV_WIDTH), D ** -0.5),
        's_k_norm': gain((ATT_HEAD_DIM,)),
        'm_norm': gain((DEPTH, D)),
        'm_w_kv': nrm((DEPTH, D, 2 * MEM_WIDTH), D ** -0.5),
        'm_q_norm': gain((DEPTH, MEM_HEAD_DIM)),
        'm_k_norm': gain((DEPTH, MEM_HEAD_DIM)),
        'moe_ln': gain((DEPTH, D)),
        'moe_router_group': nrm((DEPTH, D, N_GROUPS), D ** -0.5),
        'moe_router_expert': nrm((DEPTH, D, N_EXPERTS), D ** -0.5),
        'moe_w_gate': nrm((DEPTH, N_EXPERTS, D, D_FF), D ** -0.5),
        'moe_w_up': nrm((DEPTH, N_EXPERTS, D, D_FF), D ** -0.5),
        'moe_w_down': nrm((DEPTH, N_EXPERTS, D_FF, D), D_FF ** -0.5 * out_scale),
    }


def reference(x, mem, a_ln1, a_w_in, a_w_vdown, a_mu, a_mu_vres, a_w0, a_w_up, a_a0, a_a_up, a_g_up, a_v0, a_v_up, a_k_k, a_k_a, a_r_k, a_lnx_g, a_lnx_b, a_w_out, b_ln1, b_w_in, b_q_norm, b_w_out, s_kv_norm, s_w_kv, s_k_norm, m_norm, m_w_kv, m_q_norm, m_k_norm, moe_ln, moe_router_group, moe_router_expert, moe_w_gate, moe_w_up, moe_w_down):
    Bsz, S, D = x.shape
    s_pad = -(-S // DIL_SPAN) * DIL_SPAN
    h = x
    v_first = None
    shared_kv = None
    for l in range(DEPTH):
        mk, mv = _mem_kv(mem, m_norm[l], m_w_kv[l], m_k_norm[l])
        if l < N_A:
            i = l
            if i == 0:
                w_in, mu = a_w_in[0], a_mu[0]
            else:
                w_in = jnp.concatenate([a_w_in[i], a_w_vdown[i - 1]], axis=1)
                mu = jnp.concatenate([a_mu[i], a_mu_vres[i - 1]])
            y = _token_shift(jnp.einsum('bsd,dc->bsc', _rms(h, a_ln1[i]), w_in), mu)
            r, k, v, xw, xa, xg, qm, xv = jnp.split(y, A_SPLITS, axis=-1)
            has_vres = i > 0
            mix, v_i = _rwkv7_mix(r, k, v, xw, xa, xg, xv if has_vres else None, v_first,
                                  a_w0[i], a_w_up[i], a_a0[i], a_a_up[i], a_g_up[i],
                                  a_v0[i - 1] if has_vres else None, a_v_up[i - 1] if has_vres else None,
                                  a_k_k[i], a_k_a[i], a_r_k[i], a_lnx_g[i], a_lnx_b[i])
            if i == 0:
                v_first = v_i
            heads_out = jnp.concatenate([mix.astype(h.dtype), _mem_attend(qm, mk, mv, m_q_norm[l]).astype(h.dtype)], axis=-1)
            h = h + jnp.einsum('bsc,cd->bsd', heads_out, a_w_out[i])
        else:
            j = l - N_A
            if shared_kv is None:
                shared_kv = _shared_kv(h, s_kv_norm, s_w_kv, s_k_norm, s_pad)
            ks, vs = shared_kv
            y = jnp.einsum('bsd,dc->bsc', _rms(h, b_ln1[j]), b_w_in[j])
            qd, qm = jnp.split(y, [DIL_Q_WIDTH], axis=-1)
            heads_out = jnp.concatenate([_dilated_mix(qd, ks, vs, b_q_norm[j]).astype(h.dtype), _mem_attend(qm, mk, mv, m_q_norm[l]).astype(h.dtype)], axis=-1)
            h = h + jnp.einsum('bsc,cd->bsd', heads_out, b_w_out[j])
        moe_out = _hier_moe(_rms(h, moe_ln[l]).reshape(Bsz * S, D), moe_router_group[l], moe_router_expert[l], moe_w_gate[l], moe_w_up[l], moe_w_down[l])
        h = h + moe_out.reshape(Bsz, S, D)
    return h
```

```python
import functools

import jax
import jax.numpy as jnp
import numpy as np
from jax import lax
from jax.experimental import pallas as pl
from jax.experimental.pallas import tpu as pltpu

F32 = jnp.float32
BF16 = jnp.bfloat16

D_MODEL = 4096
RWKV_HEAD = 64
RWKV_WIDTH = 3072
LORA_DECAY = 128
LORA_AAA = 128
LORA_MV = 96
LORA_GATE = 480
LNX_EPS = 64e-5
MEM_HEADS = 4
MEM_WIDTH = 1024
MEM_HEAD_DIM = 256
ATT_HEAD_DIM = 128
DIL_PATTERNS = ((128, 1), (512, 4), (2048, 16))
N_DIL_GROUPS = 3
DIL_HEADS = 8
DIL_Q_WIDTH = 3072
KV_WIDTH = 1024
ATT_BLOCK = 128
N_GROUPS = 4
EXPERTS_PER_GROUP = 8
N_EXPERTS = 32
TOP_K = 2
D_FF = 384
NORM_EPS = 1e-6
N_A = 2

LANES = 128
VMEM_LIMIT_BYTES = 56 * 1024 * 1024

A_OFF_R = 0
A_OFF_K = RWKV_WIDTH
A_OFF_V = 2 * RWKV_WIDTH
A_OFF_QM = 3 * RWKV_WIDTH
A_OFF_XW = A_OFF_QM + MEM_WIDTH
A_OFF_XA = A_OFF_XW + LORA_DECAY
A_OFF_XG = A_OFF_XA + LORA_AAA
A_XG_PAD = 512
A_OFF_XV = A_OFF_XG + A_XG_PAD
A_XV_PAD = 128
A_IN_PAD = 11264

SCAN_CHUNK = 64
SCAN_TBLOCK = 512
MOE_ROWS = 256

_NN = (((1,), (0,)), ((), ()))
_NT = (((1,), (1,)), ((), ()))


def _cparams(*sem):
    return pltpu.CompilerParams(dimension_semantics=sem, vmem_limit_bytes=VMEM_LIMIT_BYTES)


def _rms_kernel(x_ref, g_ref, *o_refs):
    x = x_ref[...]
    y = x * lax.rsqrt(jnp.mean(x * x, axis=-1, keepdims=True) + NORM_EPS) * g_ref[...]
    for o_ref in o_refs:
        o_ref[...] = y.astype(o_ref.dtype)


def _rmsnorm(x, g, dtypes, tm=256):
    T, D = x.shape
    tm = min(tm, T)
    outs = pl.pallas_call(
        _rms_kernel,
        grid=(T // tm,),
        in_specs=[pl.BlockSpec((tm, D), lambda i: (i, 0)), pl.BlockSpec((1, D), lambda i: (0, 0))],
        out_specs=[pl.BlockSpec((tm, D), lambda i: (i, 0)) for _ in dtypes],
        out_shape=[jax.ShapeDtypeStruct((T, D), dt) for dt in dtypes],
        compiler_params=_cparams("parallel"),
        name="rmsnorm",
    )(x, g.reshape(1, D))
    return outs


def _mm_kernel(x_ref, w_ref, o_ref):
    o_ref[...] = jnp.dot(x_ref[...], w_ref[...], preferred_element_type=F32)


def _mm_res_kernel(x_ref, w_ref, r_ref, o_ref):
    o_ref[...] = r_ref[...] + jnp.dot(x_ref[...], w_ref[...], preferred_element_type=F32)


def _mm3_kernel(x_ref, w_ref, o_ref):
    x = x_ref[...]
    w = w_ref[...]
    xh = x.astype(BF16)
    xl = (x - xh.astype(F32)).astype(BF16)
    wh = w.astype(BF16)
    wl = (w - wh.astype(F32)).astype(BF16)
    d = lambda a, b: jnp.dot(a, b, preferred_element_type=F32)
    o_ref[...] = d(xh, wh) + (d(xh, wl) + d(xl, wh))


def _matmul(x, w, res=None, tm=1024, tn=512, split3=False):
    M, K = x.shape
    N = w.shape[1]
    tm = min(tm, M)
    tn = min(tn, N)
    assert M % tm == 0 and N % tn == 0
    in_specs = [pl.BlockSpec((tm, K), lambda i, j: (i, 0)), pl.BlockSpec((K, tn), lambda i, j: (0, j))]
    args = [x, w]
    if split3:
        body = _mm3_kernel
    elif res is None:
        body = _mm_kernel
    else:
        body = _mm_res_kernel
        in_specs.append(pl.BlockSpec((tm, tn), lambda i, j: (i, j)))
        args.append(res)
    return pl.pallas_call(
        body,
        grid=(M // tm, N // tn),
        in_specs=in_specs,
        out_specs=pl.BlockSpec((tm, tn), lambda i, j: (i, j)),
        out_shape=jax.ShapeDtypeStruct((M, N), F32),
        compiler_params=_cparams("parallel", "parallel"),
        name="matmul",
    )(*args)


def _split(x):
    hi = x.astype(BF16)
    lo = (x - hi.astype(F32)).astype(BF16)
    return hi, lo


def _dot3(a, b, dims=_NN):
    ah, al = a
    bh, bl = b
    d = lambda x, y: lax.dot_general(x, y, dims, preferred_element_type=F32)
    return d(ah, bh) + (d(ah, bl) + d(al, bh))


def _rwkv_scan_kernel(r_ref, lw_ref, k_ref, v_ref, a_ref, b_ref, o_ref, s_ref, *, n_chunks):
    C = SCAN_CHUNK
    H2 = 2 * C

    @pl.when(pl.program_id(1) == 0)
    def _():
        s_ref[...] = jnp.zeros_like(s_ref)

    lane = lax.broadcasted_iota(jnp.int32, (C, LANES), 1)
    head0 = lane < RWKV_HEAD
    lane2 = lax.broadcasted_iota(jnp.int32, (H2, LANES), 1)
    row2 = lax.broadcasted_iota(jnp.int32, (H2, LANES), 0)
    head0_2 = lane2 < RWKV_HEAD
    blockdiag = (row2 < C) == (lane2 < RWKV_HEAD)
    tri_r = lax.broadcasted_iota(jnp.int32, (C, C), 0)
    tri_c = lax.broadcasted_iota(jnp.int32, (C, C), 1)
    ltri = jnp.where(tri_r >= tri_c, 1.0, 0.0).astype(BF16)
    arow = lax.broadcasted_iota(jnp.int32, (H2, 2 * H2), 0)
    acol = lax.broadcasted_iota(jnp.int32, (H2, 2 * H2), 1)
    a_t = arow & (C - 1)
    a_j = acol & (C - 1)
    amask = a_t + jnp.where(arow < C, 0, 1) > a_j

    def stack2(x):
        z = jnp.zeros_like(x)
        return jnp.concatenate([jnp.where(head0, x, z), jnp.where(head0, z, x)], axis=0)

    def chunk(ci, carry):
        sl = pl.ds(pl.multiple_of(ci * C, C), C)
        r = r_ref[sl, :]
        lw = lw_ref[sl, :]
        k = k_ref[sl, :]
        v = v_ref[sl, :]
        a = a_ref[sl, :]
        b = b_ref[sl, :]

        l1 = lw.astype(BF16)
        rem = lw - l1.astype(F32)
        l2 = rem.astype(BF16)
        l3 = (rem - l2.astype(F32)).astype(BF16)
        dd = lambda x, y: jnp.dot(x, y, preferred_element_type=F32)
        cum = dd(ltri, l1) + (dd(ltri, l2) + dd(ltri, l3))
        p_inc = jnp.exp(cum)
        p_exc = jnp.exp(cum - lw)
        p_inv = jnp.exp(-cum)
        at = a * p_exc
        rt = r * p_inc
        bt = b * p_inv
        kt = k * p_inv
        p_tot = p_inc[C - 1:C, :]

        s = s_ref[...]
        x_ar = _split(jnp.concatenate([at, rt], axis=0))
        w_bk = _split(jnp.concatenate([stack2(bt), stack2(kt)], axis=0))
        amat = jnp.where(amask, _dot3(x_ar, w_bk, _NT), 0.0)
        xs = _dot3(x_ar, _split(s), _NT)

        n_ab = amat[:C, :H2]
        a_ak = amat[:C, H2:]
        a_r = amat[C:, :]
        v2 = stack2(v)
        x = xs[:C] + _dot3(_split(a_ak), _split(v2))

        npow = n_ab
        n_steps = C.bit_length() - 1
        for i in range(n_steps):
            np_split = _split(npow)
            x = x + _dot3(np_split, _split(stack2(x)))
            if i + 1 < n_steps:
                npow = _dot3(np_split, _split(stack2(npow)))
        u = x

        uv = jnp.concatenate([stack2(u), v2], axis=0)
        y = xs[C:] + _dot3(_split(a_r), _split(uv))
        o_ref[sl, :] = y

        uv_t = jnp.concatenate([u, v], axis=0).T
        bk = jnp.concatenate([bt, kt], axis=0)
        ds = _dot3(_split(uv_t), _split(bk))
        s_ref[...] = (s + jnp.where(blockdiag, ds, 0.0)) * p_tot
        return carry

    lax.fori_loop(0, n_chunks, chunk, 0)


def _rwkv_scan(r, lw, k, v, a, b):
    T, W = r.shape
    tb = min(SCAN_TBLOCK, T)
    assert T % tb == 0 and tb % SCAN_CHUNK == 0 and W % LANES == 0
    spec = pl.BlockSpec((tb, LANES), lambda p, t: (t, p))
    return pl.pallas_call(
        functools.partial(_rwkv_scan_kernel, n_chunks=tb // SCAN_CHUNK),
        grid=(W // LANES, T // tb),
        in_specs=[spec] * 6,
        out_specs=spec,
        out_shape=jax.ShapeDtypeStruct((T, W), F32),
        scratch_shapes=[pltpu.VMEM((LANES, LANES), F32)],
        compiler_params=_cparams("parallel", "arbitrary"),
        name="rwkv7_scan",
    )(r, lw, k, v, a, b)


def _mem_attn_kernel(q_ref, k_ref, v_ref, g_ref, o_ref):
    scale = MEM_HEAD_DIM ** -0.5
    for h in range(MEM_HEADS):
        hs = slice(h * MEM_HEAD_DIM, (h + 1) * MEM_HEAD_DIM)
        q = q_ref[:, hs]
        qn = q * lax.rsqrt(jnp.mean(q * q, axis=-1, keepdims=True) + NORM_EPS) * g_ref[...]
        s = lax.dot_general(qn.astype(BF16), k_ref[:, hs], _NT, preferred_element_type=F32) * scale
        m = jnp.max(s, axis=-1, keepdims=True)
        p = jnp.exp(s - m)
        l = jnp.sum(p, axis=-1, keepdims=True)
        o = jnp.dot(p.astype(BF16), v_ref[:, hs], preferred_element_type=F32)
        o_ref[:, hs] = o / l


def _mem_attend(y, col_block, mk, mv, q_gain, tm=512):
    T = y.shape[0]
    M = mk.shape[0]
    tm = min(tm, T)
    return pl.pallas_call(
        _mem_attn_kernel,
        grid=(T // tm,),
        in_specs=[
            pl.BlockSpec((tm, MEM_WIDTH), lambda i: (i, col_block)),
            pl.BlockSpec((M, MEM_WIDTH), lambda i: (0, 0)),
            pl.BlockSpec((M, MEM_WIDTH), lambda i: (0, 0)),
            pl.BlockSpec((1, MEM_HEAD_DIM), lambda i: (0, 0)),
        ],
        out_specs=pl.BlockSpec((tm, MEM_WIDTH), lambda i: (i, 0)),
        out_shape=jax.ShapeDtypeStruct((T, MEM_WIDTH), F32),
        compiler_params=_cparams("parallel"),
        name="mem_attend",
    )(y, mk, mv, q_gain.reshape(1, MEM_HEAD_DIM))


def _dil_attn_kernel(q_ref, kp_ref, kc_ref, vp_ref, vc_ref, g_ref, o_ref, lse_ref, *, dil, slopes):
    n = pl.program_id(1)
    B = ATT_BLOCK
    n_back = B
    scale = ATT_HEAD_DIM ** -0.5
    qi = lax.broadcasted_iota(jnp.int32, (B, 2 * B), 0)
    ki = lax.broadcasted_iota(jnp.int32, (B, 2 * B), 1)
    j = B + qi - ki
    valid = (j >= 0) & (j <= n_back) & ((ki >= B) | (n > 0))
    dist = (j * dil).astype(F32)
    for h in range(DIL_HEADS):
        hs = slice(h * ATT_HEAD_DIM, (h + 1) * ATT_HEAD_DIM)
        q = q_ref[:, hs]
        qn = q * lax.rsqrt(jnp.mean(q * q, axis=-1, keepdims=True) + NORM_EPS) * g_ref[...]
        kcat = jnp.concatenate([kp_ref[:, hs], kc_ref[:, hs]], axis=0)
        vcat = jnp.concatenate([vp_ref[:, hs], vc_ref[:, hs]], axis=0)
        s = lax.dot_general(qn.astype(BF16), kcat, _NT, preferred_element_type=F32) * scale
        s = jnp.where(valid, s - slopes[h] * dist, -1e30)
        m = jnp.max(s, axis=-1, keepdims=True)
        p = jnp.exp(s - m)
        l = jnp.sum(p, axis=-1, keepdims=True)
        o = jnp.dot(p.astype(BF16), vcat, preferred_element_type=F32)
        o_ref[:, hs] = o / l
        lse_ref[:, hs] = jnp.broadcast_to(m + jnp.log(l), (B, ATT_HEAD_DIM))


def _dilated_group(y, gi, dil, k, v, q_gain):
    T, W = y.shape
    L = T // dil
    nb = L // ATT_BLOCK
    wq = W // KV_WIDTH
    y2 = y.reshape(L, dil * W)
    k2 = k.reshape(L, dil * KV_WIDTH)
    v2 = v.reshape(L, dil * KV_WIDTH)
    n_heads_total = N_DIL_GROUPS * DIL_HEADS
    slopes = tuple(float(2.0 ** (-8.0 * (gi * DIL_HEADS + h + 1) / n_heads_total)) for h in range(DIL_HEADS))
    blk = (ATT_BLOCK, KV_WIDTH)
    cur = lambda c, n: (n, c)
    prev = lambda c, n: (jnp.maximum(n - 1, 0), c)
    o, lse = pl.pallas_call(
        functools.partial(_dil_attn_kernel, dil=dil, slopes=slopes),
        grid=(dil, nb),
        in_specs=[
            pl.BlockSpec(blk, lambda c, n: (n, c * wq + gi)),
            pl.BlockSpec(blk, prev), pl.BlockSpec(blk, cur),
            pl.BlockSpec(blk, prev), pl.BlockSpec(blk, cur),
            pl.BlockSpec((1, ATT_HEAD_DIM), lambda c, n: (0, 0)),
        ],
        out_specs=[pl.BlockSpec(blk, cur), pl.BlockSpec(blk, cur)],
        out_shape=[jax.ShapeDtypeStruct((L, dil * KV_WIDTH), F32)] * 2,
        compiler_params=_cparams("parallel", "parallel"),
        name=f"dilated_attn_d{dil}",
    )(y2, k2, k2, v2, v2, q_gain.reshape(1, ATT_HEAD_DIM))
    return o.reshape(T, KV_WIDTH), lse.reshape(T, KV_WIDTH)


def _merge_kernel(o0, l0, o1, l1, o2, l2, out_ref):
    a0, a1, a2 = l0[...], l1[...], l2[...]
    m = jnp.maximum(jnp.maximum(a0, a1), a2)
    w0, w1, w2 = jnp.exp(a0 - m), jnp.exp(a1 - m), jnp.exp(a2 - m)
    out_ref[...] = (w0 * o0[...] + w1 * o1[...] + w2 * o2[...]) / (w0 + w1 + w2)


def _merge_groups(parts, tm=512):
    T, W = parts[0][0].shape
    tm = min(tm, T)
    spec = pl.BlockSpec((tm, W), lambda i: (i, 0))
    flat = [t for pr in parts for t in pr]
    return pl.pallas_call(
        _merge_kernel,
        grid=(T // tm,),
        in_specs=[spec] * 6,
        out_specs=spec,
        out_shape=jax.ShapeDtypeStruct((T, W), F32),
        compiler_params=_cparams("parallel"),
        name="dilated_merge",
    )(*flat)


def _moe_kernel(be_ref, bv_ref, x_ref, wg_ref, wu_ref, wd_ref, g_ref, o_ref):
    i = pl.program_id(0)

    @pl.when(bv_ref[i] > 0)
    def _():
        x = x_ref[...]
        hg = jnp.dot(x, wg_ref[...], preferred_element_type=F32)
        hu = jnp.dot(x, wu_ref[...], preferred_element_type=F32)
        hb = hg * jax.nn.sigmoid(hg) * hu
        y = jnp.dot(hb.astype(BF16), wd_ref[...], preferred_element_type=F32)
        o_ref[...] = y * g_ref[...]

    @pl.when(bv_ref[i] == 0)
    def _():
        o_ref[...] = jnp.zeros_like(o_ref)


def _moe_experts(xg, slot_gate, blk_expert, blk_valid, w_gate, w_up, w_down):
    NR, D = xg.shape
    R = MOE_ROWS
    n_blk = NR // R
    F = w_gate.shape[2]
    grid_spec = pltpu.PrefetchScalarGridSpec(
        num_scalar_prefetch=2,
        grid=(n_blk,),
        in_specs=[
            pl.BlockSpec((R, D), lambda i, be, bv: (i, 0)),
            pl.BlockSpec((None, D, F), lambda i, be, bv: (be[i], 0, 0)),
            pl.BlockSpec((None, D, F), lambda i, be, bv: (be[i], 0, 0)),
            pl.BlockSpec((None, F, D), lambda i, be, bv: (be[i], 0, 0)),
            pl.BlockSpec((R, 1), lambda i, be, bv: (i, 0)),
        ],
        out_specs=pl.BlockSpec((R, D), lambda i, be, bv: (i, 0)),
    )
    return pl.pallas_call(
        _moe_kernel,
        grid_spec=grid_spec,
        out_shape=jax.ShapeDtypeStruct((NR, D), F32),
        compiler_params=_cparams("arbitrary"),
        name="moe_experts",
    )(blk_expert, blk_valid, xg, w_gate, w_up, w_down, slot_gate.reshape(NR, 1))


def _hier_moe(h, ln_g, w_rg, w_re, w_gate, w_up, w_down):
    T, D = h.shape
    R = MOE_ROWS
    xn32, xn16 = _rmsnorm(h, ln_g, (F32, BF16))
    n_rout = N_GROUPS + N_EXPERTS
    w_r = jnp.pad(jnp.concatenate([w_rg, w_re], axis=1), ((0, 0), (0, LANES - n_rout)))
    logits = _matmul(xn32, w_r, split3=True, tm=512)
    gl = logits[:, :N_GROUPS]
    g_sel = jnp.argmax(gl, axis=-1)
    p_group = jnp.take_along_axis(jax.nn.softmax(gl, axis=-1), g_sel[:, None], axis=-1)
    el = logits[:, N_GROUPS:n_rout].reshape(T, N_GROUPS, EXPERTS_PER_GROUP)
    el = jnp.take_along_axis(el, g_sel[:, None, None], axis=1)[:, 0]
    top_v, top_i = lax.top_k(el, TOP_K)
    gate = p_group * jax.nn.softmax(top_v, axis=-1)
    expert = (g_sel[:, None] * EXPERTS_PER_GROUP + top_i).reshape(-1).astype(jnp.int32)
    n_assign = T * TOP_K
    tok = jnp.arange(n_assign, dtype=jnp.int32) // TOP_K
    order = jnp.argsort(expert)
    e_sorted = expert[order]
    counts = jnp.bincount(expert, length=N_EXPERTS)
    starts = jnp.cumsum(counts) - counts
    padded = (counts + R - 1) // R * R
    p_end = jnp.cumsum(padded)
    p_start = p_end - padded
    dest = (p_start[e_sorted] + jnp.arange(n_assign, dtype=jnp.int32) - starts[e_sorted]).astype(jnp.int32)
    n_blk = -(-n_assign // R) + N_EXPERTS
    slot_tok = jnp.zeros((n_blk * R,), jnp.int32).at[dest].set(tok[order])
    slot_gate = jnp.zeros((n_blk * R,), F32).at[dest].set(gate.reshape(-1)[order])
    blk_start = jnp.arange(n_blk, dtype=jnp.int32) * R
    blk_valid = (blk_start < p_end[-1]).astype(jnp.int32)
    last_valid = jnp.maximum(p_end[-1] // R - 1, 0)
    blk_expert = jnp.minimum(jnp.searchsorted(p_end, blk_start, side='right'), N_EXPERTS - 1).astype(jnp.int32)
    blk_expert = jnp.where(blk_valid > 0, blk_expert, blk_expert[last_valid])
    pos = jnp.zeros((n_assign,), jnp.int32).at[order].set(dest)

    xg = jnp.take(xn16, slot_tok, axis=0)
    ys = _moe_experts(xg, slot_gate, blk_expert, blk_valid,
                      w_gate.astype(BF16), w_up.astype(BF16), w_down.astype(BF16))
    pos2 = pos.reshape(T, TOP_K)
    return h + (jnp.take(ys, pos2[:, 0], axis=0) + jnp.take(ys, pos2[:, 1], axis=0))


def _mem_kv(mem, g, w_kv, k_gain):
    M = mem.shape[0]
    (mn,) = _rmsnorm(mem, g, (BF16,))
    kv = _matmul(mn, w_kv.astype(BF16))
    k = kv[:, :MEM_WIDTH].reshape(M, MEM_HEADS, MEM_HEAD_DIM)
    k = k * lax.rsqrt(jnp.mean(k * k, axis=-1, keepdims=True) + NORM_EPS) * k_gain
    return k.reshape(M, MEM_WIDTH).astype(BF16), kv[:, MEM_WIDTH:].astype(BF16)


def _pad_cols(w, n):
    return jnp.pad(w, ((0, 0), (0, n - w.shape[1])))


def _a_layer_weights(w_in, mu, w_vdown, mu_vres):
    D = w_in.shape[0]
    c = np.cumsum([RWKV_WIDTH, RWKV_WIDTH, RWKV_WIDTH, LORA_DECAY, LORA_AAA, LORA_GATE, MEM_WIDTH])
    rkv, xw, xa, xg, qm = (slice(0, c[2]), slice(c[2], c[3]), slice(c[3], c[4]), slice(c[4], c[5]), slice(c[5], c[6]))
    if w_vdown is None:
        w_vdown = jnp.zeros((D, LORA_MV), w_in.dtype)
        mu_vres = jnp.zeros((LORA_MV,), mu.dtype)
    tail = A_IN_PAD - A_OFF_XV
    w = jnp.concatenate([w_in[:, rkv], w_in[:, qm], w_in[:, xw], w_in[:, xa],
                         _pad_cols(w_in[:, xg], A_XG_PAD), _pad_cols(w_vdown, tail)], axis=1)
    m = jnp.concatenate([mu[rkv], mu[qm], mu[xw], mu[xa],
                         jnp.pad(mu[xg], (0, A_XG_PAD - LORA_GATE)), jnp.pad(mu_vres, (0, tail - LORA_MV))])
    return w.astype(BF16), m


def _rwkv_layer(h, mk, mv, mq_gain, ln1, w_in, mu, w_vdown, mu_vres, v_first, w0, w_up, a0, a_up, g_up,
                v0, v_up, k_k, k_a, r_k, lnx_g, lnx_b, w_out):
    T, D = h.shape
    H, N = RWKV_WIDTH // RWKV_HEAD, RWKV_HEAD
    (xn,) = _rmsnorm(h, ln1, (BF16,))
    w_pad, mu_pad = _a_layer_weights(w_in, mu, w_vdown, mu_vres)
    y = _matmul(xn, w_pad)
    y_prev = jnp.pad(y, ((1, 0), (0, 0)))[:-1]
    y = y + mu_pad * (y_prev - y)

    r = y[:, A_OFF_R:A_OFF_R + RWKV_WIDTH]
    k = y[:, A_OFF_K:A_OFF_K + RWKV_WIDTH]
    v = y[:, A_OFF_V:A_OFF_V + RWKV_WIDTH]
    xw = jnp.tanh(y[:, A_OFF_XW:A_OFF_XW + LORA_DECAY]).astype(BF16)
    xa = y[:, A_OFF_XA:A_OFF_XA + LORA_AAA].astype(BF16)
    xg = jax.nn.sigmoid(y[:, A_OFF_XG:A_OFF_XG + A_XG_PAD]).astype(BF16)
    pad_rows = lambda w, n: jnp.pad(w, ((0, n - w.shape[0]), (0, 0))).astype(BF16)

    w_log = -jax.nn.softplus(-(w0 + _matmul(xw, w_up.astype(BF16)))) - 0.5
    lw = -jnp.exp(w_log)
    a = jax.nn.sigmoid(a0 + _matmul(xa, a_up.astype(BF16)))
    g = _matmul(xg, pad_rows(g_up, A_XG_PAD))
    if v_first is not None:
        xv = y[:, A_OFF_XV:A_OFF_XV + A_XV_PAD].astype(BF16)
        mix = jax.nn.sigmoid(v0 + _matmul(xv, pad_rows(v_up, A_XV_PAD)))
        v = v + (v_first - v) * mix
    heads = lambda t: t.reshape(T, H, N)
    kk = heads(k * k_k)
    kk = kk / jnp.maximum(jnp.sqrt(jnp.sum(kk * kk, axis=-1, keepdims=True)), 1e-12)
    kk = kk.reshape(T, RWKV_WIDTH)
    k = k * (1.0 + (a - 1.0) * k_a)

    ys = _rwkv_scan(r, lw, k, v, -kk, kk * a)

    yh = heads(ys)
    mean = jnp.mean(yh, axis=-1, keepdims=True)
    var = jnp.mean(jnp.square(yh - mean), axis=-1, keepdims=True)
    yn = ((yh - mean) * lax.rsqrt(var + LNX_EPS)).reshape(T, RWKV_WIDTH) * lnx_g + lnx_b
    bonus = jnp.sum(heads(r) * heads(k) * r_k, axis=-1, keepdims=True) * heads(v)
    mix_out = (yn + bonus.reshape(T, RWKV_WIDTH)) * g

    mem_out = _mem_attend(y, A_OFF_QM // MEM_WIDTH, mk, mv, mq_gain)
    heads_out = jnp.concatenate([mix_out, mem_out], axis=-1).astype(BF16)
    return _matmul(heads_out, w_out.astype(BF16), res=h), v


def _dilated_layer(h, mk, mv, mq_gain, ln1, w_in, q_gain, ks, vs, w_out):
    (xn,) = _rmsnorm(h, ln1, (BF16,))
    y = _matmul(xn, w_in.astype(BF16))
    parts = [_dilated_group(y, gi, dil, ks, vs, q_gain) for gi, (_, dil) in enumerate(DIL_PATTERNS)]
    att = _merge_groups(parts)
    mem_out = _mem_attend(y, DIL_Q_WIDTH // MEM_WIDTH, mk, mv, mq_gain)
    heads_out = jnp.concatenate([att, mem_out], axis=-1).astype(BF16)
    return _matmul(heads_out, w_out.astype(BF16), res=h)


def _shared_kv(h, g, w_kv, k_gain):
    T = h.shape[0]
    (xn,) = _rmsnorm(h, g, (BF16,))
    kv = _matmul(xn, w_kv.astype(BF16))
    k = kv[:, :KV_WIDTH].reshape(T, DIL_HEADS, ATT_HEAD_DIM)
    k = k * lax.rsqrt(jnp.mean(k * k, axis=-1, keepdims=True) + NORM_EPS) * k_gain
    return k.reshape(T, KV_WIDTH).astype(BF16), kv[:, KV_WIDTH:].astype(BF16)


def kernel(x, mem, a_ln1, a_w_in, a_w_vdown, a_mu, a_mu_vres, a_w0, a_w_up, a_a0, a_a_up, a_g_up, a_v0, a_v_up, a_k_k, a_k_a, a_r_k, a_lnx_g, a_lnx_b, a_w_out, b_ln1, b_w_in, b_q_norm, b_w_out, s_kv_norm, s_w_kv, s_k_norm, m_norm, m_w_kv, m_q_norm, m_k_norm, moe_ln, moe_router_group, moe_router_expert, moe_w_gate, moe_w_up, moe_w_down):
    Bsz, S, D = x.shape
    assert Bsz == 1 and S % (max(d for _, d in DIL_PATTERNS) * ATT_BLOCK) == 0
    depth = moe_ln.shape[0]
    h = x.reshape(S, D)
    mem2 = mem.reshape(mem.shape[1], D)
    v_first = None
    shared = None
    for l in range(depth):
        mk, mv = _mem_kv(mem2, m_norm[l], m_w_kv[l], m_k_norm[l])
        if l < N_A:
            i = l
            has_vres = i > 0
            h, v_i = _rwkv_layer(
                h, mk, mv, m_q_norm[l], a_ln1[i], a_w_in[i], a_mu[i],
                a_w_vdown[i - 1] if has_vres else None, a_mu_vres[i - 1] if has_vres else None,
                v_first, a_w0[i], a_w_up[i], a_a0[i], a_a_up[i], a_g_up[i],
                a_v0[i - 1] if has_vres else None, a_v_up[i - 1] if has_vres else None,
                a_k_k[i], a_k_a[i], a_r_k[i].reshape(1, RWKV_WIDTH // RWKV_HEAD, RWKV_HEAD),
                a_lnx_g[i], a_lnx_b[i], a_w_out[i])
            if i == 0:
                v_first = v_i
        else:
            j = l - N_A
            if shared is None:
                shared = _shared_kv(h, s_kv_norm, s_w_kv, s_k_norm)
            h = _dilated_layer(h, mk, mv, m_q_norm[l], b_ln1[j], b_w_in[j], b_q_norm[j], shared[0], shared[1], b_w_out[j])
        h = _hier_moe(h, moe_ln[l], moe_router_group[l], moe_router_expert[l], moe_w_gate[l], moe_w_up[l], moe_w_down[l])
    return h.reshape(Bsz, S, D)
```

```python
import functools

import jax
import jax.numpy as jnp
import numpy as np
from jax import lax
from jax.experimental import pallas as pl
from jax.experimental.pallas import tpu as pltpu

F32 = jnp.float32
BF16 = jnp.bfloat16

D_MODEL = 4096
RWKV_HEAD = 64
RWKV_WIDTH = 3072
LORA_DECAY = 128
LORA_AAA = 128
LORA_MV = 96
LORA_GATE = 480
LNX_EPS = 64e-5
MEM_HEADS = 4
MEM_WIDTH = 1024
MEM_HEAD_DIM = 256
ATT_HEAD_DIM = 128
DIL_PATTERNS = ((128, 1), (512, 4), (2048, 16))
N_DIL_GROUPS = 3
DIL_HEADS = 8
DIL_Q_WIDTH = 3072
KV_WIDTH = 1024
ATT_BLOCK = 128
N_GROUPS = 4
EXPERTS_PER_GROUP = 8
N_EXPERTS = 32
TOP_K = 2
D_FF = 384
NORM_EPS = 1e-6
N_A = 2

LANES = 128
SUBLANES = 8
VMEM_LIMIT_BYTES = 56 * 1024 * 1024

A_OFF_R = 0
A_OFF_K = RWKV_WIDTH
A_OFF_V = 2 * RWKV_WIDTH
A_OFF_QM = 3 * RWKV_WIDTH
A_OFF_XW = A_OFF_QM + MEM_WIDTH
A_OFF_XA = A_OFF_XW + LORA_DECAY
A_OFF_XG = A_OFF_XA + LORA_AAA
A_XG_PAD = 512
A_OFF_XV = A_OFF_XG + A_XG_PAD
A_XV_PAD = 128
A_IN_PAD = 11264

SCAN_CHUNK = 64
SCAN_TBLOCK = 512
SCAN_PAIRS = 4
MOE_ROWS = 256

_NN = (((1,), (0,)), ((), ()))
_NT = (((1,), (1,)), ((), ()))


def _cparams(*sem):
    return pltpu.CompilerParams(dimension_semantics=sem, vmem_limit_bytes=VMEM_LIMIT_BYTES)


def _rms_kernel(x_ref, g_ref, *o_refs):
    x = x_ref[...]
    y = x * lax.rsqrt(jnp.mean(x * x, axis=-1, keepdims=True) + NORM_EPS) * g_ref[...]
    for o_ref in o_refs:
        o_ref[...] = y.astype(o_ref.dtype)


def _rmsnorm(x, g, dtypes, tm=256):
    T, D = x.shape
    tm = min(tm, T)
    outs = pl.pallas_call(
        _rms_kernel,
        grid=(T // tm,),
        in_specs=[pl.BlockSpec((tm, D), lambda i: (i, 0)), pl.BlockSpec((1, D), lambda i: (0, 0))],
        out_specs=[pl.BlockSpec((tm, D), lambda i: (i, 0)) for _ in dtypes],
        out_shape=[jax.ShapeDtypeStruct((T, D), dt) for dt in dtypes],
        compiler_params=_cparams("parallel"),
        name="rmsnorm",
    )(x, g.reshape(1, D))
    return outs


def _mm_kernel(x_ref, w_ref, o_ref):
    o_ref[...] = jnp.dot(x_ref[...], w_ref[...], preferred_element_type=F32)


def _mm_res_kernel(x_ref, w_ref, r_ref, o_ref):
    o_ref[...] = r_ref[...] + jnp.dot(x_ref[...], w_ref[...], preferred_element_type=F32)


def _mm3_kernel(x_ref, w_ref, o_ref):
    x = x_ref[...]
    w = w_ref[...]
    xh = x.astype(BF16)
    xl = (x - xh.astype(F32)).astype(BF16)
    wh = w.astype(BF16)
    wl = (w - wh.astype(F32)).astype(BF16)
    d = lambda a, b: jnp.dot(a, b, preferred_element_type=F32)
    o_ref[...] = d(xh, wh) + (d(xh, wl) + d(xl, wh))


def _matmul(x, w, res=None, tm=1024, tn=512, split3=False):
    M, K = x.shape
    N = w.shape[1]
    tm = min(tm, M)
    tn = min(tn, N)
    assert M % tm == 0 and N % tn == 0
    in_specs = [pl.BlockSpec((tm, K), lambda i, j: (i, 0)), pl.BlockSpec((K, tn), lambda i, j: (0, j))]
    args = [x, w]
    if split3:
        body = _mm3_kernel
    elif res is None:
        body = _mm_kernel
    else:
        body = _mm_res_kernel
        in_specs.append(pl.BlockSpec((tm, tn), lambda i, j: (i, j)))
        args.append(res)
    return pl.pallas_call(
        body,
        grid=(M // tm, N // tn),
        in_specs=in_specs,
        out_specs=pl.BlockSpec((tm, tn), lambda i, j: (i, j)),
        out_shape=jax.ShapeDtypeStruct((M, N), F32),
        compiler_params=_cparams("parallel", "parallel"),
        name="matmul",
    )(*args)


def _split(x):
    hi = x.astype(BF16)
    lo = (x - hi.astype(F32)).astype(BF16)
    return hi, lo


def _dot3(a, b, dims=_NN):
    ah, al = a
    bh, bl = b
    d = lambda x, y: lax.dot_general(x, y, dims, preferred_element_type=F32)
    return d(ah, bh) + (d(ah, bl) + d(al, bh))


def _round_robin(gens):
    outs = [None] * len(gens)
    active = list(range(len(gens)))
    while active:
        for i in list(active):
            try:
                next(gens[i])
            except StopIteration as stop:
                outs[i] = stop.value
                active.remove(i)
    return outs


def _rwkv_scan_kernel(r_ref, lw_ref, k_ref, v_ref, a_ref, b_ref, o_ref, s_ref, *, n_chunks, n_pairs):
    C = SCAN_CHUNK
    H2 = 2 * C

    @pl.when(pl.program_id(1) == 0)
    def _():
        s_ref[...] = jnp.zeros_like(s_ref)

    lane = lax.broadcasted_iota(jnp.int32, (C, LANES), 1)
    head0 = lane < RWKV_HEAD
    trow = lax.broadcasted_iota(jnp.int32, (C, LANES), 0)
    tcol = lane & (C - 1)
    lane2 = lax.broadcasted_iota(jnp.int32, (H2, LANES), 1)
    row2 = lax.broadcasted_iota(jnp.int32, (H2, LANES), 0)
    blockdiag = (row2 < C) == (lane2 < RWKV_HEAD)
    tri_r = lax.broadcasted_iota(jnp.int32, (C, C), 0)
    tri_c = lax.broadcasted_iota(jnp.int32, (C, C), 1)
    ltri = jnp.where(tri_r >= tri_c, 1.0, 0.0).astype(BF16)
    arow = lax.broadcasted_iota(jnp.int32, (H2, 2 * H2), 0)
    acol = lax.broadcasted_iota(jnp.int32, (H2, 2 * H2), 1)
    a_t = arow & (C - 1)
    a_j = acol & (C - 1)
    amask = a_t + jnp.where(arow < C, 0, 1) > a_j
    n_levels = C.bit_length() - 1
    eye2 = jnp.where(trow == tcol, 1.0, 0.0)

    def level_mask(lvl):
        tb_, jb_ = trow >> lvl, tcol >> lvl
        return (tb_ - jb_) * (tb_ & 1) == 1

    def stack2(x):
        z = jnp.zeros_like(x)
        return jnp.concatenate([jnp.where(head0, x, z), jnp.where(head0, z, x)], axis=0)

    def pair_chunk(r, lw, k, v, a, b, s):
        l1 = lw.astype(BF16)
        rem = lw - l1.astype(F32)
        l2 = rem.astype(BF16)
        l3 = (rem - l2.astype(F32)).astype(BF16)
        dd = lambda x, y: jnp.dot(x, y, preferred_element_type=F32)
        cum = dd(ltri, l1) + (dd(ltri, l2) + dd(ltri, l3))
        yield
        p_inc = jnp.exp(cum)
        p_exc = jnp.exp(cum - lw)
        p_inv = jnp.exp(-cum)
        at = a * p_exc
        rt = r * p_inc
        bt = b * p_inv
        kt = k * p_inv
        p_tot = p_inc[C - 1:C, :]

        x_ar = _split(jnp.concatenate([at, rt], axis=0))
        w_bk = _split(jnp.concatenate([stack2(bt), stack2(kt)], axis=0))
        amat = jnp.where(amask, _dot3(x_ar, w_bk, _NT), 0.0)
        xs = _dot3(x_ar, _split(s), _NT)
        yield

        n_ab = amat[:C, :H2]
        a_ak = amat[:C, H2:]
        a_r = amat[C:, :]
        v2 = stack2(v)
        rhs = xs[:C] + _dot3(_split(a_ak), _split(v2))
        yield

        tinv = eye2 + jnp.where(level_mask(0), n_ab, 0.0)
        for lvl in range(1, n_levels):
            t_split = _split(tinv)
            t_bd = _split(stack2(tinv))
            nt = _dot3(_split(jnp.where(level_mask(lvl), n_ab, 0.0)), t_bd)
            yield
            tinv = tinv + _dot3(t_split, _split(stack2(nt)))
            yield
        u = _dot3(_split(tinv), _split(stack2(rhs)))
        yield

        uv = jnp.concatenate([stack2(u), v2], axis=0)
        y = xs[C:] + _dot3(_split(a_r), _split(uv))
        yield

        uv_t = jnp.concatenate([u, v], axis=0).T
        bk = jnp.concatenate([bt, kt], axis=0)
        ds = _dot3(_split(uv_t), _split(bk))
        return y, (s + jnp.where(blockdiag, ds, 0.0)) * p_tot

    def chunk(ci, carry):
        sl = pl.ds(pl.multiple_of(ci * C, C), C)
        lanes = [slice(p * LANES, (p + 1) * LANES) for p in range(n_pairs)]
        ins = [tuple(ref[sl, ls] for ref in (r_ref, lw_ref, k_ref, v_ref, a_ref, b_ref)) + (s_ref[p],)
               for p, ls in enumerate(lanes)]
        outs = _round_robin([pair_chunk(*args) for args in ins])
        for p, ls in enumerate(lanes):
            o_ref[sl, ls] = outs[p][0]
            s_ref[p] = outs[p][1]
        return carry

    lax.fori_loop(0, n_chunks, chunk, 0)


def _rwkv_scan(r, lw, k, v, a, b):
    T, W = r.shape
    tb = min(SCAN_TBLOCK, T)
    n_pairs = SCAN_PAIRS if W % (SCAN_PAIRS * LANES) == 0 else 1
    wb = n_pairs * LANES
    assert T % tb == 0 and tb % SCAN_CHUNK == 0 and W % wb == 0
    spec = pl.BlockSpec((tb, wb), lambda p, t: (t, p))
    return pl.pallas_call(
        functools.partial(_rwkv_scan_kernel, n_chunks=tb // SCAN_CHUNK, n_pairs=n_pairs),
        grid=(W // wb, T // tb),
        in_specs=[spec] * 6,
        out_specs=spec,
        out_shape=jax.ShapeDtypeStruct((T, W), F32),
        scratch_shapes=[pltpu.VMEM((n_pairs, LANES, LANES), F32)],
        compiler_params=_cparams("parallel", "arbitrary"),
        name="rwkv7_scan",
    )(r, lw, k, v, a, b)


def _mem_attn_kernel(q_ref, k_ref, v_ref, g_ref, o_ref):
    scale = MEM_HEAD_DIM ** -0.5
    for h in range(MEM_HEADS):
        hs = slice(h * MEM_HEAD_DIM, (h + 1) * MEM_HEAD_DIM)
        q = q_ref[:, hs]
        qn = q * lax.rsqrt(jnp.mean(q * q, axis=-1, keepdims=True) + NORM_EPS) * g_ref[...]
        s = lax.dot_general(qn.astype(BF16), k_ref[:, hs], _NT, preferred_element_type=F32) * scale
        m = jnp.max(s, axis=-1, keepdims=True)
        p = jnp.exp(s - m)
        l = jnp.sum(p, axis=-1, keepdims=True)
        o = jnp.dot(p.astype(BF16), v_ref[:, hs], preferred_element_type=F32)
        o_ref[:, hs] = o / l


def _mem_attend(y, col_block, mk, mv, q_gain, tm=512):
    T = y.shape[0]
    M = mk.shape[0]
    tm = min(tm, T)
    return pl.pallas_call(
        _mem_attn_kernel,
        grid=(T // tm,),
        in_specs=[
            pl.BlockSpec((tm, MEM_WIDTH), lambda i: (i, col_block)),
            pl.BlockSpec((M, MEM_WIDTH), lambda i: (0, 0)),
            pl.BlockSpec((M, MEM_WIDTH), lambda i: (0, 0)),
            pl.BlockSpec((1, MEM_HEAD_DIM), lambda i: (0, 0)),
        ],
        out_specs=pl.BlockSpec((tm, MEM_WIDTH), lambda i: (i, 0)),
        out_shape=jax.ShapeDtypeStruct((T, MEM_WIDTH), F32),
        compiler_params=_cparams("parallel"),
        name="mem_attend",
    )(y, mk, mv, q_gain.reshape(1, MEM_HEAD_DIM))


def _dil_attn_kernel(sl_ref, q_ref, kp_ref, kc_ref, vp_ref, vc_ref, g_ref, o_ref, lse_ref, *, dil, heads):
    n = pl.program_id(0)
    h0 = pl.program_id(1) * heads
    B = ATT_BLOCK
    n_back = B
    scale = ATT_HEAD_DIM ** -0.5
    qi = lax.broadcasted_iota(jnp.int32, (B, 2 * B), 0)
    ki = lax.broadcasted_iota(jnp.int32, (B, 2 * B), 1)
    j = B + qi - ki
    valid = (j >= 0) & (j <= n_back) & ((ki >= B) | (n > 0))
    dist = (j * dil).astype(F32)
    for c in range(dil):
        rows = pl.ds(c, B, stride=dil) if dil > 1 else slice(None)
        for h in range(heads):
            hs = slice(h * ATT_HEAD_DIM, (h + 1) * ATT_HEAD_DIM)
            q = q_ref[rows, hs]
            qn = q * lax.rsqrt(jnp.mean(q * q, axis=-1, keepdims=True) + NORM_EPS) * g_ref[...]
            kcat = jnp.concatenate([kp_ref[rows, hs], kc_ref[rows, hs]], axis=0).astype(BF16)
            vcat = jnp.concatenate([vp_ref[rows, hs], vc_ref[rows, hs]], axis=0).astype(BF16)
            s = lax.dot_general(qn.astype(BF16), kcat, _NT, preferred_element_type=F32) * scale
            s = jnp.where(valid, s - sl_ref[h0 + h] * dist, -1e30)
            m = jnp.max(s, axis=-1, keepdims=True)
            p = jnp.exp(s - m)
            l = jnp.sum(p, axis=-1, keepdims=True)
            o = jnp.dot(p.astype(BF16), vcat, preferred_element_type=F32)
            o_ref[rows, hs] = o / l
            lse_ref[rows, hs] = jnp.broadcast_to(m + jnp.log(l), (B, ATT_HEAD_DIM))


def _dilated_group(y, gi, dil, k, v, q_gain):
    T, W = y.shape
    n_heads_total = N_DIL_GROUPS * DIL_HEADS
    slopes = jnp.asarray([2.0 ** (-8.0 * (gi * DIL_HEADS + h + 1) / n_heads_total) for h in range(DIL_HEADS)], F32)
    rows = ATT_BLOCK * dil
    cols = KV_WIDTH if dil == 1 else ATT_HEAD_DIM
    ncb = KV_WIDTH // cols
    blk = (rows, cols)
    cur = lambda n, hb: (n, hb)
    prev = lambda n, hb: (jnp.maximum(n - 1, 0), hb)
    o, lse = pl.pallas_call(
        functools.partial(_dil_attn_kernel, dil=dil, heads=cols // ATT_HEAD_DIM),
        grid=(T // rows, ncb),
        in_specs=[
            pl.BlockSpec(memory_space=pltpu.SMEM),
            pl.BlockSpec(blk, lambda n, hb: (n, gi * ncb + hb)),
            pl.BlockSpec(blk, prev), pl.BlockSpec(blk, cur),
            pl.BlockSpec(blk, prev), pl.BlockSpec(blk, cur),
            pl.BlockSpec((1, ATT_HEAD_DIM), lambda n, hb: (0, 0)),
        ],
        out_specs=[pl.BlockSpec(blk, cur), pl.BlockSpec(blk, cur)],
        out_shape=[jax.ShapeDtypeStruct((T, KV_WIDTH), F32)] * 2,
        compiler_params=_cparams("parallel", "parallel"),
        name=f"dilated_attn_d{dil}",
    )(slopes, y, k, k, v, v, q_gain.reshape(1, ATT_HEAD_DIM))
    return o, lse


def _merge_kernel(o0, l0, o1, l1, o2, l2, out_ref):
    a0, a1, a2 = l0[...], l1[...], l2[...]
    m = jnp.maximum(jnp.maximum(a0, a1), a2)
    w0, w1, w2 = jnp.exp(a0 - m), jnp.exp(a1 - m), jnp.exp(a2 - m)
    out_ref[...] = (w0 * o0[...] + w1 * o1[...] + w2 * o2[...]) / (w0 + w1 + w2)


def _merge_groups(parts, tm=512):
    T, W = parts[0][0].shape
    tm = min(tm, T)
    spec = pl.BlockSpec((tm, W), lambda i: (i, 0))
    flat = [t for pr in parts for t in pr]
    return pl.pallas_call(
        _merge_kernel,
        grid=(T // tm,),
        in_specs=[spec] * 6,
        out_specs=spec,
        out_shape=jax.ShapeDtypeStruct((T, W), F32),
        compiler_params=_cparams("parallel"),
        name="dilated_merge",
    )(*flat)


def _moe_up_kernel(be_ref, bf_ref, bv_ref, x_ref, wg_ref, wu_ref, o_ref, wg16, wu16):
    i = pl.program_id(0)

    @pl.when(bf_ref[i] > 0)
    def _():
        wg16[...] = wg_ref[...].astype(BF16)
        wu16[...] = wu_ref[...].astype(BF16)

    @pl.when(bv_ref[i] > 0)
    def _():
        x = x_ref[...].astype(BF16)
        hg = jnp.dot(x, wg16[...], preferred_element_type=F32)
        hu = jnp.dot(x, wu16[...], preferred_element_type=F32)
        o_ref[...] = (hg * jax.nn.sigmoid(hg) * hu).astype(BF16)

    @pl.when(bv_ref[i] == 0)
    def _():
        o_ref[...] = jnp.zeros_like(o_ref)


def _moe_down_kernel(be_ref, bf_ref, bv_ref, h_ref, wd_ref, g_ref, o_ref, wd16):
    i = pl.program_id(0)

    @pl.when(bf_ref[i] > 0)
    def _():
        wd16[...] = wd_ref[...].astype(BF16)

    @pl.when(bv_ref[i] > 0)
    def _():
        o_ref[...] = jnp.dot(h_ref[...], wd16[...], preferred_element_type=F32) * g_ref[...]

    @pl.when(bv_ref[i] == 0)
    def _():
        o_ref[...] = jnp.zeros_like(o_ref)


def _moe_experts(xg, slot_gate, blk_expert, blk_first, blk_valid, w_gate, w_up, w_down):
    NR, D = xg.shape
    R = MOE_ROWS
    n_blk = NR // R
    F = w_gate.shape[2]
    row = lambda i, be, bf, bv: (i, 0)
    wsel = lambda i, be, bf, bv: (be[i], 0, 0)
    hb = pl.pallas_call(
        _moe_up_kernel,
        grid_spec=pltpu.PrefetchScalarGridSpec(
            num_scalar_prefetch=3,
            grid=(n_blk,),
            in_specs=[pl.BlockSpec((R, D), row), pl.BlockSpec((None, D, F), wsel), pl.BlockSpec((None, D, F), wsel)],
            out_specs=pl.BlockSpec((R, F), row),
            scratch_shapes=[pltpu.VMEM((D, F), BF16), pltpu.VMEM((D, F), BF16)],
        ),
        out_shape=jax.ShapeDtypeStruct((NR, F), BF16),
        compiler_params=_cparams("arbitrary"),
        name="moe_up",
    )(blk_expert, blk_first, blk_valid, xg, w_gate, w_up)
    return pl.pallas_call(
        _moe_down_kernel,
        grid_spec=pltpu.PrefetchScalarGridSpec(
            num_scalar_prefetch=3,
            grid=(n_blk,),
            in_specs=[pl.BlockSpec((R, F), row), pl.BlockSpec((None, F, D), wsel), pl.BlockSpec((R, 1), row)],
            out_specs=pl.BlockSpec((R, D), row),
            scratch_shapes=[pltpu.VMEM((F, D), BF16)],
        ),
        out_shape=jax.ShapeDtypeStruct((NR, D), F32),
        compiler_params=_cparams("arbitrary"),
        name="moe_down",
    )(blk_expert, blk_first, blk_valid, hb, w_down, slot_gate.reshape(NR, 1))


def _hier_moe(h, ln_g, w_rg, w_re, w_gate, w_up, w_down):
    T, D = h.shape
    R = MOE_ROWS
    (xn32,) = _rmsnorm(h, ln_g, (F32,))
    n_rout = N_GROUPS + N_EXPERTS
    w_r = jnp.pad(jnp.concatenate([w_rg, w_re], axis=1), ((0, 0), (0, LANES - n_rout)))
    logits = _matmul(xn32, w_r, split3=True, tm=512)
    gl = logits[:, :N_GROUPS]
    g_sel = jnp.argmax(gl, axis=-1)
    p_group = jnp.take_along_axis(jax.nn.softmax(gl, axis=-1), g_sel[:, None], axis=-1)
    el = logits[:, N_GROUPS:n_rout].reshape(T, N_GROUPS, EXPERTS_PER_GROUP)
    el = jnp.take_along_axis(el, g_sel[:, None, None], axis=1)[:, 0]
    top_v, top_i = lax.top_k(el, TOP_K)
    gate = p_group * jax.nn.softmax(top_v, axis=-1)
    expert = (g_sel[:, None] * EXPERTS_PER_GROUP + top_i).reshape(-1).astype(jnp.int32)
    n_assign = T * TOP_K
    tok = jnp.arange(n_assign, dtype=jnp.int32) // TOP_K
    order = jnp.argsort(expert)
    e_sorted = expert[order]
    counts = jnp.bincount(expert, length=N_EXPERTS)
    starts = jnp.cumsum(counts) - counts
    padded = (counts + R - 1) // R * R
    p_end = jnp.cumsum(padded)
    p_start = p_end - padded
    dest = (p_start[e_sorted] + jnp.arange(n_assign, dtype=jnp.int32) - starts[e_sorted]).astype(jnp.int32)
    n_blk = -(-n_assign // R) + N_EXPERTS
    slot_tok = jnp.zeros((n_blk * R,), jnp.int32).at[dest].set(tok[order])
    slot_gate = jnp.zeros((n_blk * R,), F32).at[dest].set(gate.reshape(-1)[order])
    blk_start = jnp.arange(n_blk, dtype=jnp.int32) * R
    blk_valid = (blk_start < p_end[-1]).astype(jnp.int32)
    last_valid = jnp.maximum(p_end[-1] // R - 1, 0)
    blk_expert = jnp.minimum(jnp.searchsorted(p_end, blk_start, side='right'), N_EXPERTS - 1).astype(jnp.int32)
    blk_expert = jnp.where(blk_valid > 0, blk_expert, blk_expert[last_valid])
    blk_first = jnp.concatenate([jnp.ones((1,), jnp.int32), (blk_expert[1:] != blk_expert[:-1]).astype(jnp.int32)])
    pos = jnp.zeros((n_assign,), jnp.int32).at[order].set(dest)

    xg = jnp.take(xn32, slot_tok, axis=0)
    ys = _moe_experts(xg, slot_gate, blk_expert, blk_first, blk_valid, w_gate, w_up, w_down)
    pos2 = pos.reshape(T, TOP_K)
    return h + (jnp.take(ys, pos2[:, 0], axis=0) + jnp.take(ys, pos2[:, 1], axis=0))


def _mem_kv(mem, g, w_kv, k_gain):
    M = mem.shape[0]
    (mn,) = _rmsnorm(mem, g, (BF16,))
    kv = _matmul(mn, w_kv.astype(BF16))
    k = kv[:, :MEM_WIDTH].reshape(M, MEM_HEADS, MEM_HEAD_DIM)
    k = k * lax.rsqrt(jnp.mean(k * k, axis=-1, keepdims=True) + NORM_EPS) * k_gain
    return k.reshape(M, MEM_WIDTH).astype(BF16), kv[:, MEM_WIDTH:].astype(BF16)


def _pad_cols(w, n):
    return jnp.pad(w, ((0, 0), (0, n - w.shape[1])))


def _a_layer_weights(w_in, mu, w_vdown, mu_vres):
    D = w_in.shape[0]
    c = np.cumsum([RWKV_WIDTH, RWKV_WIDTH, RWKV_WIDTH, LORA_DECAY, LORA_AAA, LORA_GATE, MEM_WIDTH])
    rkv, xw, xa, xg, qm = (slice(0, c[2]), slice(c[2], c[3]), slice(c[3], c[4]), slice(c[4], c[5]), slice(c[5], c[6]))
    if w_vdown is None:
        w_vdown = jnp.zeros((D, LORA_MV), w_in.dtype)
        mu_vres = jnp.zeros((LORA_MV,), mu.dtype)
    tail = A_IN_PAD - A_OFF_XV
    w = jnp.concatenate([w_in[:, rkv], w_in[:, qm], w_in[:, xw], w_in[:, xa],
                         _pad_cols(w_in[:, xg], A_XG_PAD), _pad_cols(w_vdown, tail)], axis=1)
    m = jnp.concatenate([mu[rkv], mu[qm], mu[xw], mu[xa],
                         jnp.pad(mu[xg], (0, A_XG_PAD - LORA_GATE)), jnp.pad(mu_vres, (0, tail - LORA_MV))])
    return w.astype(BF16), m


def _rwkv_layer(h, mk, mv, mq_gain, ln1, w_in, mu, w_vdown, mu_vres, v_first, w0, w_up, a0, a_up, g_up,
                v0, v_up, k_k, k_a, r_k, lnx_g, lnx_b, w_out):
    T, D = h.shape
    H, N = RWKV_WIDTH // RWKV_HEAD, RWKV_HEAD
    (xn,) = _rmsnorm(h, ln1, (BF16,))
    w_pad, mu_pad = _a_layer_weights(w_in, mu, w_vdown, mu_vres)
    y = _matmul(xn, w_pad)
    y_prev = jnp.pad(y, ((1, 0), (0, 0)))[:-1]
    y = y + mu_pad * (y_prev - y)

    r = y[:, A_OFF_R:A_OFF_R + RWKV_WIDTH]
    k = y[:, A_OFF_K:A_OFF_K + RWKV_WIDTH]
    v = y[:, A_OFF_V:A_OFF_V + RWKV_WIDTH]
    xw = jnp.tanh(y[:, A_OFF_XW:A_OFF_XW + LORA_DECAY]).astype(BF16)
    xa = y[:, A_OFF_XA:A_OFF_XA + LORA_AAA].astype(BF16)
    xg = jax.nn.sigmoid(y[:, A_OFF_XG:A_OFF_XG + A_XG_PAD]).astype(BF16)
    pad_rows = lambda w, n: jnp.pad(w, ((0, n - w.shape[0]), (0, 0))).astype(BF16)

    w_log = -jax.nn.softplus(-(w0 + _matmul(xw, w_up.astype(BF16)))) - 0.5
    lw = -jnp.exp(w_log)
    a = jax.nn.sigmoid(a0 + _matmul(xa, a_up.astype(BF16)))
    g = _matmul(xg, pad_rows(g_up, A_XG_PAD))
    if v_first is not None:
        xv = y[:, A_OFF_XV:A_OFF_XV + A_XV_PAD].astype(BF16)
        mix = jax.nn.sigmoid(v0 + _matmul(xv, pad_rows(v_up, A_XV_PAD)))
        v = v + (v_first - v) * mix
    heads = lambda t: t.reshape(T, H, N)
    kk = heads(k * k_k)
    kk = kk / jnp.maximum(jnp.sqrt(jnp.sum(kk * kk, axis=-1, keepdims=True)), 1e-12)
    kk = kk.reshape(T, RWKV_WIDTH)
    k = k * (1.0 + (a - 1.0) * k_a)

    ys = _rwkv_scan(r, lw, k, v, -kk, kk * a)

    yh = heads(ys)
    mean = jnp.mean(yh, axis=-1, keepdims=True)
    var = jnp.mean(jnp.square(yh - mean), axis=-1, keepdims=True)
    yn = ((yh - mean) * lax.rsqrt(var + LNX_EPS)).reshape(T, RWKV_WIDTH) * lnx_g + lnx_b
    bonus = jnp.sum(heads(r) * heads(k) * r_k, axis=-1, keepdims=True) * heads(v)
    mix_out = (yn + bonus.reshape(T, RWKV_WIDTH)) * g

    mem_out = _mem_attend(y, A_OFF_QM // MEM_WIDTH, mk, mv, mq_gain)
    heads_out = jnp.concatenate([mix_out, mem_out], axis=-1).astype(BF16)
    return _matmul(heads_out, w_out.astype(BF16), res=h), v


def _dilated_layer(h, mk, mv, mq_gain, ln1, w_in, q_gain, ks, vs, w_out):
    (xn,) = _rmsnorm(h, ln1, (BF16,))
    y = _matmul(xn, w_in.astype(BF16))
    parts = [_dilated_group(y, gi, dil, ks, vs, q_gain) for gi, (_, dil) in enumerate(DIL_PATTERNS)]
    att = _merge_groups(parts)
    mem_out = _mem_attend(y, DIL_Q_WIDTH // MEM_WIDTH, mk, mv, mq_gain)
    heads_out = jnp.concatenate([att, mem_out], axis=-1).astype(BF16)
    return _matmul(heads_out, w_out.astype(BF16), res=h)


def _shared_kv(h, g, w_kv, k_gain):
    T = h.shape[0]
    (xn,) = _rmsnorm(h, g, (BF16,))
    kv = _matmul(xn, w_kv.astype(BF16))
    k = kv[:, :KV_WIDTH].reshape(T, DIL_HEADS, ATT_HEAD_DIM)
    k = k * lax.rsqrt(jnp.mean(k * k, axis=-1, keepdims=True) + NORM_EPS) * k_gain
    return k.reshape(T, KV_WIDTH), kv[:, KV_WIDTH:]


def kernel(x, mem, a_ln1, a_w_in, a_w_vdown, a_mu, a_mu_vres, a_w0, a_w_up, a_a0, a_a_up, a_g_up, a_v0, a_v_up, a_k_k, a_k_a, a_r_k, a_lnx_g, a_lnx_b, a_w_out, b_ln1, b_w_in, b_q_norm, b_w_out, s_kv_norm, s_w_kv, s_k_norm, m_norm, m_w_kv, m_q_norm, m_k_norm, moe_ln, moe_router_group, moe_router_expert, moe_w_gate, moe_w_up, moe_w_down):
    Bsz, S, D = x.shape
    assert Bsz == 1 and S % (max(d for _, d in DIL_PATTERNS) * ATT_BLOCK) == 0
    depth = moe_ln.shape[0]
    h = x.reshape(S, D)
    mem2 = mem.reshape(mem.shape[1], D)
    v_first = None
    shared = None
    for l in range(depth):
        mk, mv = _mem_kv(mem2, m_norm[l], m_w_kv[l], m_k_norm[l])
        if l < N_A:
            i = l
            has_vres = i > 0
            h, v_i = _rwkv_layer(
                h, mk, mv, m_q_norm[l], a_ln1[i], a_w_in[i], a_mu[i],
                a_w_vdown[i - 1] if has_vres else None, a_mu_vres[i - 1] if has_vres else None,
                v_first, a_w0[i], a_w_up[i], a_a0[i], a_a_up[i], a_g_up[i],
                a_v0[i - 1] if has_vres else None, a_v_up[i - 1] if has_vres else None,
                a_k_k[i], a_k_a[i], a_r_k[i].reshape(1, RWKV_WIDTH // RWKV_HEAD, RWKV_HEAD),
                a_lnx_g[i], a_lnx_b[i], a_w_out[i])
            if i == 0:
                v_first = v_i
        else:
            j = l - N_A
            if shared is None:
                shared = _shared_kv(h, s_kv_norm, s_w_kv, s_k_norm)
            h = _dilated_layer(h, mk, mv, m_q_norm[l], b_ln1[j], b_w_in[j], b_q_norm[j], shared[0], shared[1], b_w_out[j])
        h = _hier_moe(h, moe_ln[l], moe_router_group[l], moe_router_expert[l], moe_w_gate[l], moe_w_up[l], moe_w_down[l])
    return h.reshape(Bsz, S, D)
```

```python
import functools

import jax
import jax.numpy as jnp
import numpy as np
from jax import lax
from jax.experimental import pallas as pl
from jax.experimental.pallas import tpu as pltpu

F32 = jnp.float32
BF16 = jnp.bfloat16

D_MODEL = 4096
RWKV_HEAD = 64
RWKV_WIDTH = 3072
LORA_DECAY = 128
LORA_AAA = 128
LORA_MV = 96
LORA_GATE = 480
LNX_EPS = 64e-5
MEM_HEADS = 4
MEM_WIDTH = 1024
MEM_HEAD_DIM = 256
ATT_HEAD_DIM = 128
DIL_PATTERNS = ((128, 1), (512, 4), (2048, 16))
N_DIL_GROUPS = 3
DIL_HEADS = 8
DIL_Q_WIDTH = 3072
KV_WIDTH = 1024
ATT_BLOCK = 128
N_GROUPS = 4
EXPERTS_PER_GROUP = 8
N_EXPERTS = 32
TOP_K = 2
D_FF = 384
NORM_EPS = 1e-6
N_A = 2

LANES = 128
SUBLANES = 8
VMEM_LIMIT_BYTES = 56 * 1024 * 1024

A_OFF_R = 0
A_OFF_K = RWKV_WIDTH
A_OFF_V = 2 * RWKV_WIDTH
A_OFF_QM = 3 * RWKV_WIDTH
A_OFF_XW = A_OFF_QM + MEM_WIDTH
A_OFF_XA = A_OFF_XW + LORA_DECAY
A_OFF_XG = A_OFF_XA + LORA_AAA
A_XG_PAD = 512
A_OFF_XV = A_OFF_XG + A_XG_PAD
A_XV_PAD = 128
A_IN_PAD = 11264

SCAN_CHUNK = 64
SCAN_TBLOCK = 512
SCAN_PAIRS = 4
MOE_ROWS = 256

_NN = (((1,), (0,)), ((), ()))
_NT = (((1,), (1,)), ((), ()))


def _cparams(*sem):
    return pltpu.CompilerParams(dimension_semantics=sem, vmem_limit_bytes=VMEM_LIMIT_BYTES)


def _rms_kernel(x_ref, g_ref, *o_refs):
    x = x_ref[...]
    y = x * lax.rsqrt(jnp.mean(x * x, axis=-1, keepdims=True) + NORM_EPS) * g_ref[...]
    for o_ref in o_refs:
        o_ref[...] = y.astype(o_ref.dtype)


def _rmsnorm(x, g, dtypes, tm=256):
    T, D = x.shape
    tm = min(tm, T)
    outs = pl.pallas_call(
        _rms_kernel,
        grid=(T // tm,),
        in_specs=[pl.BlockSpec((tm, D), lambda i: (i, 0)), pl.BlockSpec((1, D), lambda i: (0, 0))],
        out_specs=[pl.BlockSpec((tm, D), lambda i: (i, 0)) for _ in dtypes],
        out_shape=[jax.ShapeDtypeStruct((T, D), dt) for dt in dtypes],
        compiler_params=_cparams("parallel"),
        name="rmsnorm",
    )(x, g.reshape(1, D))
    return outs


def _mm_kernel(x_ref, w_ref, o_ref):
    o_ref[...] = jnp.dot(x_ref[...], w_ref[...], preferred_element_type=F32)


def _mm_res_kernel(x_ref, w_ref, r_ref, o_ref):
    o_ref[...] = r_ref[...] + jnp.dot(x_ref[...], w_ref[...], preferred_element_type=F32)


def _mm3_kernel(x_ref, w_ref, o_ref):
    x = x_ref[...]
    w = w_ref[...]
    xh = x.astype(BF16)
    xl = (x - xh.astype(F32)).astype(BF16)
    wh = w.astype(BF16)
    wl = (w - wh.astype(F32)).astype(BF16)
    d = lambda a, b: jnp.dot(a, b, preferred_element_type=F32)
    o_ref[...] = d(xh, wh) + (d(xh, wl) + d(xl, wh))


def _matmul(x, w, res=None, tm=1024, tn=512, split3=False):
    M, K = x.shape
    N = w.shape[1]
    tm = min(tm, M)
    tn = min(tn, N)
    assert M % tm == 0 and N % tn == 0
    in_specs = [pl.BlockSpec((tm, K), lambda i, j: (i, 0)), pl.BlockSpec((K, tn), lambda i, j: (0, j))]
    args = [x, w]
    if split3:
        body = _mm3_kernel
    elif res is None:
        body = _mm_kernel
    else:
        body = _mm_res_kernel
        in_specs.append(pl.BlockSpec((tm, tn), lambda i, j: (i, j)))
        args.append(res)
    return pl.pallas_call(
        body,
        grid=(M // tm, N // tn),
        in_specs=in_specs,
        out_specs=pl.BlockSpec((tm, tn), lambda i, j: (i, j)),
        out_shape=jax.ShapeDtypeStruct((M, N), F32),
        compiler_params=_cparams("parallel", "parallel"),
        name="matmul",
    )(*args)


def _split(x):
    hi = x.astype(BF16)
    lo = (x - hi.astype(F32)).astype(BF16)
    return hi, lo


def _dot3(a, b, dims=_NN):
    ah, al = a
    bh, bl = b
    d = lambda x, y: lax.dot_general(x, y, dims, preferred_element_type=F32)
    return d(ah, bh) + (d(ah, bl) + d(al, bh))


def _round_robin(gens):
    outs = [None] * len(gens)
    active = list(range(len(gens)))
    while active:
        for i in list(active):
            try:
                next(gens[i])
            except StopIteration as stop:
                outs[i] = stop.value
                active.remove(i)
    return outs


def _rwkv_scan_kernel(r_ref, lw_ref, k_ref, v_ref, a_ref, b_ref, o_ref, s_ref, *, n_chunks, n_pairs):
    C = SCAN_CHUNK
    H2 = 2 * C

    @pl.when(pl.program_id(1) == 0)
    def _():
        s_ref[...] = jnp.zeros_like(s_ref)

    lane = lax.broadcasted_iota(jnp.int32, (C, LANES), 1)
    head0 = lane < RWKV_HEAD
    trow = lax.broadcasted_iota(jnp.int32, (C, LANES), 0)
    tcol = lane & (C - 1)
    lane2 = lax.broadcasted_iota(jnp.int32, (H2, LANES), 1)
    row2 = lax.broadcasted_iota(jnp.int32, (H2, LANES), 0)
    blockdiag = (row2 < C) == (lane2 < RWKV_HEAD)
    tri_r = lax.broadcasted_iota(jnp.int32, (C, C), 0)
    tri_c = lax.broadcasted_iota(jnp.int32, (C, C), 1)
    ltri = jnp.where(tri_r >= tri_c, 1.0, 0.0).astype(BF16)
    arow = lax.broadcasted_iota(jnp.int32, (H2, 2 * H2), 0)
    acol = lax.broadcasted_iota(jnp.int32, (H2, 2 * H2), 1)
    a_t = arow & (C - 1)
    a_j = acol & (C - 1)
    amask = a_t + jnp.where(arow < C, 0, 1) > a_j
    n_levels = C.bit_length() - 1
    eye2 = jnp.where(trow == tcol, 1.0, 0.0)

    def level_mask(lvl):
        tb_, jb_ = trow >> lvl, tcol >> lvl
        return (tb_ - jb_) * (tb_ & 1) == 1

    def stack2(x):
        z = jnp.zeros_like(x)
        return jnp.concatenate([jnp.where(head0, x, z), jnp.where(head0, z, x)], axis=0)

    def pair_chunk(r, lw, k, v, a, b, s):
        l1 = lw.astype(BF16)
        rem = lw - l1.astype(F32)
        l2 = rem.astype(BF16)
        l3 = (rem - l2.astype(F32)).astype(BF16)
        dd = lambda x, y: jnp.dot(x, y, preferred_element_type=F32)
        cum = dd(ltri, l1) + (dd(ltri, l2) + dd(ltri, l3))
        yield
        p_inc = jnp.exp(cum)
        p_exc = jnp.exp(cum - lw)
        p_inv = jnp.exp(-cum)
        at = a * p_exc
        rt = r * p_inc
        bt = b * p_inv
        kt = k * p_inv
        p_tot = p_inc[C - 1:C, :]

        x_ar = _split(jnp.concatenate([at, rt], axis=0))
        w_bk = _split(jnp.concatenate([stack2(bt), stack2(kt)], axis=0))
        amat = jnp.where(amask, _dot3(x_ar, w_bk, _NT), 0.0)
        xs = _dot3(x_ar, _split(s), _NT)
        yield

        n_ab = amat[:C, :H2]
        a_ak = amat[:C, H2:]
        a_r = amat[C:, :]
        v2 = stack2(v)
        rhs = xs[:C] + _dot3(_split(a_ak), _split(v2))
        yield

        tinv = eye2 + jnp.where(level_mask(0), n_ab, 0.0)
        for lvl in range(1, n_levels):
            t_split = _split(tinv)
            t_bd = _split(stack2(tinv))
            nt = _dot3(_split(jnp.where(level_mask(lvl), n_ab, 0.0)), t_bd)
            yield
            tinv = tinv + _dot3(t_split, _split(stack2(nt)))
            yield
        u = _dot3(_split(tinv), _split(stack2(rhs)))
        yield

        uv = jnp.concatenate([stack2(u), v2], axis=0)
        y = xs[C:] + _dot3(_split(a_r), _split(uv))
        yield

        uv_t = jnp.concatenate([u, v], axis=0).T
        bk = jnp.concatenate([bt, kt], axis=0)
        ds = _dot3(_split(uv_t), _split(bk))
        return y, (s + jnp.where(blockdiag, ds, 0.0)) * p_tot

    def chunk(ci, carry):
        sl = pl.ds(pl.multiple_of(ci * C, C), C)
        lanes = [slice(p * LANES, (p + 1) * LANES) for p in range(n_pairs)]
        ins = [tuple(ref[sl, ls] for ref in (r_ref, lw_ref, k_ref, v_ref, a_ref, b_ref)) + (s_ref[p],)
               for p, ls in enumerate(lanes)]
        outs = _round_robin([pair_chunk(*args) for args in ins])
        for p, ls in enumerate(lanes):
            o_ref[sl, ls] = outs[p][0]
            s_ref[p] = outs[p][1]
        return carry

    lax.fori_loop(0, n_chunks, chunk, 0)


def _rwkv_scan(r, lw, k, v, a, b):
    T, W = r.shape
    tb = min(SCAN_TBLOCK, T)
    n_pairs = SCAN_PAIRS if W % (SCAN_PAIRS * LANES) == 0 else 1
    wb = n_pairs * LANES
    assert T % tb == 0 and tb % SCAN_CHUNK == 0 and W % wb == 0
    spec = pl.BlockSpec((tb, wb), lambda p, t: (t, p))
    return pl.pallas_call(
        functools.partial(_rwkv_scan_kernel, n_chunks=tb // SCAN_CHUNK, n_pairs=n_pairs),
        grid=(W // wb, T // tb),
        in_specs=[spec] * 6,
        out_specs=spec,
        out_shape=jax.ShapeDtypeStruct((T, W), F32),
        scratch_shapes=[pltpu.VMEM((n_pairs, LANES, LANES), F32)],
        compiler_params=_cparams("parallel", "arbitrary"),
        name="rwkv7_scan",
    )(r, lw, k, v, a, b)


def _mem_attn_kernel(q_ref, k_ref, v_ref, g_ref, o_ref):
    scale = MEM_HEAD_DIM ** -0.5
    for h in range(MEM_HEADS):
        hs = slice(h * MEM_HEAD_DIM, (h + 1) * MEM_HEAD_DIM)
        q = q_ref[:, hs]
        qn = q * lax.rsqrt(jnp.mean(q * q, axis=-1, keepdims=True) + NORM_EPS) * g_ref[...]
        s = lax.dot_general(qn.astype(BF16), k_ref[:, hs], _NT, preferred_element_type=F32) * scale
        m = jnp.max(s, axis=-1, keepdims=True)
        p = jnp.exp(s - m)
        l = jnp.sum(p, axis=-1, keepdims=True)
        o = jnp.dot(p.astype(BF16), v_ref[:, hs], preferred_element_type=F32)
        o_ref[:, hs] = o / l


def _mem_attend(y, col_block, mk, mv, q_gain, tm=512):
    T = y.shape[0]
    M = mk.shape[0]
    tm = min(tm, T)
    return pl.pallas_call(
        _mem_attn_kernel,
        grid=(T // tm,),
        in_specs=[
            pl.BlockSpec((tm, MEM_WIDTH), lambda i: (i, col_block)),
            pl.BlockSpec((M, MEM_WIDTH), lambda i: (0, 0)),
            pl.BlockSpec((M, MEM_WIDTH), lambda i: (0, 0)),
            pl.BlockSpec((1, MEM_HEAD_DIM), lambda i: (0, 0)),
        ],
        out_specs=pl.BlockSpec((tm, MEM_WIDTH), lambda i: (i, 0)),
        out_shape=jax.ShapeDtypeStruct((T, MEM_WIDTH), F32),
        compiler_params=_cparams("parallel"),
        name="mem_attend",
    )(y, mk, mv, q_gain.reshape(1, MEM_HEAD_DIM))


def _dil_attn_kernel(sl_ref, q_ref, kp_ref, kc_ref, vp_ref, vc_ref, g_ref, o_ref, lse_ref, *, dil, heads):
    n = pl.program_id(0)
    h0 = pl.program_id(1) * heads
    B = ATT_BLOCK
    n_back = B
    scale = ATT_HEAD_DIM ** -0.5
    qi = lax.broadcasted_iota(jnp.int32, (B, 2 * B), 0)
    ki = lax.broadcasted_iota(jnp.int32, (B, 2 * B), 1)
    j = B + qi - ki
    valid = (j >= 0) & (j <= n_back) & ((ki >= B) | (n > 0))
    dist = (j * dil).astype(F32)
    for c in range(dil):
        rows = pl.ds(c, B, stride=dil) if dil > 1 else slice(None)
        for h in range(heads):
            hs = slice(h * ATT_HEAD_DIM, (h + 1) * ATT_HEAD_DIM)
            q = q_ref[rows, hs]
            qn = q * lax.rsqrt(jnp.mean(q * q, axis=-1, keepdims=True) + NORM_EPS) * g_ref[...]
            kcat = jnp.concatenate([kp_ref[rows, hs], kc_ref[rows, hs]], axis=0).astype(BF16)
            vcat = jnp.concatenate([vp_ref[rows, hs], vc_ref[rows, hs]], axis=0).astype(BF16)
            s = lax.dot_general(qn.astype(BF16), kcat, _NT, preferred_element_type=F32) * scale
            s = jnp.where(valid, s - sl_ref[h0 + h] * dist, -1e30)
            m = jnp.max(s, axis=-1, keepdims=True)
            p = jnp.exp(s - m)
            l = jnp.sum(p, axis=-1, keepdims=True)
            o = jnp.dot(p.astype(BF16), vcat, preferred_element_type=F32)
            o_ref[rows, hs] = o / l
            lse_ref[rows, hs] = jnp.broadcast_to(m + jnp.log(l), (B, ATT_HEAD_DIM))


def _dilated_group(y, gi, dil, k, v, q_gain):
    T, W = y.shape
    n_heads_total = N_DIL_GROUPS * DIL_HEADS
    slopes = jnp.asarray([2.0 ** (-8.0 * (gi * DIL_HEADS + h + 1) / n_heads_total) for h in range(DIL_HEADS)], F32)
    rows = ATT_BLOCK * dil
    cols = KV_WIDTH if dil == 1 else ATT_HEAD_DIM
    ncb = KV_WIDTH // cols
    blk = (rows, cols)
    cur = lambda n, hb: (n, hb)
    prev = lambda n, hb: (jnp.maximum(n - 1, 0), hb)
    o, lse = pl.pallas_call(
        functools.partial(_dil_attn_kernel, dil=dil, heads=cols // ATT_HEAD_DIM),
        grid=(T // rows, ncb),
        in_specs=[
            pl.BlockSpec(memory_space=pltpu.SMEM),
            pl.BlockSpec(blk, lambda n, hb: (n, gi * ncb + hb)),
            pl.BlockSpec(blk, prev), pl.BlockSpec(blk, cur),
            pl.BlockSpec(blk, prev), pl.BlockSpec(blk, cur),
            pl.BlockSpec((1, ATT_HEAD_DIM), lambda n, hb: (0, 0)),
        ],
        out_specs=[pl.BlockSpec(blk, cur), pl.BlockSpec(blk, cur)],
        out_shape=[jax.ShapeDtypeStruct((T, KV_WIDTH), F32)] * 2,
        compiler_params=_cparams("parallel", "parallel"),
        name=f"dilated_attn_d{dil}",
    )(slopes, y, k, k, v, v, q_gain.reshape(1, ATT_HEAD_DIM))
    return o, lse


def _merge_kernel(o0, l0, o1, l1, o2, l2, out_ref):
    a0, a1, a2 = l0[...], l1[...], l2[...]
    m = jnp.maximum(jnp.maximum(a0, a1), a2)
    w0, w1, w2 = jnp.exp(a0 - m), jnp.exp(a1 - m), jnp.exp(a2 - m)
    out_ref[...] = (w0 * o0[...] + w1 * o1[...] + w2 * o2[...]) / (w0 + w1 + w2)


def _merge_groups(parts, tm=512):
    T, W = parts[0][0].shape
    tm = min(tm, T)
    spec = pl.BlockSpec((tm, W), lambda i: (i, 0))
    flat = [t for pr in parts for t in pr]
    return pl.pallas_call(
        _merge_kernel,
        grid=(T // tm,),
        in_specs=[spec] * 6,
        out_specs=spec,
        out_shape=jax.ShapeDtypeStruct((T, W), F32),
        compiler_params=_cparams("parallel"),
        name="dilated_merge",
    )(*flat)


def _moe_up_kernel(tok_ref, be_ref, bf_ref, bv_ref, x_hbm, wg_ref, wu_ref, o_ref, xbuf, sem, wg16, wu16):
    i = pl.program_id(0)
    n_blk = pl.num_programs(0)
    R = MOE_ROWS

    def start_gather(blk, slot):
        def body(r, carry):
            tok = tok_ref[blk * R + r]
            pltpu.make_async_copy(x_hbm.at[pl.ds(tok, 1)], xbuf.at[slot, pl.ds(r, 1)], sem.at[slot]).start()
            return carry
        lax.fori_loop(0, R, body, 0, unroll=8)

    @pl.when(i == 0)
    def _():
        start_gather(0, 0)

    nxt = jnp.minimum(i + 1, n_blk - 1)

    @pl.when((i + 1 < n_blk) & (bv_ref[nxt] > 0))
    def _():
        start_gather(i + 1, (i + 1) & 1)

    @pl.when(bf_ref[i] > 0)
    def _():
        wg16[...] = wg_ref[...].astype(BF16)
        wu16[...] = wu_ref[...].astype(BF16)

    @pl.when(bv_ref[i] > 0)
    def _():
        slot = i & 1
        pltpu.make_async_copy(x_hbm.at[pl.ds(0, R)], xbuf.at[slot], sem.at[slot]).wait()
        x = xbuf[slot].astype(BF16)
        hg = jnp.dot(x, wg16[...], preferred_element_type=F32)
        hu = jnp.dot(x, wu16[...], preferred_element_type=F32)
        o_ref[...] = (hg * jax.nn.sigmoid(hg) * hu).astype(BF16)

    @pl.when(bv_ref[i] == 0)
    def _():
        o_ref[...] = jnp.zeros_like(o_ref)


def _moe_down_kernel(be_ref, bf_ref, bv_ref, h_ref, wd_ref, g_ref, o_ref, wd16):
    i = pl.program_id(0)

    @pl.when(bf_ref[i] > 0)
    def _():
        wd16[...] = wd_ref[...].astype(BF16)

    @pl.when(bv_ref[i] > 0)
    def _():
        o_ref[...] = jnp.dot(h_ref[...], wd16[...], preferred_element_type=F32) * g_ref[...]

    @pl.when(bv_ref[i] == 0)
    def _():
        o_ref[...] = jnp.zeros_like(o_ref)


def _moe_experts(xn, slot_tok, slot_gate, blk_expert, blk_first, blk_valid, w_gate, w_up, w_down, layer):
    T, D = xn.shape
    NR = slot_tok.shape[0]
    R = MOE_ROWS
    n_blk = NR // R
    F = w_gate.shape[3]
    hb = pl.pallas_call(
        _moe_up_kernel,
        grid_spec=pltpu.PrefetchScalarGridSpec(
            num_scalar_prefetch=4,
            grid=(n_blk,),
            in_specs=[pl.BlockSpec(memory_space=pl.ANY),
                      pl.BlockSpec((None, None, D, F), lambda i, tok, be, bf, bv: (layer, be[i], 0, 0)),
                      pl.BlockSpec((None, None, D, F), lambda i, tok, be, bf, bv: (layer, be[i], 0, 0))],
            out_specs=pl.BlockSpec((R, F), lambda i, tok, be, bf, bv: (i, 0)),
            scratch_shapes=[pltpu.VMEM((2, R, D), F32), pltpu.SemaphoreType.DMA((2,)),
                            pltpu.VMEM((D, F), BF16), pltpu.VMEM((D, F), BF16)],
        ),
        out_shape=jax.ShapeDtypeStruct((NR, F), BF16),
        compiler_params=_cparams("arbitrary"),
        name="moe_up",
    )(slot_tok, blk_expert, blk_first, blk_valid, xn, w_gate, w_up)
    row = lambda i, be, bf, bv: (i, 0)
    wsel = lambda i, be, bf, bv: (layer, be[i], 0, 0)
    return pl.pallas_call(
        _moe_down_kernel,
        grid_spec=pltpu.PrefetchScalarGridSpec(
            num_scalar_prefetch=3,
            grid=(n_blk,),
            in_specs=[pl.BlockSpec((R, F), row), pl.BlockSpec((None, None, F, D), wsel), pl.BlockSpec((R, 1), row)],
            out_specs=pl.BlockSpec((R, D), row),
            scratch_shapes=[pltpu.VMEM((F, D), BF16)],
        ),
        out_shape=jax.ShapeDtypeStruct((NR, D), F32),
        compiler_params=_cparams("arbitrary"),
        name="moe_down",
    )(blk_expert, blk_first, blk_valid, hb, w_down, slot_gate.reshape(NR, 1))


def _hier_moe(h, ln_g, w_rg, w_re, w_gate, w_up, w_down, layer):
    T, D = h.shape
    R = MOE_ROWS
    (xn32,) = _rmsnorm(h, ln_g, (F32,))
    n_rout = N_GROUPS + N_EXPERTS
    w_r = jnp.pad(jnp.concatenate([w_rg, w_re], axis=1), ((0, 0), (0, LANES - n_rout)))
    logits = _matmul(xn32, w_r, split3=True, tm=512)
    gl = logits[:, :N_GROUPS]
    g_sel = jnp.argmax(gl, axis=-1)
    p_group = jnp.take_along_axis(jax.nn.softmax(gl, axis=-1), g_sel[:, None], axis=-1)
    el = logits[:, N_GROUPS:n_rout].reshape(T, N_GROUPS, EXPERTS_PER_GROUP)
    el = jnp.take_along_axis(el, g_sel[:, None, None], axis=1)[:, 0]
    top_v, top_i = lax.top_k(el, TOP_K)
    gate = p_group * jax.nn.softmax(top_v, axis=-1)
    expert = (g_sel[:, None] * EXPERTS_PER_GROUP + top_i).reshape(-1).astype(jnp.int32)
    n_assign = T * TOP_K
    tok = jnp.arange(n_assign, dtype=jnp.int32) // TOP_K
    order = jnp.argsort(expert)
    e_sorted = expert[order]
    counts = jnp.bincount(expert, length=N_EXPERTS)
    starts = jnp.cumsum(counts) - counts
    padded = (counts + R - 1) // R * R
    p_end = jnp.cumsum(padded)
    p_start = p_end - padded
    dest = (p_start[e_sorted] + jnp.arange(n_assign, dtype=jnp.int32) - starts[e_sorted]).astype(jnp.int32)
    n_blk = -(-n_assign // R) + N_EXPERTS
    slot_tok = jnp.zeros((n_blk * R,), jnp.int32).at[dest].set(tok[order])
    slot_gate = jnp.zeros((n_blk * R,), F32).at[dest].set(gate.reshape(-1)[order])
    blk_start = jnp.arange(n_blk, dtype=jnp.int32) * R
    blk_valid = (blk_start < p_end[-1]).astype(jnp.int32)
    last_valid = jnp.maximum(p_end[-1] // R - 1, 0)
    blk_expert = jnp.minimum(jnp.searchsorted(p_end, blk_start, side='right'), N_EXPERTS - 1).astype(jnp.int32)
    blk_expert = jnp.where(blk_valid > 0, blk_expert, blk_expert[last_valid])
    blk_first = jnp.concatenate([jnp.ones((1,), jnp.int32), (blk_expert[1:] != blk_expert[:-1]).astype(jnp.int32)])
    pos = jnp.zeros((n_assign,), jnp.int32).at[order].set(dest)

    ys = _moe_experts(xn32, slot_tok, slot_gate, blk_expert, blk_first, blk_valid, w_gate, w_up, w_down, layer)
    pos2 = pos.reshape(T, TOP_K)
    rows_of = lambda idx: ys.at[idx].get(mode="promise_in_bounds")
    return h + (rows_of(pos2[:, 0]) + rows_of(pos2[:, 1]))


def _mem_kv(mem, g, w_kv, k_gain):
    M = mem.shape[0]
    (mn,) = _rmsnorm(mem, g, (BF16,))
    kv = _matmul(mn, w_kv.astype(BF16))
    k = kv[:, :MEM_WIDTH].reshape(M, MEM_HEADS, MEM_HEAD_DIM)
    k = k * lax.rsqrt(jnp.mean(k * k, axis=-1, keepdims=True) + NORM_EPS) * k_gain
    return k.reshape(M, MEM_WIDTH).astype(BF16), kv[:, MEM_WIDTH:].astype(BF16)


def _pad_cols(w, n):
    return jnp.pad(w, ((0, 0), (0, n - w.shape[1])))


def _a_layer_weights(w_in, mu, w_vdown, mu_vres):
    D = w_in.shape[0]
    c = np.cumsum([RWKV_WIDTH, RWKV_WIDTH, RWKV_WIDTH, LORA_DECAY, LORA_AAA, LORA_GATE, MEM_WIDTH])
    rkv, xw, xa, xg, qm = (slice(0, c[2]), slice(c[2], c[3]), slice(c[3], c[4]), slice(c[4], c[5]), slice(c[5], c[6]))
    if w_vdown is None:
        w_vdown = jnp.zeros((D, LORA_MV), w_in.dtype)
        mu_vres = jnp.zeros((LORA_MV,), mu.dtype)
    tail = A_IN_PAD - A_OFF_XV
    w = jnp.concatenate([w_in[:, rkv], w_in[:, qm], w_in[:, xw], w_in[:, xa],
                         _pad_cols(w_in[:, xg], A_XG_PAD), _pad_cols(w_vdown, tail)], axis=1)
    m = jnp.concatenate([mu[rkv], mu[qm], mu[xw], mu[xa],
                         jnp.pad(mu[xg], (0, A_XG_PAD - LORA_GATE)), jnp.pad(mu_vres, (0, tail - LORA_MV))])
    return w.astype(BF16), m


def _rwkv_layer(h, mk, mv, mq_gain, ln1, w_in, mu, w_vdown, mu_vres, v_first, w0, w_up, a0, a_up, g_up,
                v0, v_up, k_k, k_a, r_k, lnx_g, lnx_b, w_out):
    T, D = h.shape
    H, N = RWKV_WIDTH // RWKV_HEAD, RWKV_HEAD
    (xn,) = _rmsnorm(h, ln1, (BF16,))
    w_pad, mu_pad = _a_layer_weights(w_in, mu, w_vdown, mu_vres)
    y = _matmul(xn, w_pad)
    y_prev = jnp.pad(y, ((1, 0), (0, 0)))[:-1]
    y = y + mu_pad * (y_prev - y)

    r = y[:, A_OFF_R:A_OFF_R + RWKV_WIDTH]
    k = y[:, A_OFF_K:A_OFF_K + RWKV_WIDTH]
    v = y[:, A_OFF_V:A_OFF_V + RWKV_WIDTH]
    xw = jnp.tanh(y[:, A_OFF_XW:A_OFF_XW + LORA_DECAY]).astype(BF16)
    xa = y[:, A_OFF_XA:A_OFF_XA + LORA_AAA].astype(BF16)
    xg = jax.nn.sigmoid(y[:, A_OFF_XG:A_OFF_XG + A_XG_PAD]).astype(BF16)
    pad_rows = lambda w, n: jnp.pad(w, ((0, n - w.shape[0]), (0, 0))).astype(BF16)

    w_log = -jax.nn.softplus(-(w0 + _matmul(xw, w_up.astype(BF16)))) - 0.5
    lw = -jnp.exp(w_log)
    a = jax.nn.sigmoid(a0 + _matmul(xa, a_up.astype(BF16)))
    g = _matmul(xg, pad_rows(g_up, A_XG_PAD))
    if v_first is not None:
        xv = y[:, A_OFF_XV:A_OFF_XV + A_XV_PAD].astype(BF16)
        mix = jax.nn.sigmoid(v0 + _matmul(xv, pad_rows(v_up, A_XV_PAD)))
        v = v + (v_first - v) * mix
    heads = lambda t: t.reshape(T, H, N)
    kk = heads(k * k_k)
    kk = kk / jnp.maximum(jnp.sqrt(jnp.sum(kk * kk, axis=-1, keepdims=True)), 1e-12)
    kk = kk.reshape(T, RWKV_WIDTH)
    k = k * (1.0 + (a - 1.0) * k_a)

    ys = _rwkv_scan(r, lw, k, v, -kk, kk * a)

    yh = heads(ys)
    mean = jnp.mean(yh, axis=-1, keepdims=True)
    var = jnp.mean(jnp.square(yh - mean), axis=-1, keepdims=True)
    yn = ((yh - mean) * lax.rsqrt(var + LNX_EPS)).reshape(T, RWKV_WIDTH) * lnx_g + lnx_b
    bonus = jnp.sum(heads(r) * heads(k) * r_k, axis=-1, keepdims=True) * heads(v)
    mix_out = (yn + bonus.reshape(T, RWKV_WIDTH)) * g

    mem_out = _mem_attend(y, A_OFF_QM // MEM_WIDTH, mk, mv, mq_gain)
    heads_out = jnp.concatenate([mix_out, mem_out], axis=-1).astype(BF16)
    return _matmul(heads_out, w_out.astype(BF16), res=h), v


def _dilated_layer(h, mk, mv, mq_gain, ln1, w_in, q_gain, ks, vs, w_out):
    (xn,) = _rmsnorm(h, ln1, (BF16,))
    y = _matmul(xn, w_in.astype(BF16))
    parts = [_dilated_group(y, gi, dil, ks, vs, q_gain) for gi, (_, dil) in enumerate(DIL_PATTERNS)]
    att = _merge_groups(parts)
    mem_out = _mem_attend(y, DIL_Q_WIDTH // MEM_WIDTH, mk, mv, mq_gain)
    heads_out = jnp.concatenate([att, mem_out], axis=-1).astype(BF16)
    return _matmul(heads_out, w_out.astype(BF16), res=h)


def _shared_kv(h, g, w_kv, k_gain):
    T = h.shape[0]
    (xn,) = _rmsnorm(h, g, (BF16,))
    kv = _matmul(xn, w_kv.astype(BF16))
    k = kv[:, :KV_WIDTH].reshape(T, DIL_HEADS, ATT_HEAD_DIM)
    k = k * lax.rsqrt(jnp.mean(k * k, axis=-1, keepdims=True) + NORM_EPS) * k_gain
    return k.reshape(T, KV_WIDTH), kv[:, KV_WIDTH:]


def kernel(x, mem, a_ln1, a_w_in, a_w_vdown, a_mu, a_mu_vres, a_w0, a_w_up, a_a0, a_a_up, a_g_up, a_v0, a_v_up, a_k_k, a_k_a, a_r_k, a_lnx_g, a_lnx_b, a_w_out, b_ln1, b_w_in, b_q_norm, b_w_out, s_kv_norm, s_w_kv, s_k_norm, m_norm, m_w_kv, m_q_norm, m_k_norm, moe_ln, moe_router_group, moe_router_expert, moe_w_gate, moe_w_up, moe_w_down):
    Bsz, S, D = x.shape
    assert Bsz == 1 and S % (max(d for _, d in DIL_PATTERNS) * ATT_BLOCK) == 0
    depth = moe_ln.shape[0]
    h = x.reshape(S, D)
    mem2 = mem.reshape(mem.shape[1], D)
    v_first = None
    shared = None
    for l in range(depth):
        mk, mv = _mem_kv(mem2, m_norm[l], m_w_kv[l], m_k_norm[l])
        if l < N_A:
            i = l
            has_vres = i > 0
            h, v_i = _rwkv_layer(
                h, mk, mv, m_q_norm[l], a_ln1[i], a_w_in[i], a_mu[i],
                a_w_vdown[i - 1] if has_vres else None, a_mu_vres[i - 1] if has_vres else None,
                v_first, a_w0[i], a_w_up[i], a_a0[i], a_a_up[i], a_g_up[i],
                a_v0[i - 1] if has_vres else None, a_v_up[i - 1] if has_vres else None,
                a_k_k[i], a_k_a[i], a_r_k[i].reshape(1, RWKV_WIDTH // RWKV_HEAD, RWKV_HEAD),
                a_lnx_g[i], a_lnx_b[i], a_w_out[i])
            if i == 0:
                v_first = v_i
        else:
            j = l - N_A
            if shared is None:
                shared = _shared_kv(h, s_kv_norm, s_w_kv, s_k_norm)
            h = _dilated_layer(h, mk, mv, m_q_norm[l], b_ln1[j], b_w_in[j], b_q_norm[j], shared[0], shared[1], b_w_out[j])
        h = _hier_moe(h, moe_ln[l], moe_router_group[l], moe_router_expert[l], moe_w_gate, moe_w_up, moe_w_down, l)
    return h.reshape(Bsz, S, D)
```

```python
import functools

import jax
import jax.numpy as jnp
import numpy as np
from jax import lax
from jax.experimental import pallas as pl
from jax.experimental.pallas import tpu as pltpu

F32 = jnp.float32
BF16 = jnp.bfloat16

D_MODEL = 4096
RWKV_HEAD = 64
RWKV_WIDTH = 3072
LORA_DECAY = 128
LORA_AAA = 128
LORA_MV = 96
LORA_GATE = 480
LNX_EPS = 64e-5
MEM_HEADS = 4
MEM_WIDTH = 1024
MEM_HEAD_DIM = 256
ATT_HEAD_DIM = 128
DIL_PATTERNS = ((128, 1), (512, 4), (2048, 16))
N_DIL_GROUPS = 3
DIL_HEADS = 8
DIL_Q_WIDTH = 3072
KV_WIDTH = 1024
ATT_BLOCK = 128
N_GROUPS = 4
EXPERTS_PER_GROUP = 8
N_EXPERTS = 32
TOP_K = 2
D_FF = 384
NORM_EPS = 1e-6
N_A = 2

LANES = 128
SUBLANES = 8
VMEM_LIMIT_BYTES = 56 * 1024 * 1024

A_OFF_R = 0
A_OFF_K = RWKV_WIDTH
A_OFF_V = 2 * RWKV_WIDTH
A_OFF_QM = 3 * RWKV_WIDTH
A_OFF_XW = A_OFF_QM + MEM_WIDTH
A_OFF_XA = A_OFF_XW + LORA_DECAY
A_OFF_XG = A_OFF_XA + LORA_AAA
A_XG_PAD = 512
A_OFF_XV = A_OFF_XG + A_XG_PAD
A_XV_PAD = 128
A_IN_PAD = 11264

SCAN_CHUNK = 64
SCAN_TBLOCK = 128
SCAN_PAIRS = 24
MOE_ROWS = 256

_NN = (((1,), (0,)), ((), ()))
_NT = (((1,), (1,)), ((), ()))


def _cparams(*sem):
    return pltpu.CompilerParams(dimension_semantics=sem, vmem_limit_bytes=VMEM_LIMIT_BYTES)


def _rms_kernel(x_ref, g_ref, *o_refs):
    x = x_ref[...]
    y = x * lax.rsqrt(jnp.mean(x * x, axis=-1, keepdims=True) + NORM_EPS) * g_ref[...]
    for o_ref in o_refs:
        o_ref[...] = y.astype(o_ref.dtype)


def _rmsnorm(x, g, dtypes, tm=256):
    T, D = x.shape
    tm = min(tm, T)
    outs = pl.pallas_call(
        _rms_kernel,
        grid=(T // tm,),
        in_specs=[pl.BlockSpec((tm, D), lambda i: (i, 0)), pl.BlockSpec((1, D), lambda i: (0, 0))],
        out_specs=[pl.BlockSpec((tm, D), lambda i: (i, 0)) for _ in dtypes],
        out_shape=[jax.ShapeDtypeStruct((T, D), dt) for dt in dtypes],
        compiler_params=_cparams("parallel"),
        name="rmsnorm",
    )(x, g.reshape(1, D))
    return outs


def _rms_router_kernel(x_ref, g_ref, w_ref, logit_ref, packed_ref):
    x = x_ref[...]
    y = x * lax.rsqrt(jnp.mean(x * x, axis=-1, keepdims=True) + NORM_EPS) * g_ref[...]
    yh = y.astype(BF16)
    yl = (y - yh.astype(F32)).astype(BF16)
    w = w_ref[...]
    wh = w.astype(BF16)
    wl = (w - wh.astype(F32)).astype(BF16)
    d = lambda a, b: jnp.dot(a, b, preferred_element_type=F32)
    logit_ref[...] = d(yh, wh) + (d(yh, wl) + d(yl, wh))
    bits = lax.bitcast_convert_type(yh.astype(F32), jnp.uint32)
    half = bits.shape[1] // 2
    packed_ref[...] = (bits[:, :half] >> 16) | bits[:, half:]


def _rms_router(x, g, w_r, tm=256):
    T, D = x.shape
    N = w_r.shape[1]
    return pl.pallas_call(
        _rms_router_kernel,
        grid=(T // tm,),
        in_specs=[pl.BlockSpec((tm, D), lambda i: (i, 0)), pl.BlockSpec((1, D), lambda i: (0, 0)),
                  pl.BlockSpec((D, N), lambda i: (0, 0))],
        out_specs=[pl.BlockSpec((tm, N), lambda i: (i, 0)), pl.BlockSpec((tm, D // 2), lambda i: (i, 0))],
        out_shape=[jax.ShapeDtypeStruct((T, N), F32), jax.ShapeDtypeStruct((T, D // 2), jnp.uint32)],
        compiler_params=_cparams("parallel"),
        name="rms_router",
    )(x, g.reshape(1, D), w_r)


def _mm_kernel(x_ref, w_ref, o_ref):
    o_ref[...] = jnp.dot(x_ref[...], w_ref[...], preferred_element_type=F32)


def _mm_res_kernel(x_ref, w_ref, r_ref, o_ref):
    o_ref[...] = r_ref[...] + jnp.dot(x_ref[...], w_ref[...], preferred_element_type=F32)


def _mm3_kernel(x_ref, w_ref, o_ref):
    x = x_ref[...]
    w = w_ref[...]
    xh = x.astype(BF16)
    xl = (x - xh.astype(F32)).astype(BF16)
    wh = w.astype(BF16)
    wl = (w - wh.astype(F32)).astype(BF16)
    d = lambda a, b: jnp.dot(a, b, preferred_element_type=F32)
    o_ref[...] = d(xh, wh) + (d(xh, wl) + d(xl, wh))


def _matmul(x, w, res=None, tm=1024, tn=512, split3=False):
    M, K = x.shape
    N = w.shape[1]
    tm = min(tm, M)
    tn = min(tn, N)
    assert M % tm == 0 and N % tn == 0
    in_specs = [pl.BlockSpec((tm, K), lambda i, j: (i, 0)), pl.BlockSpec((K, tn), lambda i, j: (0, j))]
    args = [x, w]
    if split3:
        body = _mm3_kernel
    elif res is None:
        body = _mm_kernel
    else:
        body = _mm_res_kernel
        in_specs.append(pl.BlockSpec((tm, tn), lambda i, j: (i, j)))
        args.append(res)
    return pl.pallas_call(
        body,
        grid=(M // tm, N // tn),
        in_specs=in_specs,
        out_specs=pl.BlockSpec((tm, tn), lambda i, j: (i, j)),
        out_shape=jax.ShapeDtypeStruct((M, N), F32),
        compiler_params=_cparams("parallel", "parallel"),
        name="matmul",
    )(*args)


def _split(x):
    hi = x.astype(BF16)
    lo = (x - hi.astype(F32)).astype(BF16)
    return hi, lo


def _dot3(a, b, dims=_NN):
    ah, al = a
    bh, bl = b
    d = lambda x, y: lax.dot_general(x, y, dims, preferred_element_type=F32)
    return d(ah, bh) + (d(ah, bl) + d(al, bh))


def _dot1(a, b, dims=_NN):
    return lax.dot_general(a.astype(BF16), b.astype(BF16), dims, preferred_element_type=F32)


def _round_robin(gens):
    outs = [None] * len(gens)
    active = list(range(len(gens)))
    while active:
        for i in list(active):
            try:
                next(gens[i])
            except StopIteration as stop:
                outs[i] = stop.value
                active.remove(i)
    return outs


def _rwkv_scan_kernel(r_ref, lw_ref, k_ref, v_ref, a_ref, b_ref, o_ref, s_ref, *, n_chunks, n_pairs):
    C = SCAN_CHUNK
    H2 = 2 * C

    @pl.when(pl.program_id(1) == 0)
    def _():
        s_ref[...] = jnp.zeros_like(s_ref)

    lane = lax.broadcasted_iota(jnp.int32, (C, LANES), 1)
    head0 = lane < RWKV_HEAD
    trow = lax.broadcasted_iota(jnp.int32, (C, LANES), 0)
    tcol = lane & (C - 1)
    lane2 = lax.broadcasted_iota(jnp.int32, (H2, LANES), 1)
    row2 = lax.broadcasted_iota(jnp.int32, (H2, LANES), 0)
    blockdiag = (row2 < C) == (lane2 < RWKV_HEAD)
    tri_r = lax.broadcasted_iota(jnp.int32, (C, C), 0)
    tri_c = lax.broadcasted_iota(jnp.int32, (C, C), 1)
    ltri = jnp.where(tri_r >= tri_c, 1.0, 0.0).astype(BF16)
    arow = lax.broadcasted_iota(jnp.int32, (H2, 2 * H2), 0)
    acol = lax.broadcasted_iota(jnp.int32, (H2, 2 * H2), 1)
    a_t = arow & (C - 1)
    a_j = acol & (C - 1)
    amask = a_t + jnp.where(arow < C, 0, 1) > a_j
    n_levels = C.bit_length() - 1
    eye2 = jnp.where(trow == tcol, 1.0, 0.0)

    def level_mask(lvl):
        tb_, jb_ = trow >> lvl, tcol >> lvl
        return (tb_ - jb_) * (tb_ & 1) == 1

    def stack2(x):
        z = jnp.zeros_like(x)
        return jnp.concatenate([jnp.where(head0, x, z), jnp.where(head0, z, x)], axis=0)

    def pair_chunk(r, lw, k, v, a, b, s):
        l1 = lw.astype(BF16)
        rem = lw - l1.astype(F32)
        l2 = rem.astype(BF16)
        l3 = (rem - l2.astype(F32)).astype(BF16)
        dd = lambda x, y: jnp.dot(x, y, preferred_element_type=F32)
        cum = dd(ltri, l1) + (dd(ltri, l2) + dd(ltri, l3))
        yield
        p_inc = jnp.exp(cum)
        p_exc = jnp.exp(cum - lw)
        p_inv = jnp.exp(-cum)
        at = a * p_exc
        rt = r * p_inc
        bt = b * p_inv
        kt = k * p_inv
        p_tot = p_inc[C - 1:C, :]

        x_ar = _split(jnp.concatenate([at, rt], axis=0))
        w_bk = jnp.concatenate([stack2(bt), stack2(kt)], axis=0).astype(BF16)
        amat = jnp.where(amask, lax.dot_general(x_ar[0], w_bk, _NT, preferred_element_type=F32), 0.0)
        xs = _dot3(x_ar, _split(s), _NT)
        yield

        n_ab = amat[:C, :H2]
        a_ak = amat[:C, H2:]
        a_r = amat[C:, :]
        v2 = stack2(v)
        rhs = xs[:C] + _dot1(a_ak, v2)
        yield

        tinv = eye2 + jnp.where(level_mask(0), n_ab, 0.0)
        for lvl in range(1, n_levels):
            nt = _dot1(jnp.where(level_mask(lvl), n_ab, 0.0), stack2(tinv))
            yield
            tinv = tinv + _dot1(tinv, stack2(nt))
            yield
        u = _dot1(tinv, stack2(rhs))
        yield

        uv = jnp.concatenate([stack2(u), v2], axis=0)
        y = xs[C:] + _dot1(a_r, uv)
        yield

        uv_t = jnp.concatenate([u, v], axis=0).T
        bk = jnp.concatenate([bt, kt], axis=0)
        ds = _dot3(_split(uv_t), _split(bk))
        return y, (s + jnp.where(blockdiag, ds, 0.0)) * p_tot

    def chunk(ci, carry):
        sl = pl.ds(pl.multiple_of(ci * C, C), C)
        lanes = [slice(p * LANES, (p + 1) * LANES) for p in range(n_pairs)]
        ins = [tuple(ref[sl, ls] for ref in (r_ref, lw_ref, k_ref, v_ref, a_ref, b_ref)) + (s_ref[p],)
               for p, ls in enumerate(lanes)]
        outs = _round_robin([pair_chunk(*args) for args in ins])
        for p, ls in enumerate(lanes):
            o_ref[sl, ls] = outs[p][0]
            s_ref[p] = outs[p][1]
        return carry

    lax.fori_loop(0, n_chunks, chunk, 0)


def _rwkv_scan(r, lw, k, v, a, b):
    T, W = r.shape
    tb = min(SCAN_TBLOCK, T)
    n_pairs = SCAN_PAIRS if W % (SCAN_PAIRS * LANES) == 0 else 1
    wb = n_pairs * LANES
    assert T % tb == 0 and tb % SCAN_CHUNK == 0 and W % wb == 0
    spec = pl.BlockSpec((tb, wb), lambda p, t: (t, p))
    return pl.pallas_call(
        functools.partial(_rwkv_scan_kernel, n_chunks=tb // SCAN_CHUNK, n_pairs=n_pairs),
        grid=(W // wb, T // tb),
        in_specs=[spec] * 6,
        out_specs=spec,
        out_shape=jax.ShapeDtypeStruct((T, W), F32),
        scratch_shapes=[pltpu.VMEM((n_pairs, LANES, LANES), F32)],
        compiler_params=_cparams("parallel", "arbitrary"),
        name="rwkv7_scan",
    )(r, lw, k, v, a, b)


def _mem_attn_kernel(q_ref, k_ref, v_ref, g_ref, o_ref):
    scale = MEM_HEAD_DIM ** -0.5
    for h in range(MEM_HEADS):
        hs = slice(h * MEM_HEAD_DIM, (h + 1) * MEM_HEAD_DIM)
        q = q_ref[:, hs]
        qn = q * lax.rsqrt(jnp.mean(q * q, axis=-1, keepdims=True) + NORM_EPS) * g_ref[...]
        s = lax.dot_general(qn.astype(BF16), k_ref[:, hs], _NT, preferred_element_type=F32) * scale
        m = jnp.max(s, axis=-1, keepdims=True)
        p = jnp.exp(s - m)
        l = jnp.sum(p, axis=-1, keepdims=True)
        o = jnp.dot(p.astype(BF16), v_ref[:, hs], preferred_element_type=F32)
        o_ref[:, hs] = o / l


def _mem_attend(y, col_block, mk, mv, q_gain, tm=512):
    T = y.shape[0]
    M = mk.shape[0]
    tm = min(tm, T)
    return pl.pallas_call(
        _mem_attn_kernel,
        grid=(T // tm,),
        in_specs=[
            pl.BlockSpec((tm, MEM_WIDTH), lambda i: (i, col_block)),
            pl.BlockSpec((M, MEM_WIDTH), lambda i: (0, 0)),
            pl.BlockSpec((M, MEM_WIDTH), lambda i: (0, 0)),
            pl.BlockSpec((1, MEM_HEAD_DIM), lambda i: (0, 0)),
        ],
        out_specs=pl.BlockSpec((tm, MEM_WIDTH), lambda i: (i, 0)),
        out_shape=jax.ShapeDtypeStruct((T, MEM_WIDTH), F32),
        compiler_params=_cparams("parallel"),
        name="mem_attend",
    )(y, mk, mv, q_gain.reshape(1, MEM_HEAD_DIM))


def _dil_attn_kernel(sl_ref, q_ref, kp_ref, kc_ref, vp_ref, vc_ref, g_ref, o_ref, lse_ref, *, dil, heads):
    n = pl.program_id(0)
    h0 = pl.program_id(1) * heads
    B = ATT_BLOCK
    n_back = B
    scale = ATT_HEAD_DIM ** -0.5
    qi = lax.broadcasted_iota(jnp.int32, (B, 2 * B), 0)
    ki = lax.broadcasted_iota(jnp.int32, (B, 2 * B), 1)
    j = B + qi - ki
    valid = (j >= 0) & (j <= n_back) & ((ki >= B) | (n > 0))
    dist = (j * dil).astype(F32)
    for c in range(dil):
        rows = pl.ds(c, B, stride=dil) if dil > 1 else slice(None)
        for h in range(heads):
            hs = slice(h * ATT_HEAD_DIM, (h + 1) * ATT_HEAD_DIM)
            q = q_ref[rows, hs]
            qn = q * lax.rsqrt(jnp.mean(q * q, axis=-1, keepdims=True) + NORM_EPS) * g_ref[...]
            kcat = jnp.concatenate([kp_ref[rows, hs], kc_ref[rows, hs]], axis=0).astype(BF16)
            vcat = jnp.concatenate([vp_ref[rows, hs], vc_ref[rows, hs]], axis=0).astype(BF16)
            s = lax.dot_general(qn.astype(BF16), kcat, _NT, preferred_element_type=F32) * scale
            s = jnp.where(valid, s - sl_ref[h0 + h] * dist, -1e30)
            m = jnp.max(s, axis=-1, keepdims=True)
            p = jnp.exp(s - m)
            l = jnp.sum(p, axis=-1, keepdims=True)
            o = jnp.dot(p.astype(BF16), vcat, preferred_element_type=F32)
            o_ref[rows, hs] = o / l
            lse_ref[rows, hs] = jnp.broadcast_to(m + jnp.log(l), (B, ATT_HEAD_DIM))


def _dilated_group(y, gi, dil, k, v, q_gain):
    T, W = y.shape
    n_heads_total = N_DIL_GROUPS * DIL_HEADS
    slopes = jnp.asarray([2.0 ** (-8.0 * (gi * DIL_HEADS + h + 1) / n_heads_total) for h in range(DIL_HEADS)], F32)
    rows = ATT_BLOCK * dil
    cols = KV_WIDTH if dil == 1 else ATT_HEAD_DIM
    ncb = KV_WIDTH // cols
    blk = (rows, cols)
    cur = lambda n, hb: (n, hb)
    prev = lambda n, hb: (jnp.maximum(n - 1, 0), hb)
    o, lse = pl.pallas_call(
        functools.partial(_dil_attn_kernel, dil=dil, heads=cols // ATT_HEAD_DIM),
        grid=(T // rows, ncb),
        in_specs=[
            pl.BlockSpec(memory_space=pltpu.SMEM),
            pl.BlockSpec(blk, lambda n, hb: (n, gi * ncb + hb)),
            pl.BlockSpec(blk, prev), pl.BlockSpec(blk, cur),
            pl.BlockSpec(blk, prev), pl.BlockSpec(blk, cur),
            pl.BlockSpec((1, ATT_HEAD_DIM), lambda n, hb: (0, 0)),
        ],
        out_specs=[pl.BlockSpec(blk, cur), pl.BlockSpec(blk, cur)],
        out_shape=[jax.ShapeDtypeStruct((T, KV_WIDTH), F32)] * 2,
        compiler_params=_cparams("parallel", "parallel"),
        name=f"dilated_attn_d{dil}",
    )(slopes, y, k, k, v, v, q_gain.reshape(1, ATT_HEAD_DIM))
    return o, lse


def _merge_kernel(o0, l0, o1, l1, o2, l2, out_ref):
    a0, a1, a2 = l0[...], l1[...], l2[...]
    m = jnp.maximum(jnp.maximum(a0, a1), a2)
    w0, w1, w2 = jnp.exp(a0 - m), jnp.exp(a1 - m), jnp.exp(a2 - m)
    out_ref[...] = (w0 * o0[...] + w1 * o1[...] + w2 * o2[...]) / (w0 + w1 + w2)


def _merge_groups(parts, tm=512):
    T, W = parts[0][0].shape
    tm = min(tm, T)
    spec = pl.BlockSpec((tm, W), lambda i: (i, 0))
    flat = [t for pr in parts for t in pr]
    return pl.pallas_call(
        _merge_kernel,
        grid=(T // tm,),
        in_specs=[spec] * 6,
        out_specs=spec,
        out_shape=jax.ShapeDtypeStruct((T, W), F32),
        compiler_params=_cparams("parallel"),
        name="dilated_merge",
    )(*flat)


def _moe_up_kernel(tok_ref, be_ref, bf_ref, bv_ref, x_hbm, wg_ref, wu_ref, o_ref, xbuf, sem, wg16, wu16):
    i = pl.program_id(0)
    n_blk = pl.num_programs(0)
    R = MOE_ROWS

    def row_copy(tok, slot, r):
        return pltpu.make_async_copy(x_hbm.at[pl.ds(tok, 1)], xbuf.at[slot, pl.ds(r, 1)], sem.at[slot])

    def start_gather(blk, slot):
        def body(r, carry):
            row_copy(tok_ref[blk * R + r], slot, r).start()
            return carry
        lax.fori_loop(0, bv_ref[blk], body, 0)

    @pl.when(i == 0)
    def _():
        xbuf[...] = jnp.zeros_like(xbuf)
        start_gather(0, 0)

    nxt = jnp.minimum(i + 1, n_blk - 1)

    @pl.when((i + 1 < n_blk) & (bv_ref[nxt] > 0))
    def _():
        start_gather(i + 1, (i + 1) & 1)

    @pl.when(bf_ref[i] > 0)
    def _():
        wg16[...] = wg_ref[...].astype(BF16)
        wu16[...] = wu_ref[...].astype(BF16)

    @pl.when(bv_ref[i] > 0)
    def _():
        slot = i & 1

        def wait_row(r, carry):
            row_copy(0, slot, r).wait()
            return carry
        lax.fori_loop(0, bv_ref[i], wait_row, 0)
        w = xbuf[slot]
        lo = lax.bitcast_convert_type(w << 16, F32)
        hi = lax.bitcast_convert_type(w & jnp.uint32(0xFFFF0000), F32)
        x = jnp.concatenate([lo, hi], axis=1).astype(BF16)
        hg = jnp.dot(x, wg16[...], preferred_element_type=F32)
        hu = jnp.dot(x, wu16[...], preferred_element_type=F32)
        o_ref[...] = (hg * jax.nn.sigmoid(hg) * hu).astype(BF16)

    @pl.when(bv_ref[i] == 0)
    def _():
        o_ref[...] = jnp.zeros_like(o_ref)


def _moe_down_kernel(be_ref, bf_ref, bv_ref, h_ref, wd_ref, g_ref, o_ref, wd16):
    i = pl.program_id(0)

    @pl.when(bf_ref[i] > 0)
    def _():
        wd16[...] = wd_ref[...].astype(BF16)

    @pl.when(bv_ref[i] > 0)
    def _():
        o_ref[...] = jnp.dot(h_ref[...], wd16[...], preferred_element_type=F32) * g_ref[...]

    @pl.when(bv_ref[i] == 0)
    def _():
        o_ref[...] = jnp.zeros_like(o_ref)


def _moe_experts(xn, slot_tok, slot_gate, blk_expert, blk_first, blk_valid, w_gate, w_up, w_down, layer):
    D = w_gate.shape[2]
    NR = slot_tok.shape[0]
    R = MOE_ROWS
    n_blk = NR // R
    F = w_gate.shape[3]
    hb = pl.pallas_call(
        _moe_up_kernel,
        grid_spec=pltpu.PrefetchScalarGridSpec(
            num_scalar_prefetch=4,
            grid=(n_blk,),
            in_specs=[pl.BlockSpec(memory_space=pl.ANY),
                      pl.BlockSpec((None, None, D, F), lambda i, tok, be, bf, bv: (layer, be[i], 0, 0)),
                      pl.BlockSpec((None, None, D, F), lambda i, tok, be, bf, bv: (layer, be[i], 0, 0))],
            out_specs=pl.BlockSpec((R, F), lambda i, tok, be, bf, bv: (i, 0)),
            scratch_shapes=[pltpu.VMEM((2, R, D // 2), jnp.uint32), pltpu.SemaphoreType.DMA((2,)),
                            pltpu.VMEM((D, F), BF16), pltpu.VMEM((D, F), BF16)],
        ),
        out_shape=jax.ShapeDtypeStruct((NR, F), BF16),
        compiler_params=_cparams("arbitrary"),
        name="moe_up",
    )(slot_tok, blk_expert, blk_first, blk_valid, xn, w_gate, w_up)
    row = lambda i, be, bf, bv: (i, 0)
    wsel = lambda i, be, bf, bv: (layer, be[i], 0, 0)
    return pl.pallas_call(
        _moe_down_kernel,
        grid_spec=pltpu.PrefetchScalarGridSpec(
            num_scalar_prefetch=3,
            grid=(n_blk,),
            in_specs=[pl.BlockSpec((R, F), row), pl.BlockSpec((None, None, F, D), wsel), pl.BlockSpec((R, 1), row)],
            out_specs=pl.BlockSpec((R, D), row),
            scratch_shapes=[pltpu.VMEM((F, D), BF16)],
        ),
        out_shape=jax.ShapeDtypeStruct((NR, D), F32),
        compiler_params=_cparams("arbitrary"),
        name="moe_down",
    )(blk_expert, blk_first, blk_valid, hb, w_down, slot_gate.reshape(NR, 1))


def _hier_moe(h, ln_g, w_rg, w_re, w_gate, w_up, w_down, layer):
    T, D = h.shape
    R = MOE_ROWS
    n_rout = N_GROUPS + N_EXPERTS
    w_r = jnp.pad(jnp.concatenate([w_rg, w_re], axis=1), ((0, 0), (0, LANES - n_rout)))
    logits, xn_packed = _rms_router(h, ln_g, w_r)
    gl = logits[:, :N_GROUPS]
    g_sel = jnp.argmax(gl, axis=-1)
    p_group = jnp.take_along_axis(jax.nn.softmax(gl, axis=-1), g_sel[:, None], axis=-1)
    el = logits[:, N_GROUPS:n_rout].reshape(T, N_GROUPS, EXPERTS_PER_GROUP)
    el = jnp.take_along_axis(el, g_sel[:, None, None], axis=1)[:, 0]
    top_v, top_i = lax.top_k(el, TOP_K)
    gate = p_group * jax.nn.softmax(top_v, axis=-1)
    expert = (g_sel[:, None] * EXPERTS_PER_GROUP + top_i).reshape(-1).astype(jnp.int32)
    n_assign = T * TOP_K
    tok = jnp.arange(n_assign, dtype=jnp.int32) // TOP_K
    order = jnp.argsort(expert)
    e_sorted = expert[order]
    counts = jnp.bincount(expert, length=N_EXPERTS)
    starts = jnp.cumsum(counts) - counts
    padded = (counts + R - 1) // R * R
    p_end = jnp.cumsum(padded)
    p_start = p_end - padded
    dest = (p_start[e_sorted] + jnp.arange(n_assign, dtype=jnp.int32) - starts[e_sorted]).astype(jnp.int32)
    n_blk = -(-n_assign // R) + N_EXPERTS
    slot_tok = jnp.zeros((n_blk * R,), jnp.int32).at[dest].set(tok[order])
    slot_gate = jnp.zeros((n_blk * R,), F32).at[dest].set(gate.reshape(-1)[order])
    blk_start = jnp.arange(n_blk, dtype=jnp.int32) * R
    blk_valid = (blk_start < p_end[-1]).astype(jnp.int32)
    last_valid = jnp.maximum(p_end[-1] // R - 1, 0)
    blk_expert = jnp.minimum(jnp.searchsorted(p_end, blk_start, side='right'), N_EXPERTS - 1).astype(jnp.int32)
    blk_expert = jnp.where(blk_valid > 0, blk_expert, blk_expert[last_valid])
    blk_first = jnp.concatenate([jnp.ones((1,), jnp.int32), (blk_expert[1:] != blk_expert[:-1]).astype(jnp.int32)])
    blk_rows = jnp.clip((p_start + counts)[blk_expert] - blk_start, 0, R).astype(jnp.int32) * blk_valid
    pos = jnp.zeros((n_assign,), jnp.int32).at[order].set(dest)

    ys = _moe_experts(xn_packed, slot_tok, slot_gate, blk_expert, blk_first, blk_rows, w_gate, w_up, w_down, layer)
    pos2 = pos.reshape(T, TOP_K)
    rows_of = lambda idx: ys.at[idx].get(mode="promise_in_bounds")
    return h + (rows_of(pos2[:, 0]) + rows_of(pos2[:, 1]))


def _mem_kv(mem, g, w_kv, k_gain):
    M = mem.shape[0]
    (mn,) = _rmsnorm(mem, g, (BF16,))
    kv = _matmul(mn, w_kv.astype(BF16))
    k = kv[:, :MEM_WIDTH].reshape(M, MEM_HEADS, MEM_HEAD_DIM)
    k = k * lax.rsqrt(jnp.mean(k * k, axis=-1, keepdims=True) + NORM_EPS) * k_gain
    return k.reshape(M, MEM_WIDTH).astype(BF16), kv[:, MEM_WIDTH:].astype(BF16)


def _pad_cols(w, n):
    return jnp.pad(w, ((0, 0), (0, n - w.shape[1])))


def _a_layer_weights(w_in, mu, w_vdown, mu_vres):
    D = w_in.shape[0]
    c = np.cumsum([RWKV_WIDTH, RWKV_WIDTH, RWKV_WIDTH, LORA_DECAY, LORA_AAA, LORA_GATE, MEM_WIDTH])
    rkv, xw, xa, xg, qm = (slice(0, c[2]), slice(c[2], c[3]), slice(c[3], c[4]), slice(c[4], c[5]), slice(c[5], c[6]))
    if w_vdown is None:
        w_vdown = jnp.zeros((D, LORA_MV), w_in.dtype)
        mu_vres = jnp.zeros((LORA_MV,), mu.dtype)
    tail = A_IN_PAD - A_OFF_XV
    w = jnp.concatenate([w_in[:, rkv], w_in[:, qm], w_in[:, xw], w_in[:, xa],
                         _pad_cols(w_in[:, xg], A_XG_PAD), _pad_cols(w_vdown, tail)], axis=1)
    m = jnp.concatenate([mu[rkv], mu[qm], mu[xw], mu[xa],
                         jnp.pad(mu[xg], (0, A_XG_PAD - LORA_GATE)), jnp.pad(mu_vres, (0, tail - LORA_MV))])
    return w.astype(BF16), m


def _rwkv_layer(h, mk, mv, mq_gain, ln1, w_in, mu, w_vdown, mu_vres, v_first, w0, w_up, a0, a_up, g_up,
                v0, v_up, k_k, k_a, r_k, lnx_g, lnx_b, w_out):
    T, D = h.shape
    H, N = RWKV_WIDTH // RWKV_HEAD, RWKV_HEAD
    (xn,) = _rmsnorm(h, ln1, (BF16,))
    w_pad, mu_pad = _a_layer_weights(w_in, mu, w_vdown, mu_vres)
    y = _matmul(xn, w_pad)
    y_prev = jnp.pad(y, ((1, 0), (0, 0)))[:-1]
    y = y + mu_pad * (y_prev - y)

    r = y[:, A_OFF_R:A_OFF_R + RWKV_WIDTH]
    k = y[:, A_OFF_K:A_OFF_K + RWKV_WIDTH]
    v = y[:, A_OFF_V:A_OFF_V + RWKV_WIDTH]
    xw = jnp.tanh(y[:, A_OFF_XW:A_OFF_XW + LORA_DECAY]).astype(BF16)
    xa = y[:, A_OFF_XA:A_OFF_XA + LORA_AAA].astype(BF16)
    xg = jax.nn.sigmoid(y[:, A_OFF_XG:A_OFF_XG + A_XG_PAD]).astype(BF16)
    pad_rows = lambda w, n: jnp.pad(w, ((0, n - w.shape[0]), (0, 0))).astype(BF16)

    w_log = -jax.nn.softplus(-(w0 + _matmul(xw, w_up.astype(BF16)))) - 0.5
    lw = -jnp.exp(w_log)
    a = jax.nn.sigmoid(a0 + _matmul(xa, a_up.astype(BF16)))
    g = _matmul(xg, pad_rows(g_up, A_XG_PAD))
    if v_first is not None:
        xv = y[:, A_OFF_XV:A_OFF_XV + A_XV_PAD].astype(BF16)
        mix = jax.nn.sigmoid(v0 + _matmul(xv, pad_rows(v_up, A_XV_PAD)))
        v = v + (v_first - v) * mix
    heads = lambda t: t.reshape(T, H, N)
    kk = heads(k * k_k)
    kk = kk / jnp.maximum(jnp.sqrt(jnp.sum(kk * kk, axis=-1, keepdims=True)), 1e-12)
    kk = kk.reshape(T, RWKV_WIDTH)
    k = k * (1.0 + (a - 1.0) * k_a)

    ys = _rwkv_scan(r, lw, k, v, -kk, kk * a)

    yh = heads(ys)
    mean = jnp.mean(yh, axis=-1, keepdims=True)
    var = jnp.mean(jnp.square(yh - mean), axis=-1, keepdims=True)
    yn = ((yh - mean) * lax.rsqrt(var + LNX_EPS)).reshape(T, RWKV_WIDTH) * lnx_g + lnx_b
    bonus = jnp.sum(heads(r) * heads(k) * r_k, axis=-1, keepdims=True) * heads(v)
    mix_out = (yn + bonus.reshape(T, RWKV_WIDTH)) * g

    mem_out = _mem_attend(y, A_OFF_QM // MEM_WIDTH, mk, mv, mq_gain)
    heads_out = jnp.concatenate([mix_out, mem_out], axis=-1).astype(BF16)
    return _matmul(heads_out, w_out.astype(BF16), res=h), v


def _dilated_layer(h, mk, mv, mq_gain, ln1, w_in, q_gain, ks, vs, w_out):
    (xn,) = _rmsnorm(h, ln1, (BF16,))
    y = _matmul(xn, w_in.astype(BF16))
    parts = [_dilated_group(y, gi, dil, ks, vs, q_gain) for gi, (_, dil) in enumerate(DIL_PATTERNS)]
    att = _merge_groups(parts)
    mem_out = _mem_attend(y, DIL_Q_WIDTH // MEM_WIDTH, mk, mv, mq_gain)
    heads_out = jnp.concatenate([att, mem_out], axis=-1).astype(BF16)
    return _matmul(heads_out, w_out.astype(BF16), res=h)


def _shared_kv(h, g, w_kv, k_gain):
    T = h.shape[0]
    (xn,) = _rmsnorm(h, g, (BF16,))
    kv = _matmul(xn, w_kv.astype(BF16))
    k = kv[:, :KV_WIDTH].reshape(T, DIL_HEADS, ATT_HEAD_DIM)
    k = k * lax.rsqrt(jnp.mean(k * k, axis=-1, keepdims=True) + NORM_EPS) * k_gain
    return k.reshape(T, KV_WIDTH), kv[:, KV_WIDTH:]


def kernel(x, mem, a_ln1, a_w_in, a_w_vdown, a_mu, a_mu_vres, a_w0, a_w_up, a_a0, a_a_up, a_g_up, a_v0, a_v_up, a_k_k, a_k_a, a_r_k, a_lnx_g, a_lnx_b, a_w_out, b_ln1, b_w_in, b_q_norm, b_w_out, s_kv_norm, s_w_kv, s_k_norm, m_norm, m_w_kv, m_q_norm, m_k_norm, moe_ln, moe_router_group, moe_router_expert, moe_w_gate, moe_w_up, moe_w_down):
    Bsz, S, D = x.shape
    assert Bsz == 1 and S % (max(d for _, d in DIL_PATTERNS) * ATT_BLOCK) == 0
    depth = moe_ln.shape[0]
    h = x.reshape(S, D)
    mem2 = mem.reshape(mem.shape[1], D)
    v_first = None
    shared = None
    for l in range(depth):
        mk, mv = _mem_kv(mem2, m_norm[l], m_w_kv[l], m_k_norm[l])
        if l < N_A:
            i = l
            has_vres = i > 0
            h, v_i = _rwkv_layer(
                h, mk, mv, m_q_norm[l], a_ln1[i], a_w_in[i], a_mu[i],
                a_w_vdown[i - 1] if has_vres else None, a_mu_vres[i - 1] if has_vres else None,
                v_first, a_w0[i], a_w_up[i], a_a0[i], a_a_up[i], a_g_up[i],
                a_v0[i - 1] if has_vres else None, a_v_up[i - 1] if has_vres else None,
                a_k_k[i], a_k_a[i], a_r_k[i].reshape(1, RWKV_WIDTH // RWKV_HEAD, RWKV_HEAD),
                a_lnx_g[i], a_lnx_b[i], a_w_out[i])
            if i == 0:
                v_first = v_i
        else:
            j = l - N_A
            if shared is None:
                shared = _shared_kv(h, s_kv_norm, s_w_kv, s_k_norm)
            h = _dilated_layer(h, mk, mv, m_q_norm[l], b_ln1[j], b_w_in[j], b_q_norm[j], shared[0], shared[1], b_w_out[j])
        h = _hier_moe(h, moe_ln[l], moe_router_group[l], moe_router_expert[l], moe_w_gate, moe_w_up, moe_w_down, l)
    return h.reshape(Bsz, S, D)
```

```python
import functools

import jax
import jax.numpy as jnp
import numpy as np
from jax import lax
from jax.experimental import pallas as pl
from jax.experimental.pallas import tpu as pltpu

F32 = jnp.float32
BF16 = jnp.bfloat16

D_MODEL = 4096
RWKV_HEAD = 64
RWKV_WIDTH = 3072
LORA_DECAY = 128
LORA_AAA = 128
LORA_MV = 96
LORA_GATE = 480
LNX_EPS = 64e-5
MEM_HEADS = 4
MEM_WIDTH = 1024
MEM_HEAD_DIM = 256
ATT_HEAD_DIM = 128
DIL_PATTERNS = ((128, 1), (512, 4), (2048, 16))
N_DIL_GROUPS = 3
DIL_HEADS = 8
DIL_Q_WIDTH = 3072
KV_WIDTH = 1024
ATT_BLOCK = 128
N_GROUPS = 4
EXPERTS_PER_GROUP = 8
N_EXPERTS = 32
TOP_K = 2
D_FF = 384
NORM_EPS = 1e-6
N_A = 2

LANES = 128
SUBLANES = 8
VMEM_LIMIT_BYTES = 56 * 1024 * 1024

A_OFF_R = 0
A_OFF_K = RWKV_WIDTH
A_OFF_V = 2 * RWKV_WIDTH
A_OFF_QM = 3 * RWKV_WIDTH
A_OFF_XG = A_OFF_QM + MEM_WIDTH
A_XG_PAD = 512
A_OFF_XW = A_OFF_XG + A_XG_PAD
A_OFF_XA = A_OFF_XW + LORA_DECAY
A_OFF_XV = A_OFF_XA + LORA_AAA
A_XV_PAD = 128
A_IN_PAD = 11264

SCAN_CHUNK = 64
SCAN_TBLOCK = 128
MOE_ROWS = 256

_NN = (((1,), (0,)), ((), ()))
_NT = (((1,), (1,)), ((), ()))


def _cparams(*sem):
    return pltpu.CompilerParams(dimension_semantics=sem, vmem_limit_bytes=VMEM_LIMIT_BYTES)


def _rms_kernel(x_ref, g_ref, *o_refs):
    x = x_ref[...]
    y = x * lax.rsqrt(jnp.mean(x * x, axis=-1, keepdims=True) + NORM_EPS) * g_ref[...]
    for o_ref in o_refs:
        o_ref[...] = y.astype(o_ref.dtype)


def _rmsnorm(x, g, dtypes, tm=256):
    T, D = x.shape
    tm = min(tm, T)
    outs = pl.pallas_call(
        _rms_kernel,
        grid=(T // tm,),
        in_specs=[pl.BlockSpec((tm, D), lambda i: (i, 0)), pl.BlockSpec((1, D), lambda i: (0, 0))],
        out_specs=[pl.BlockSpec((tm, D), lambda i: (i, 0)) for _ in dtypes],
        out_shape=[jax.ShapeDtypeStruct((T, D), dt) for dt in dtypes],
        compiler_params=_cparams("parallel"),
        name="rmsnorm",
    )(x, g.reshape(1, D))
    return outs


def _rms_router_kernel(x_ref, g_ref, w_ref, logit_ref, packed_ref):
    x = x_ref[...]
    y = x * lax.rsqrt(jnp.mean(x * x, axis=-1, keepdims=True) + NORM_EPS) * g_ref[...]
    yh = y.astype(BF16)
    yl = (y - yh.astype(F32)).astype(BF16)
    w = w_ref[...]
    wh = w.astype(BF16)
    wl = (w - wh.astype(F32)).astype(BF16)
    d = lambda a, b: jnp.dot(a, b, preferred_element_type=F32)
    logit_ref[...] = d(yh, wh) + (d(yh, wl) + d(yl, wh))
    bits = lax.bitcast_convert_type(yh.astype(F32), jnp.uint32)
    half = bits.shape[1] // 2
    packed_ref[...] = (bits[:, :half] >> 16) | bits[:, half:]


def _rms_router(x, g, w_r, tm=256):
    T, D = x.shape
    N = w_r.shape[1]
    return pl.pallas_call(
        _rms_router_kernel,
        grid=(T // tm,),
        in_specs=[pl.BlockSpec((tm, D), lambda i: (i, 0)), pl.BlockSpec((1, D), lambda i: (0, 0)),
                  pl.BlockSpec((D, N), lambda i: (0, 0))],
        out_specs=[pl.BlockSpec((tm, N), lambda i: (i, 0)), pl.BlockSpec((tm, D // 2), lambda i: (i, 0))],
        out_shape=[jax.ShapeDtypeStruct((T, N), F32), jax.ShapeDtypeStruct((T, D // 2), jnp.uint32)],
        compiler_params=_cparams("parallel"),
        name="rms_router",
    )(x, g.reshape(1, D), w_r)


def _mm_kernel(x_ref, w_ref, o_ref):
    o_ref[...] = jnp.dot(x_ref[...], w_ref[...], preferred_element_type=F32)


def _mm_res_kernel(x_ref, w_ref, r_ref, o_ref):
    o_ref[...] = r_ref[...] + jnp.dot(x_ref[...], w_ref[...], preferred_element_type=F32)


def _mm_shift_kernel(x_ref, xp_ref, w_ref, mu_ref, o_ref):
    w = w_ref[...]
    y = jnp.dot(x_ref[...], w, preferred_element_type=F32)
    tail = jnp.dot(xp_ref[...], w, preferred_element_type=F32)
    last = jnp.where(pl.program_id(0) > 0, tail[tail.shape[0] - 1:, :], 0.0)
    row = lax.broadcasted_iota(jnp.int32, y.shape, 0)
    y_prev = jnp.where(row == 0, last, pltpu.roll(y, 1, axis=0))
    o_ref[...] = y + mu_ref[...] * (y_prev - y)


def _matmul_shift(x, w, mu, tm=1024, tn=512):
    M, K = x.shape
    N = w.shape[1]
    tm = min(tm, M)
    tp = 2 * SUBLANES
    assert M % tm == 0 and N % tn == 0 and tm % tp == 0
    return pl.pallas_call(
        _mm_shift_kernel,
        grid=(M // tm, N // tn),
        in_specs=[pl.BlockSpec((tm, K), lambda i, j: (i, 0)),
                  pl.BlockSpec((tp, K), lambda i, j: (jnp.maximum(i * (tm // tp) - 1, 0), 0)),
                  pl.BlockSpec((K, tn), lambda i, j: (0, j)),
                  pl.BlockSpec((1, tn), lambda i, j: (0, j))],
        out_specs=pl.BlockSpec((tm, tn), lambda i, j: (i, j)),
        out_shape=jax.ShapeDtypeStruct((M, N), F32),
        compiler_params=_cparams("parallel", "parallel"),
        name="matmul_shift",
    )(x, x, w, mu.reshape(1, N))


def _mm2_res_kernel(x1_ref, x2_ref, w1_ref, w2_ref, r_ref, o_ref):
    acc = jnp.dot(x1_ref[...], w1_ref[...], preferred_element_type=F32)
    o_ref[...] = r_ref[...] + (acc + jnp.dot(x2_ref[...], w2_ref[...], preferred_element_type=F32))


def _matmul2_res(x1, x2, w, res, tm=1024, tn=512):
    M, K1 = x1.shape
    K2 = x2.shape[1]
    N = w.shape[1]
    tm = min(tm, M)
    assert M % tm == 0 and N % tn == 0 and K1 % K2 == 0 and w.shape[0] == K1 + K2
    return pl.pallas_call(
        _mm2_res_kernel,
        grid=(M // tm, N // tn),
        in_specs=[pl.BlockSpec((tm, K1), lambda i, j: (i, 0)), pl.BlockSpec((tm, K2), lambda i, j: (i, 0)),
                  pl.BlockSpec((K1, tn), lambda i, j: (0, j)), pl.BlockSpec((K2, tn), lambda i, j: (K1 // K2, j)),
                  pl.BlockSpec((tm, tn), lambda i, j: (i, j))],
        out_specs=pl.BlockSpec((tm, tn), lambda i, j: (i, j)),
        out_shape=jax.ShapeDtypeStruct((M, N), F32),
        compiler_params=_cparams("parallel", "parallel"),
        name="matmul_out",
    )(x1, x2, w, w, res)


def _matmul(x, w, res=None, tm=1024, tn=512):
    M, K = x.shape
    N = w.shape[1]
    tm = min(tm, M)
    tn = min(tn, N)
    assert M % tm == 0 and N % tn == 0
    in_specs = [pl.BlockSpec((tm, K), lambda i, j: (i, 0)), pl.BlockSpec((K, tn), lambda i, j: (0, j))]
    args = [x, w]
    if res is None:
        body = _mm_kernel
    else:
        body = _mm_res_kernel
        in_specs.append(pl.BlockSpec((tm, tn), lambda i, j: (i, j)))
        args.append(res)
    return pl.pallas_call(
        body,
        grid=(M // tm, N // tn),
        in_specs=in_specs,
        out_specs=pl.BlockSpec((tm, tn), lambda i, j: (i, j)),
        out_shape=jax.ShapeDtypeStruct((M, N), F32),
        compiler_params=_cparams("parallel", "parallel"),
        name="matmul",
    )(*args)


def _split(x):
    hi = x.astype(BF16)
    lo = (x - hi.astype(F32)).astype(BF16)
    return hi, lo


def _dot3(a, b, dims=_NN):
    ah, al = a
    bh, bl = b
    d = lambda x, y: lax.dot_general(x, y, dims, preferred_element_type=F32)
    return d(ah, bh) + (d(ah, bl) + d(al, bh))


def _dot1(a, b, dims=_NN):
    return lax.dot_general(a.astype(BF16), b.astype(BF16), dims, preferred_element_type=F32)


def _round_robin(gens):
    outs = [None] * len(gens)
    active = list(range(len(gens)))
    while active:
        for i in list(active):
            try:
                next(gens[i])
            except StopIteration as stop:
                outs[i] = stop.value
                active.remove(i)
    return outs


def _rwkv_mix_kernel(*refs, n_chunks, n_pairs, has_vres):
    if has_vres:
        (r_ref, k_ref, v_ref, xg_ref, xw_ref, xa_ref, xv_ref, vf_ref, wup_ref, aup_ref, gup_ref, vup_ref, vec_ref,
         o_ref, s_ref, lw_s, k_s, v_s, a_s, b_s, y_s) = refs
    else:
        (r_ref, k_ref, v_ref, xg_ref, xw_ref, xa_ref, wup_ref, aup_ref, gup_ref, vec_ref,
         o_ref, s_ref, lw_s, k_s, v_s, a_s, b_s, y_s) = refs
    C = SCAN_CHUNK
    H2 = 2 * C

    @pl.when(pl.program_id(0) == 0)
    def _():
        s_ref[...] = jnp.zeros_like(s_ref)

    lane = lax.broadcasted_iota(jnp.int32, (C, LANES), 1)
    head0 = lane < RWKV_HEAD
    trow = lax.broadcasted_iota(jnp.int32, (C, LANES), 0)
    tcol = lane & (C - 1)
    lane2 = lax.broadcasted_iota(jnp.int32, (H2, LANES), 1)
    row2 = lax.broadcasted_iota(jnp.int32, (H2, LANES), 0)
    blockdiag = (row2 < C) == (lane2 < RWKV_HEAD)
    head_ones = jnp.where(blockdiag, 1.0, 0.0).astype(BF16)
    tri_r = lax.broadcasted_iota(jnp.int32, (C, C), 0)
    tri_c = lax.broadcasted_iota(jnp.int32, (C, C), 1)
    ltri = jnp.where(tri_r >= tri_c, 1.0, 0.0).astype(BF16)
    arow = lax.broadcasted_iota(jnp.int32, (H2, 2 * H2), 0)
    acol = lax.broadcasted_iota(jnp.int32, (H2, 2 * H2), 1)
    a_t = arow & (C - 1)
    a_j = acol & (C - 1)
    amask = a_t + jnp.where(arow < C, 0, 1) > a_j
    n_levels = C.bit_length() - 1
    eye2 = jnp.where(trow == tcol, 1.0, 0.0)
    dd = lambda x, y: jnp.dot(x, y, preferred_element_type=F32)
    tiles = [slice(p * LANES, (p + 1) * LANES) for p in range(n_pairs)]
    vec = lambda i, ls: vec_ref[i:i + 1, ls]

    def head_sum(x):
        hi, lo = _split(x)
        return dd(hi, head_ones) + dd(lo, head_ones)

    def level_mask(lvl):
        tb_, jb_ = trow >> lvl, tcol >> lvl
        return (tb_ - jb_) * (tb_ & 1) == 1

    def stack2(x):
        z = jnp.zeros_like(x)
        return jnp.concatenate([jnp.where(head0, x, z), jnp.where(head0, z, x)], axis=0)

    xw = jnp.tanh(xw_ref[...]).astype(BF16)
    xa = xa_ref[...].astype(BF16)
    if has_vres:
        xv = xv_ref[...].astype(BF16)
    for ls in tiles:
        z = vec(_V_W0, ls) + dd(xw, wup_ref[:, ls])
        softplus_neg = jnp.maximum(-z, 0.0) + jnp.log1p(jnp.exp(-jnp.abs(z)))
        lw_s[:, ls] = -jnp.exp(-softplus_neg - 0.5)
        a = jax.nn.sigmoid(vec(_V_A0, ls) + dd(xa, aup_ref[:, ls]))
        k = k_ref[:, ls]
        kk = k * vec(_V_KK, ls)
        kk = kk / jnp.maximum(jnp.sqrt(head_sum(kk * kk)), 1e-12)
        v = v_ref[:, ls]
        if has_vres:
            mix = jax.nn.sigmoid(vec(_V_V0, ls) + dd(xv, vup_ref[:, ls]))
            v = v + (vf_ref[:, ls] - v) * mix
        k_s[:, ls] = k * (1.0 + (a - 1.0) * vec(_V_KA, ls))
        v_s[:, ls] = v
        a_s[:, ls] = -kk
        b_s[:, ls] = kk * a

    def pair_chunk(r, lw, k, v, a, b, s):
        l1 = lw.astype(BF16)
        rem = lw - l1.astype(F32)
        l2 = rem.astype(BF16)
        l3 = (rem - l2.astype(F32)).astype(BF16)
        cum = dd(ltri, l1) + (dd(ltri, l2) + dd(ltri, l3))
        yield
        p_inc = jnp.exp(cum)
        p_exc = jnp.exp(cum - lw)
        p_inv = jnp.exp(-cum)
        at = a * p_exc
        rt = r * p_inc
        bt = b * p_inv
        kt = k * p_inv
        p_tot = p_inc[C - 1:C, :]

        x_ar = _split(jnp.concatenate([at, rt], axis=0))
        w_bk = jnp.concatenate([stack2(bt), stack2(kt)], axis=0).astype(BF16)
        amat = jnp.where(amask, lax.dot_general(x_ar[0], w_bk, _NT, preferred_element_type=F32), 0.0)
        xs = _dot3(x_ar, _split(s), _NT)
        yield

        n_ab = amat[:C, :H2]
        a_ak = amat[:C, H2:]
        a_r = amat[C:, :]
        v2 = stack2(v)
        rhs = xs[:C] + _dot1(a_ak, v2)
        yield

        tinv = eye2 + jnp.where(level_mask(0), n_ab, 0.0)
        for lvl in range(1, n_levels):
            nt = _dot1(jnp.where(level_mask(lvl), n_ab, 0.0), stack2(tinv))
            yield
            tinv = tinv + _dot1(tinv, stack2(nt))
            yield
        u = _dot1(tinv, stack2(rhs))
        yield

        uv = jnp.concatenate([stack2(u), v2], axis=0)
        y = xs[C:] + _dot1(a_r, uv)
        yield

        uv_t = jnp.concatenate([u, v], axis=0).T
        bk = jnp.concatenate([bt, kt], axis=0)
        ds = _dot3(_split(uv_t), _split(bk))
        return y, (s + jnp.where(blockdiag, ds, 0.0)) * p_tot

    def chunk(ci, carry):
        sl = pl.ds(pl.multiple_of(ci * C, C), C)
        ins = [tuple(ref[sl, ls] for ref in (r_ref, lw_s, k_s, v_s, a_s, b_s)) + (s_ref[p],)
               for p, ls in enumerate(tiles)]
        outs = _round_robin([pair_chunk(*args) for args in ins])
        for p, ls in enumerate(tiles):
            y_s[sl, ls] = outs[p][0]
            s_ref[p] = outs[p][1]
        return carry

    lax.fori_loop(0, n_chunks, chunk, 0)

    xg = jax.nn.sigmoid(xg_ref[...]).astype(BF16)
    inv_n = 1.0 / RWKV_HEAD
    for ls in tiles:
        y = y_s[:, ls]
        d = y - head_sum(y) * inv_n
        var = head_sum(d * d) * inv_n
        yn = d * lax.rsqrt(var + LNX_EPS) * vec(_V_LNG, ls) + vec(_V_LNB, ls)
        bonus = head_sum(r_ref[:, ls] * k_s[:, ls] * vec(_V_RK, ls)) * v_s[:, ls]
        o_ref[:, ls] = ((yn + bonus) * dd(xg, gup_ref[:, ls])).astype(BF16)


_V_W0, _V_A0, _V_V0, _V_KK, _V_KA, _V_RK, _V_LNG, _V_LNB = range(8)


def _rwkv_mix(y, y_first, w_up, a_up, g_up, v_up, vecs):
    T = y.shape[0]
    W = RWKV_WIDTH
    tb = min(SCAN_TBLOCK, T)
    has_vres = y_first is not None
    assert T % tb == 0 and tb % SCAN_CHUNK == 0
    wide = lambda blk: pl.BlockSpec((tb, W), lambda t: (t, blk))
    narrow = lambda off, width: pl.BlockSpec((tb, width), lambda t: (t, off // width))
    full = lambda arr: pl.BlockSpec(arr.shape, lambda t: (0, 0))
    in_specs = [wide(A_OFF_R // W), wide(A_OFF_K // W), wide(A_OFF_V // W),
                narrow(A_OFF_XG, A_XG_PAD), narrow(A_OFF_XW, LORA_DECAY), narrow(A_OFF_XA, LORA_AAA)]
    args = [y, y, y, y, y, y]
    if has_vres:
        in_specs += [narrow(A_OFF_XV, A_XV_PAD), wide(A_OFF_V // W)]
        args += [y, y_first]
    weights = [w_up, a_up, g_up] + ([v_up] if has_vres else []) + [vecs]
    in_specs += [full(w) for w in weights]
    args += weights
    return pl.pallas_call(
        functools.partial(_rwkv_mix_kernel, n_chunks=tb // SCAN_CHUNK, n_pairs=W // LANES, has_vres=has_vres),
        grid=(T // tb,),
        in_specs=in_specs,
        out_specs=pl.BlockSpec((tb, W), lambda t: (t, 0)),
        out_shape=jax.ShapeDtypeStruct((T, W), BF16),
        scratch_shapes=[pltpu.VMEM((W // LANES, LANES, LANES), F32)] + [pltpu.VMEM((tb, W), F32)] * 6,
        compiler_params=_cparams("arbitrary"),
        name="rwkv7_mix",
    )(*args)


def _mem_attn_kernel(q_ref, k_ref, v_ref, g_ref, o_ref):
    scale = MEM_HEAD_DIM ** -0.5
    for h in range(MEM_HEADS):
        hs = slice(h * MEM_HEAD_DIM, (h + 1) * MEM_HEAD_DIM)
        q = q_ref[:, hs]
        qn = q * lax.rsqrt(jnp.mean(q * q, axis=-1, keepdims=True) + NORM_EPS) * g_ref[...]
        s = lax.dot_general(qn.astype(BF16), k_ref[:, hs], _NT, preferred_element_type=F32) * scale
        m = jnp.max(s, axis=-1, keepdims=True)
        p = jnp.exp(s - m)
        l = jnp.sum(p, axis=-1, keepdims=True)
        o = jnp.dot(p.astype(BF16), v_ref[:, hs], preferred_element_type=F32)
        o_ref[:, hs] = (o / l).astype(o_ref.dtype)


def _mem_attend(y, col_block, mk, mv, q_gain, tm=512):
    T = y.shape[0]
    M = mk.shape[0]
    tm = min(tm, T)
    return pl.pallas_call(
        _mem_attn_kernel,
        grid=(T // tm,),
        in_specs=[
            pl.BlockSpec((tm, MEM_WIDTH), lambda i: (i, col_block)),
            pl.BlockSpec((M, MEM_WIDTH), lambda i: (0, 0)),
            pl.BlockSpec((M, MEM_WIDTH), lambda i: (0, 0)),
            pl.BlockSpec((1, MEM_HEAD_DIM), lambda i: (0, 0)),
        ],
        out_specs=pl.BlockSpec((tm, MEM_WIDTH), lambda i: (i, 0)),
        out_shape=jax.ShapeDtypeStruct((T, MEM_WIDTH), BF16),
        compiler_params=_cparams("parallel"),
        name="mem_attend",
    )(y, mk, mv, q_gain.reshape(1, MEM_HEAD_DIM))


def _dil_attn_kernel(sl_ref, q_ref, kp_ref, kc_ref, vp_ref, vc_ref, g_ref, o_ref, lse_ref, *, dil, heads):
    n = pl.program_id(0)
    h0 = pl.program_id(1) * heads
    B = ATT_BLOCK
    n_back = B
    scale = ATT_HEAD_DIM ** -0.5
    qi = lax.broadcasted_iota(jnp.int32, (B, 2 * B), 0)
    ki = lax.broadcasted_iota(jnp.int32, (B, 2 * B), 1)
    j = B + qi - ki
    valid = (j >= 0) & (j <= n_back) & ((ki >= B) | (n > 0))
    dist = (j * dil).astype(F32)
    for c in range(dil):
        rows = pl.ds(c, B, stride=dil) if dil > 1 else slice(None)
        for h in range(heads):
            hs = slice(h * ATT_HEAD_DIM, (h + 1) * ATT_HEAD_DIM)
            q = q_ref[rows, hs]
            qn = q * lax.rsqrt(jnp.mean(q * q, axis=-1, keepdims=True) + NORM_EPS) * g_ref[...]
            kcat = jnp.concatenate([kp_ref[rows, hs], kc_ref[rows, hs]], axis=0).astype(BF16)
            vcat = jnp.concatenate([vp_ref[rows, hs], vc_ref[rows, hs]], axis=0).astype(BF16)
            s = lax.dot_general(qn.astype(BF16), kcat, _NT, preferred_element_type=F32) * scale
            s = jnp.where(valid, s - sl_ref[h0 + h] * dist, -1e30)
            m = jnp.max(s, axis=-1, keepdims=True)
            p = jnp.exp(s - m)
            l = jnp.sum(p, axis=-1, keepdims=True)
            o = jnp.dot(p.astype(BF16), vcat, preferred_element_type=F32)
            o_ref[rows, hs] = o / l
            lse_ref[rows, hs] = jnp.broadcast_to(m + jnp.log(l), (B, ATT_HEAD_DIM))


def _dilated_group(y, gi, dil, k, v, q_gain):
    T, W = y.shape
    n_heads_total = N_DIL_GROUPS * DIL_HEADS
    slopes = jnp.asarray([2.0 ** (-8.0 * (gi * DIL_HEADS + h + 1) / n_heads_total) for h in range(DIL_HEADS)], F32)
    rows = ATT_BLOCK * dil
    cols = KV_WIDTH if dil == 1 else ATT_HEAD_DIM
    ncb = KV_WIDTH // cols
    blk = (rows, cols)
    cur = lambda n, hb: (n, hb)
    prev = lambda n, hb: (jnp.maximum(n - 1, 0), hb)
    o, lse = pl.pallas_call(
        functools.partial(_dil_attn_kernel, dil=dil, heads=cols // ATT_HEAD_DIM),
        grid=(T // rows, ncb),
        in_specs=[
            pl.BlockSpec(memory_space=pltpu.SMEM),
            pl.BlockSpec(blk, lambda n, hb: (n, gi * ncb + hb)),
            pl.BlockSpec(blk, prev), pl.BlockSpec(blk, cur),
            pl.BlockSpec(blk, prev), pl.BlockSpec(blk, cur),
            pl.BlockSpec((1, ATT_HEAD_DIM), lambda n, hb: (0, 0)),
        ],
        out_specs=[pl.BlockSpec(blk, cur), pl.BlockSpec(blk, cur)],
        out_shape=[jax.ShapeDtypeStruct((T, KV_WIDTH), F32)] * 2,
        compiler_params=_cparams("parallel", "parallel"),
        name=f"dilated_attn_d{dil}",
    )(slopes, y, k, k, v, v, q_gain.reshape(1, ATT_HEAD_DIM))
    return o, lse


def _merge_kernel(o0, l0, o1, l1, o2, l2, out_ref):
    a0, a1, a2 = l0[...], l1[...], l2[...]
    m = jnp.maximum(jnp.maximum(a0, a1), a2)
    w0, w1, w2 = jnp.exp(a0 - m), jnp.exp(a1 - m), jnp.exp(a2 - m)
    out_ref[...] = ((w0 * o0[...] + w1 * o1[...] + w2 * o2[...]) / (w0 + w1 + w2)).astype(out_ref.dtype)


def _merge_groups(parts, tm=512):
    T, W = parts[0][0].shape
    tm = min(tm, T)
    spec = pl.BlockSpec((tm, W), lambda i: (i, 0))
    flat = [t for pr in parts for t in pr]
    return pl.pallas_call(
        _merge_kernel,
        grid=(T // tm,),
        in_specs=[spec] * 6,
        out_specs=spec,
        out_shape=jax.ShapeDtypeStruct((T, W), BF16),
        compiler_params=_cparams("parallel"),
        name="dilated_merge",
    )(*flat)


def _moe_up_kernel(tok_ref, be_ref, bf_ref, bv_ref, x_hbm, wg_ref, wu_ref, o_ref, xbuf, sem, wg16, wu16):
    i = pl.program_id(0)
    n_blk = pl.num_programs(0)
    R = MOE_ROWS

    def row_copy(tok, slot, r):
        return pltpu.make_async_copy(x_hbm.at[pl.ds(tok, 1)], xbuf.at[slot, pl.ds(r, 1)], sem.at[slot])

    def start_gather(blk, slot):
        def body(r, carry):
            row_copy(tok_ref[blk * R + r], slot, r).start()
            return carry
        lax.fori_loop(0, bv_ref[blk], body, 0)

    @pl.when(i == 0)
    def _():
        xbuf[...] = jnp.zeros_like(xbuf)
        start_gather(0, 0)

    nxt = jnp.minimum(i + 1, n_blk - 1)

    @pl.when((i + 1 < n_blk) & (bv_ref[nxt] > 0))
    def _():
        start_gather(i + 1, (i + 1) & 1)

    @pl.when(bf_ref[i] > 0)
    def _():
        wg16[...] = wg_ref[...].astype(BF16)
        wu16[...] = wu_ref[...].astype(BF16)

    @pl.when(bv_ref[i] > 0)
    def _():
        slot = i & 1

        def wait_row(r, carry):
            row_copy(0, slot, r).wait()
            return carry
        lax.fori_loop(0, bv_ref[i], wait_row, 0)
        w = xbuf[slot]
        lo = lax.bitcast_convert_type(w << 16, F32)
        hi = lax.bitcast_convert_type(w & jnp.uint32(0xFFFF0000), F32)
        x = jnp.concatenate([lo, hi], axis=1).astype(BF16)
        hg = jnp.dot(x, wg16[...], preferred_element_type=F32)
        hu = jnp.dot(x, wu16[...], preferred_element_type=F32)
        o_ref[...] = (hg * jax.nn.sigmoid(hg) * hu).astype(BF16)

    @pl.when(bv_ref[i] == 0)
    def _():
        o_ref[...] = jnp.zeros_like(o_ref)


def _moe_down_kernel(be_ref, bf_ref, bv_ref, h_ref, wd_ref, g_ref, o_ref, wd16):
    i = pl.program_id(0)

    @pl.when(bf_ref[i] > 0)
    def _():
        wd16[...] = wd_ref[...].astype(BF16)

    @pl.when(bv_ref[i] > 0)
    def _():
        o_ref[...] = jnp.dot(h_ref[...], wd16[...], preferred_element_type=F32) * g_ref[...]

    @pl.when(bv_ref[i] == 0)
    def _():
        o_ref[...] = jnp.zeros_like(o_ref)


def _moe_experts(xn, slot_tok, slot_gate, blk_expert, blk_first, blk_valid, w_gate, w_up, w_down, layer):
    D = w_gate.shape[2]
    NR = slot_tok.shape[0]
    R = MOE_ROWS
    n_blk = NR // R
    F = w_gate.shape[3]
    hb = pl.pallas_call(
        _moe_up_kernel,
        grid_spec=pltpu.PrefetchScalarGridSpec(
            num_scalar_prefetch=4,
            grid=(n_blk,),
            in_specs=[pl.BlockSpec(memory_space=pl.ANY),
                      pl.BlockSpec((None, None, D, F), lambda i, tok, be, bf, bv: (layer, be[i], 0, 0)),
                      pl.BlockSpec((None, None, D, F), lambda i, tok, be, bf, bv: (layer, be[i], 0, 0))],
            out_specs=pl.BlockSpec((R, F), lambda i, tok, be, bf, bv: (i, 0)),
            scratch_shapes=[pltpu.VMEM((2, R, D // 2), jnp.uint32), pltpu.SemaphoreType.DMA((2,)),
                            pltpu.VMEM((D, F), BF16), pltpu.VMEM((D, F), BF16)],
        ),
        out_shape=jax.ShapeDtypeStruct((NR, F), BF16),
        compiler_params=_cparams("arbitrary"),
        name="moe_up",
    )(slot_tok, blk_expert, blk_first, blk_valid, xn, w_gate, w_up)
    row = lambda i, be, bf, bv: (i, 0)
    wsel = lambda i, be, bf, bv: (layer, be[i], 0, 0)
    return pl.pallas_call(
        _moe_down_kernel,
        grid_spec=pltpu.PrefetchScalarGridSpec(
            num_scalar_prefetch=3,
            grid=(n_blk,),
            in_specs=[pl.BlockSpec((R, F), row), pl.BlockSpec((None, None, F, D), wsel), pl.BlockSpec((R, 1), row)],
            out_specs=pl.BlockSpec((R, D), row),
            scratch_shapes=[pltpu.VMEM((F, D), BF16)],
        ),
        out_shape=jax.ShapeDtypeStruct((NR, D), F32),
        compiler_params=_cparams("arbitrary"),
        name="moe_down",
    )(blk_expert, blk_first, blk_valid, hb, w_down, slot_gate.reshape(NR, 1))


def _hier_moe(h, ln_g, w_rg, w_re, w_gate, w_up, w_down, layer):
    T, D = h.shape
    R = MOE_ROWS
    n_rout = N_GROUPS + N_EXPERTS
    w_r = jnp.pad(jnp.concatenate([w_rg, w_re], axis=1), ((0, 0), (0, LANES - n_rout)))
    logits, xn_packed = _rms_router(h, ln_g, w_r)
    gl = logits[:, :N_GROUPS]
    g_sel = jnp.argmax(gl, axis=-1)
    p_group = jnp.take_along_axis(jax.nn.softmax(gl, axis=-1), g_sel[:, None], axis=-1)
    el = logits[:, N_GROUPS:n_rout].reshape(T, N_GROUPS, EXPERTS_PER_GROUP)
    el = jnp.take_along_axis(el, g_sel[:, None, None], axis=1)[:, 0]
    top_v, top_i = lax.top_k(el, TOP_K)
    gate = p_group * jax.nn.softmax(top_v, axis=-1)
    expert = (g_sel[:, None] * EXPERTS_PER_GROUP + top_i).reshape(-1).astype(jnp.int32)
    n_assign = T * TOP_K
    tok = jnp.arange(n_assign, dtype=jnp.int32) // TOP_K
    order = jnp.argsort(expert)
    e_sorted = expert[order]
    counts = jnp.bincount(expert, length=N_EXPERTS)
    starts = jnp.cumsum(counts) - counts
    padded = (counts + R - 1) // R * R
    p_end = jnp.cumsum(padded)
    p_start = p_end - padded
    dest = (p_start[e_sorted] + jnp.arange(n_assign, dtype=jnp.int32) - starts[e_sorted]).astype(jnp.int32)
    n_blk = -(-n_assign // R) + N_EXPERTS
    slot_tok = jnp.zeros((n_blk * R,), jnp.int32).at[dest].set(tok[order])
    slot_gate = jnp.zeros((n_blk * R,), F32).at[dest].set(gate.reshape(-1)[order])
    blk_start = jnp.arange(n_blk, dtype=jnp.int32) * R
    blk_valid = (blk_start < p_end[-1]).astype(jnp.int32)
    last_valid = jnp.maximum(p_end[-1] // R - 1, 0)
    blk_expert = jnp.minimum(jnp.searchsorted(p_end, blk_start, side='right'), N_EXPERTS - 1).astype(jnp.int32)
    blk_expert = jnp.where(blk_valid > 0, blk_expert, blk_expert[last_valid])
    blk_first = jnp.concatenate([jnp.ones((1,), jnp.int32), (blk_expert[1:] != blk_expert[:-1]).astype(jnp.int32)])
    blk_rows = jnp.clip((p_start + counts)[blk_expert] - blk_start, 0, R).astype(jnp.int32) * blk_valid
    pos = jnp.zeros((n_assign,), jnp.int32).at[order].set(dest)

    ys = _moe_experts(xn_packed, slot_tok, slot_gate, blk_expert, blk_first, blk_rows, w_gate, w_up, w_down, layer)
    pos2 = pos.reshape(T, TOP_K)
    rows_of = lambda idx: ys.at[idx].get(mode="promise_in_bounds")
    return h + (rows_of(pos2[:, 0]) + rows_of(pos2[:, 1]))


def _mem_kv(mem, g, w_kv, k_gain):
    M = mem.shape[0]
    (mn,) = _rmsnorm(mem, g, (BF16,))
    kv = _matmul(mn, w_kv.astype(BF16))
    k = kv[:, :MEM_WIDTH].reshape(M, MEM_HEADS, MEM_HEAD_DIM)
    k = k * lax.rsqrt(jnp.mean(k * k, axis=-1, keepdims=True) + NORM_EPS) * k_gain
    return k.reshape(M, MEM_WIDTH).astype(BF16), kv[:, MEM_WIDTH:].astype(BF16)


def _pad_cols(w, n):
    return jnp.pad(w, ((0, 0), (0, n - w.shape[1])))


def _a_layer_weights(w_in, mu, w_vdown, mu_vres):
    D = w_in.shape[0]
    c = np.cumsum([RWKV_WIDTH, RWKV_WIDTH, RWKV_WIDTH, LORA_DECAY, LORA_AAA, LORA_GATE, MEM_WIDTH])
    rkv, xw, xa, xg, qm = (slice(0, c[2]), slice(c[2], c[3]), slice(c[3], c[4]), slice(c[4], c[5]), slice(c[5], c[6]))
    if w_vdown is None:
        w_vdown = jnp.zeros((D, LORA_MV), w_in.dtype)
        mu_vres = jnp.zeros((LORA_MV,), mu.dtype)
    tail = A_IN_PAD - A_OFF_XV
    w = jnp.concatenate([w_in[:, rkv], w_in[:, qm], _pad_cols(w_in[:, xg], A_XG_PAD), w_in[:, xw], w_in[:, xa],
                         _pad_cols(w_vdown, tail)], axis=1)
    m = jnp.concatenate([mu[rkv], mu[qm], jnp.pad(mu[xg], (0, A_XG_PAD - LORA_GATE)), mu[xw], mu[xa],
                         jnp.pad(mu_vres, (0, tail - LORA_MV))])
    return w.astype(BF16), m


def _rwkv_layer(h, mk, mv, mq_gain, ln1, w_in, mu, w_vdown, mu_vres, y_first, w0, w_up, a0, a_up, g_up,
                v0, v_up, k_k, k_a, r_k, lnx_g, lnx_b, w_out):
    (xn,) = _rmsnorm(h, ln1, (BF16,))
    w_pad, mu_pad = _a_layer_weights(w_in, mu, w_vdown, mu_vres)
    y = _matmul_shift(xn, w_pad, mu_pad)
    pad_rows = lambda w, n: jnp.pad(w, ((0, n - w.shape[0]), (0, 0))).astype(BF16)
    has_vres = y_first is not None
    vecs = jnp.stack([w0, a0, v0 if has_vres else jnp.zeros_like(w0), k_k, k_a, r_k.reshape(-1), lnx_g, lnx_b])
    mix_out = _rwkv_mix(y, y_first, w_up.astype(BF16), a_up.astype(BF16), pad_rows(g_up, A_XG_PAD),
                        pad_rows(v_up, A_XV_PAD) if has_vres else None, vecs)
    mem_out = _mem_attend(y, A_OFF_QM // MEM_WIDTH, mk, mv, mq_gain)
    return _matmul2_res(mix_out, mem_out, w_out.astype(BF16), h), y


def _dilated_layer(h, mk, mv, mq_gain, ln1, w_in, q_gain, ks, vs, w_out):
    (xn,) = _rmsnorm(h, ln1, (BF16,))
    y = _matmul(xn, w_in.astype(BF16))
    parts = [_dilated_group(y, gi, dil, ks, vs, q_gain) for gi, (_, dil) in enumerate(DIL_PATTERNS)]
    att = _merge_groups(parts)
    mem_out = _mem_attend(y, DIL_Q_WIDTH // MEM_WIDTH, mk, mv, mq_gain)
    return _matmul2_res(att, mem_out, w_out.astype(BF16), h)


def _shared_kv(h, g, w_kv, k_gain):
    T = h.shape[0]
    (xn,) = _rmsnorm(h, g, (BF16,))
    kv = _matmul(xn, w_kv.astype(BF16))
    k = kv[:, :KV_WIDTH].reshape(T, DIL_HEADS, ATT_HEAD_DIM)
    k = k * lax.rsqrt(jnp.mean(k * k, axis=-1, keepdims=True) + NORM_EPS) * k_gain
    return k.reshape(T, KV_WIDTH), kv[:, KV_WIDTH:]


def kernel(x, mem, a_ln1, a_w_in, a_w_vdown, a_mu, a_mu_vres, a_w0, a_w_up, a_a0, a_a_up, a_g_up, a_v0, a_v_up, a_k_k, a_k_a, a_r_k, a_lnx_g, a_lnx_b, a_w_out, b_ln1, b_w_in, b_q_norm, b_w_out, s_kv_norm, s_w_kv, s_k_norm, m_norm, m_w_kv, m_q_norm, m_k_norm, moe_ln, moe_router_group, moe_router_expert, moe_w_gate, moe_w_up, moe_w_down):
    Bsz, S, D = x.shape
    assert Bsz == 1 and S % (max(d for _, d in DIL_PATTERNS) * ATT_BLOCK) == 0
    depth = moe_ln.shape[0]
    h = x.reshape(S, D)
    mem2 = mem.reshape(mem.shape[1], D)
    y_first = None
    shared = None
    for l in range(depth):
        mk, mv = _mem_kv(mem2, m_norm[l], m_w_kv[l], m_k_norm[l])
        if l < N_A:
            i = l
            has_vres = i > 0
            h, y_i = _rwkv_layer(
                h, mk, mv, m_q_norm[l], a_ln1[i], a_w_in[i], a_mu[i],
                a_w_vdown[i - 1] if has_vres else None, a_mu_vres[i - 1] if has_vres else None,
                y_first, a_w0[i], a_w_up[i], a_a0[i], a_a_up[i], a_g_up[i],
                a_v0[i - 1] if has_vres else None, a_v_up[i - 1] if has_vres else None,
                a_k_k[i], a_k_a[i], a_r_k[i],
                a_lnx_g[i], a_lnx_b[i], a_w_out[i])
            if i == 0:
                y_first = y_i
        else:
            j = l - N_A
            if shared is None:
                shared = _shared_kv(h, s_kv_norm, s_w_kv, s_k_norm)
            h = _dilated_layer(h, mk, mv, m_q_norm[l], b_ln1[j], b_w_in[j], b_q_norm[j], shared[0], shared[1], b_w_out[j])
        h = _hier_moe(h, moe_ln[l], moe_router_group[l], moe_router_expert[l], moe_w_gate, moe_w_up, moe_w_down, l)
    return h.reshape(Bsz, S, D)
```

```python
import functools

import jax
import jax.numpy as jnp
import numpy as np
from jax import lax
from jax.experimental import pallas as pl
from jax.experimental.pallas import tpu as pltpu

F32 = jnp.float32
BF16 = jnp.bfloat16

D_MODEL = 4096
RWKV_HEAD = 64
RWKV_WIDTH = 3072
LORA_DECAY = 128
LORA_AAA = 128
LORA_MV = 96
LORA_GATE = 480
LNX_EPS = 64e-5
MEM_HEADS = 4
MEM_WIDTH = 1024
MEM_HEAD_DIM = 256
ATT_HEAD_DIM = 128
DIL_PATTERNS = ((128, 1), (512, 4), (2048, 16))
N_DIL_GROUPS = 3
DIL_HEADS = 8
DIL_Q_WIDTH = 3072
KV_WIDTH = 1024
ATT_BLOCK = 128
N_GROUPS = 4
EXPERTS_PER_GROUP = 8
N_EXPERTS = 32
TOP_K = 2
D_FF = 384
NORM_EPS = 1e-6
N_A = 2

LANES = 128
SUBLANES = 8
VMEM_LIMIT_BYTES = 56 * 1024 * 1024

A_OFF_R = 0
A_OFF_K = RWKV_WIDTH
A_OFF_V = 2 * RWKV_WIDTH
A_OFF_QM = 3 * RWKV_WIDTH
A_OFF_XG = A_OFF_QM + MEM_WIDTH
A_XG_PAD = 512
A_OFF_XW = A_OFF_XG + A_XG_PAD
A_OFF_XA = A_OFF_XW + LORA_DECAY
A_OFF_XV = A_OFF_XA + LORA_AAA
A_XV_PAD = 128
A_IN_PAD = 11264

SCAN_CHUNK = 64
SCAN_TBLOCK = 128
MOE_ROWS = 256

_NN = (((1,), (0,)), ((), ()))
_NT = (((1,), (1,)), ((), ()))


def _cparams(*sem):
    return pltpu.CompilerParams(dimension_semantics=sem, vmem_limit_bytes=VMEM_LIMIT_BYTES)


def _rms_kernel(x_ref, g_ref, *o_refs):
    x = x_ref[...]
    y = x * lax.rsqrt(jnp.mean(x * x, axis=-1, keepdims=True) + NORM_EPS) * g_ref[...]
    for o_ref in o_refs:
        o_ref[...] = y.astype(o_ref.dtype)


def _rmsnorm(x, g, dtypes, tm=256):
    T, D = x.shape
    tm = min(tm, T)
    outs = pl.pallas_call(
        _rms_kernel,
        grid=(T // tm,),
        in_specs=[pl.BlockSpec((tm, D), lambda i: (i, 0)), pl.BlockSpec((1, D), lambda i: (0, 0))],
        out_specs=[pl.BlockSpec((tm, D), lambda i: (i, 0)) for _ in dtypes],
        out_shape=[jax.ShapeDtypeStruct((T, D), dt) for dt in dtypes],
        compiler_params=_cparams("parallel"),
        name="rmsnorm",
    )(x, g.reshape(1, D))
    return outs


def _pack_halves(x16):
    bits = lax.bitcast_convert_type(x16.astype(F32), jnp.uint32)
    half = bits.shape[1] // 2
    return (bits[:, :half] >> 16) | bits[:, half:]


def _unpack_halves(words):
    lo = [lax.bitcast_convert_type(w << 16, F32) for w in words]
    hi = [lax.bitcast_convert_type(w & jnp.uint32(0xFFFF0000), F32) for w in words]
    return jnp.concatenate(lo + hi, axis=1).astype(BF16)


def _rms_router_kernel(x_ref, g_ref, w_ref, logit_ref, packed_ref):
    x = x_ref[...]
    y = x * lax.rsqrt(jnp.mean(x * x, axis=-1, keepdims=True) + NORM_EPS) * g_ref[...]
    yh = y.astype(BF16)
    yl = (y - yh.astype(F32)).astype(BF16)
    w = w_ref[...]
    wh = w.astype(BF16)
    wl = (w - wh.astype(F32)).astype(BF16)
    d = lambda a, b: jnp.dot(a, b, preferred_element_type=F32)
    logit_ref[...] = d(yh, wh) + (d(yh, wl) + d(yl, wh))
    packed = _pack_halves(yh)
    tm = packed.shape[0]
    n_tiles = packed.shape[1] // LANES
    for c in range(n_tiles):
        packed_ref[pl.ds(c, tm, stride=n_tiles), :] = packed[:, c * LANES:(c + 1) * LANES]


def _rms_router(x, g, w_r, tm=256):
    T, D = x.shape
    N = w_r.shape[1]
    n_tiles = D // 2 // LANES
    return pl.pallas_call(
        _rms_router_kernel,
        grid=(T // tm,),
        in_specs=[pl.BlockSpec((tm, D), lambda i: (i, 0)), pl.BlockSpec((1, D), lambda i: (0, 0)),
                  pl.BlockSpec((D, N), lambda i: (0, 0))],
        out_specs=[pl.BlockSpec((tm, N), lambda i: (i, 0)), pl.BlockSpec((tm * n_tiles, LANES), lambda i: (i, 0))],
        out_shape=[jax.ShapeDtypeStruct((T, N), F32), jax.ShapeDtypeStruct((T * n_tiles, LANES), jnp.uint32)],
        compiler_params=_cparams("parallel"),
        name="rms_router",
    )(x, g.reshape(1, D), w_r)


def _mm_kernel(x_ref, w_ref, o_ref):
    o_ref[...] = jnp.dot(x_ref[...], w_ref[...], preferred_element_type=F32)


def _mm_res_kernel(x_ref, w_ref, r_ref, o_ref):
    o_ref[...] = r_ref[...] + jnp.dot(x_ref[...], w_ref[...], preferred_element_type=F32)


def _mm_shift_kernel(x_ref, xp_ref, w_ref, mu_ref, o_ref):
    w = w_ref[...]
    y = jnp.dot(x_ref[...], w, preferred_element_type=F32)
    tail = jnp.dot(xp_ref[...], w, preferred_element_type=F32)
    last = jnp.where(pl.program_id(0) > 0, tail[tail.shape[0] - 1:, :], 0.0)
    row = lax.broadcasted_iota(jnp.int32, y.shape, 0)
    y_prev = jnp.where(row == 0, last, pltpu.roll(y, 1, axis=0))
    o_ref[...] = y + mu_ref[...] * (y_prev - y)


def _matmul_shift(x, w, mu, tm=1024, tn=512):
    M, K = x.shape
    N = w.shape[1]
    tm = min(tm, M)
    tp = 2 * SUBLANES
    assert M % tm == 0 and N % tn == 0 and tm % tp == 0
    return pl.pallas_call(
        _mm_shift_kernel,
        grid=(M // tm, N // tn),
        in_specs=[pl.BlockSpec((tm, K), lambda i, j: (i, 0)),
                  pl.BlockSpec((tp, K), lambda i, j: (jnp.maximum(i * (tm // tp) - 1, 0), 0)),
                  pl.BlockSpec((K, tn), lambda i, j: (0, j)),
                  pl.BlockSpec((1, tn), lambda i, j: (0, j))],
        out_specs=pl.BlockSpec((tm, tn), lambda i, j: (i, j)),
        out_shape=jax.ShapeDtypeStruct((M, N), F32),
        compiler_params=_cparams("parallel", "parallel"),
        name="matmul_shift",
    )(x, x, w, mu.reshape(1, N))


def _mm2_res_kernel(x1_ref, x2_ref, w1_ref, w2_ref, r_ref, o_ref):
    acc = jnp.dot(x1_ref[...], w1_ref[...], preferred_element_type=F32)
    o_ref[...] = r_ref[...] + (acc + jnp.dot(x2_ref[...], w2_ref[...], preferred_element_type=F32))


def _matmul2_res(x1, x2, w, res, tm=1024, tn=512):
    M, K1 = x1.shape
    K2 = x2.shape[1]
    N = w.shape[1]
    tm = min(tm, M)
    assert M % tm == 0 and N % tn == 0 and K1 % K2 == 0 and w.shape[0] == K1 + K2
    return pl.pallas_call(
        _mm2_res_kernel,
        grid=(M // tm, N // tn),
        in_specs=[pl.BlockSpec((tm, K1), lambda i, j: (i, 0)), pl.BlockSpec((tm, K2), lambda i, j: (i, 0)),
                  pl.BlockSpec((K1, tn), lambda i, j: (0, j)), pl.BlockSpec((K2, tn), lambda i, j: (K1 // K2, j)),
                  pl.BlockSpec((tm, tn), lambda i, j: (i, j))],
        out_specs=pl.BlockSpec((tm, tn), lambda i, j: (i, j)),
        out_shape=jax.ShapeDtypeStruct((M, N), F32),
        compiler_params=_cparams("parallel", "parallel"),
        name="matmul_out",
    )(x1, x2, w, w, res)


def _matmul(x, w, res=None, tm=1024, tn=512):
    M, K = x.shape
    N = w.shape[1]
    tm = min(tm, M)
    tn = min(tn, N)
    assert M % tm == 0 and N % tn == 0
    in_specs = [pl.BlockSpec((tm, K), lambda i, j: (i, 0)), pl.BlockSpec((K, tn), lambda i, j: (0, j))]
    args = [x, w]
    if res is None:
        body = _mm_kernel
    else:
        body = _mm_res_kernel
        in_specs.append(pl.BlockSpec((tm, tn), lambda i, j: (i, j)))
        args.append(res)
    return pl.pallas_call(
        body,
        grid=(M // tm, N // tn),
        in_specs=in_specs,
        out_specs=pl.BlockSpec((tm, tn), lambda i, j: (i, j)),
        out_shape=jax.ShapeDtypeStruct((M, N), F32),
        compiler_params=_cparams("parallel", "parallel"),
        name="matmul",
    )(*args)


def _split(x):
    hi = x.astype(BF16)
    lo = (x - hi.astype(F32)).astype(BF16)
    return hi, lo


def _dot3(a, b, dims=_NN):
    ah, al = a
    bh, bl = b
    d = lambda x, y: lax.dot_general(x, y, dims, preferred_element_type=F32)
    return d(ah, bh) + (d(ah, bl) + d(al, bh))


def _dot1(a, b, dims=_NN):
    return lax.dot_general(a.astype(BF16), b.astype(BF16), dims, preferred_element_type=F32)


def _round_robin(gens):
    outs = [None] * len(gens)
    active = list(range(len(gens)))
    while active:
        for i in list(active):
            try:
                next(gens[i])
            except StopIteration as stop:
                outs[i] = stop.value
                active.remove(i)
    return outs


def _rwkv_mix_kernel(*refs, n_chunks, n_pairs, has_vres):
    if has_vres:
        (r_ref, k_ref, v_ref, xg_ref, xw_ref, xa_ref, xv_ref, vf_ref, wup_ref, aup_ref, gup_ref, vup_ref, vec_ref,
         o_ref, s_ref, lw_s, k_s, v_s, a_s, b_s, y_s) = refs
    else:
        (r_ref, k_ref, v_ref, xg_ref, xw_ref, xa_ref, wup_ref, aup_ref, gup_ref, vec_ref,
         o_ref, s_ref, lw_s, k_s, v_s, a_s, b_s, y_s) = refs
    C = SCAN_CHUNK
    H2 = 2 * C

    @pl.when(pl.program_id(0) == 0)
    def _():
        s_ref[...] = jnp.zeros_like(s_ref)

    lane = lax.broadcasted_iota(jnp.int32, (C, LANES), 1)
    head0 = lane < RWKV_HEAD
    trow = lax.broadcasted_iota(jnp.int32, (C, LANES), 0)
    tcol = lane & (C - 1)
    lane2 = lax.broadcasted_iota(jnp.int32, (H2, LANES), 1)
    row2 = lax.broadcasted_iota(jnp.int32, (H2, LANES), 0)
    blockdiag = (row2 < C) == (lane2 < RWKV_HEAD)
    head_ones = jnp.where(blockdiag, 1.0, 0.0).astype(BF16)
    tri_r = lax.broadcasted_iota(jnp.int32, (C, C), 0)
    tri_c = lax.broadcasted_iota(jnp.int32, (C, C), 1)
    ltri = jnp.where(tri_r >= tri_c, 1.0, 0.0).astype(BF16)
    arow = lax.broadcasted_iota(jnp.int32, (H2, 2 * H2), 0)
    acol = lax.broadcasted_iota(jnp.int32, (H2, 2 * H2), 1)
    a_t = arow & (C - 1)
    a_j = acol & (C - 1)
    amask = a_t + jnp.where(arow < C, 0, 1) > a_j
    n_levels = C.bit_length() - 1
    eye2 = jnp.where(trow == tcol, 1.0, 0.0)
    dd = lambda x, y: jnp.dot(x, y, preferred_element_type=F32)
    tiles = [slice(p * LANES, (p + 1) * LANES) for p in range(n_pairs)]
    vec = lambda i, ls: vec_ref[i:i + 1, ls]

    def head_sum(x):
        hi, lo = _split(x)
        return dd(hi, head_ones) + dd(lo, head_ones)

    def level_mask(lvl):
        tb_, jb_ = trow >> lvl, tcol >> lvl
        return (tb_ - jb_) * (tb_ & 1) == 1

    def stack2(x):
        z = jnp.zeros_like(x)
        return jnp.concatenate([jnp.where(head0, x, z), jnp.where(head0, z, x)], axis=0)

    xw = jnp.tanh(xw_ref[...]).astype(BF16)
    xa = xa_ref[...].astype(BF16)
    if has_vres:
        xv = xv_ref[...].astype(BF16)
    for ls in tiles:
        z = vec(_V_W0, ls) + dd(xw, wup_ref[:, ls])
        softplus_neg = jnp.maximum(-z, 0.0) + jnp.log1p(jnp.exp(-jnp.abs(z)))
        lw_s[:, ls] = -jnp.exp(-softplus_neg - 0.5)
        a = jax.nn.sigmoid(vec(_V_A0, ls) + dd(xa, aup_ref[:, ls]))
        k = k_ref[:, ls]
        kk = k * vec(_V_KK, ls)
        kk = kk / jnp.maximum(jnp.sqrt(head_sum(kk * kk)), 1e-12)
        v = v_ref[:, ls]
        if has_vres:
            mix = jax.nn.sigmoid(vec(_V_V0, ls) + dd(xv, vup_ref[:, ls]))
            v = v + (vf_ref[:, ls] - v) * mix
        k_s[:, ls] = k * (1.0 + (a - 1.0) * vec(_V_KA, ls))
        v_s[:, ls] = v
        a_s[:, ls] = -kk
        b_s[:, ls] = kk * a

    def pair_chunk(r, lw, k, v, a, b, s):
        l1 = lw.astype(BF16)
        rem = lw - l1.astype(F32)
        l2 = rem.astype(BF16)
        l3 = (rem - l2.astype(F32)).astype(BF16)
        cum = dd(ltri, l1) + (dd(ltri, l2) + dd(ltri, l3))
        yield
        p_inc = jnp.exp(cum)
        p_exc = jnp.exp(cum - lw)
        p_inv = jnp.exp(-cum)
        at = a * p_exc
        rt = r * p_inc
        bt = b * p_inv
        kt = k * p_inv
        p_tot = p_inc[C - 1:C, :]

        x_ar = _split(jnp.concatenate([at, rt], axis=0))
        w_bk = jnp.concatenate([stack2(bt), stack2(kt)], axis=0).astype(BF16)
        amat = jnp.where(amask, lax.dot_general(x_ar[0], w_bk, _NT, preferred_element_type=F32), 0.0)
        xs = _dot3(x_ar, _split(s), _NT)
        yield

        n_ab = amat[:C, :H2]
        a_ak = amat[:C, H2:]
        a_r = amat[C:, :]
        v2 = stack2(v)
        rhs = xs[:C] + _dot1(a_ak, v2)
        yield

        tinv = eye2 + jnp.where(level_mask(0), n_ab, 0.0)
        for lvl in range(1, n_levels):
            nt = _dot1(jnp.where(level_mask(lvl), n_ab, 0.0), stack2(tinv))
            yield
            tinv = tinv + _dot1(tinv, stack2(nt))
            yield
        u = _dot1(tinv, stack2(rhs))
        yield

        uv = jnp.concatenate([stack2(u), v2], axis=0)
        y = xs[C:] + _dot1(a_r, uv)
        yield

        uv_t = jnp.concatenate([u, v], axis=0).T
        bk = jnp.concatenate([bt, kt], axis=0)
        ds = _dot3(_split(uv_t), _split(bk))
        return y, (s + jnp.where(blockdiag, ds, 0.0)) * p_tot

    def chunk(ci, carry):
        sl = pl.ds(pl.multiple_of(ci * C, C), C)
        ins = [tuple(ref[sl, ls] for ref in (r_ref, lw_s, k_s, v_s, a_s, b_s)) + (s_ref[p],)
               for p, ls in enumerate(tiles)]
        outs = _round_robin([pair_chunk(*args) for args in ins])
        for p, ls in enumerate(tiles):
            y_s[sl, ls] = outs[p][0]
            s_ref[p] = outs[p][1]
        return carry

    lax.fori_loop(0, n_chunks, chunk, 0)

    xg = jax.nn.sigmoid(xg_ref[...]).astype(BF16)
    inv_n = 1.0 / RWKV_HEAD
    for ls in tiles:
        y = y_s[:, ls]
        d = y - head_sum(y) * inv_n
        var = head_sum(d * d) * inv_n
        yn = d * lax.rsqrt(var + LNX_EPS) * vec(_V_LNG, ls) + vec(_V_LNB, ls)
        bonus = head_sum(r_ref[:, ls] * k_s[:, ls] * vec(_V_RK, ls)) * v_s[:, ls]
        o_ref[:, ls] = ((yn + bonus) * dd(xg, gup_ref[:, ls])).astype(BF16)


_V_W0, _V_A0, _V_V0, _V_KK, _V_KA, _V_RK, _V_LNG, _V_LNB = range(8)


def _rwkv_mix(y, y_first, w_up, a_up, g_up, v_up, vecs):
    T = y.shape[0]
    W = RWKV_WIDTH
    tb = min(SCAN_TBLOCK, T)
    has_vres = y_first is not None
    assert T % tb == 0 and tb % SCAN_CHUNK == 0
    wide = lambda blk: pl.BlockSpec((tb, W), lambda t: (t, blk))
    narrow = lambda off, width: pl.BlockSpec((tb, width), lambda t: (t, off // width))
    full = lambda arr: pl.BlockSpec(arr.shape, lambda t: (0, 0))
    in_specs = [wide(A_OFF_R // W), wide(A_OFF_K // W), wide(A_OFF_V // W),
                narrow(A_OFF_XG, A_XG_PAD), narrow(A_OFF_XW, LORA_DECAY), narrow(A_OFF_XA, LORA_AAA)]
    args = [y, y, y, y, y, y]
    if has_vres:
        in_specs += [narrow(A_OFF_XV, A_XV_PAD), wide(A_OFF_V // W)]
        args += [y, y_first]
    weights = [w_up, a_up, g_up] + ([v_up] if has_vres else []) + [vecs]
    in_specs += [full(w) for w in weights]
    args += weights
    return pl.pallas_call(
        functools.partial(_rwkv_mix_kernel, n_chunks=tb // SCAN_CHUNK, n_pairs=W // LANES, has_vres=has_vres),
        grid=(T // tb,),
        in_specs=in_specs,
        out_specs=pl.BlockSpec((tb, W), lambda t: (t, 0)),
        out_shape=jax.ShapeDtypeStruct((T, W), BF16),
        scratch_shapes=[pltpu.VMEM((W // LANES, LANES, LANES), F32)] + [pltpu.VMEM((tb, W), F32)] * 6,
        compiler_params=_cparams("arbitrary"),
        name="rwkv7_mix",
    )(*args)


def _mem_attn_kernel(q_ref, k_ref, v_ref, g_ref, o_ref):
    scale = MEM_HEAD_DIM ** -0.5
    for h in range(MEM_HEADS):
        hs = slice(h * MEM_HEAD_DIM, (h + 1) * MEM_HEAD_DIM)
        q = q_ref[:, hs]
        qn = q * lax.rsqrt(jnp.mean(q * q, axis=-1, keepdims=True) + NORM_EPS) * g_ref[...]
        s = lax.dot_general(qn.astype(BF16), k_ref[:, hs], _NT, preferred_element_type=F32) * scale
        m = jnp.max(s, axis=-1, keepdims=True)
        p = jnp.exp(s - m)
        l = jnp.sum(p, axis=-1, keepdims=True)
        o = jnp.dot(p.astype(BF16), v_ref[:, hs], preferred_element_type=F32)
        o_ref[:, hs] = (o / l).astype(o_ref.dtype)


def _mem_attend(y, col_block, mk, mv, q_gain, tm=512):
    T = y.shape[0]
    M = mk.shape[0]
    tm = min(tm, T)
    return pl.pallas_call(
        _mem_attn_kernel,
        grid=(T // tm,),
        in_specs=[
            pl.BlockSpec((tm, MEM_WIDTH), lambda i: (i, col_block)),
            pl.BlockSpec((M, MEM_WIDTH), lambda i: (0, 0)),
            pl.BlockSpec((M, MEM_WIDTH), lambda i: (0, 0)),
            pl.BlockSpec((1, MEM_HEAD_DIM), lambda i: (0, 0)),
        ],
        out_specs=pl.BlockSpec((tm, MEM_WIDTH), lambda i: (i, 0)),
        out_shape=jax.ShapeDtypeStruct((T, MEM_WIDTH), BF16),
        compiler_params=_cparams("parallel"),
        name="mem_attend",
    )(y, mk, mv, q_gain.reshape(1, MEM_HEAD_DIM))


def _dil_attn_kernel(sl_ref, q_ref, kp_ref, kc_ref, vp_ref, vc_ref, g_ref, o_ref, lse_ref, *, dil, heads):
    n = pl.program_id(0)
    h0 = pl.program_id(1) * heads
    B = ATT_BLOCK
    n_back = B
    scale = ATT_HEAD_DIM ** -0.5
    qi = lax.broadcasted_iota(jnp.int32, (B, 2 * B), 0)
    ki = lax.broadcasted_iota(jnp.int32, (B, 2 * B), 1)
    j = B + qi - ki
    valid = (j >= 0) & (j <= n_back) & ((ki >= B) | (n > 0))
    dist = (j * dil).astype(F32)
    for c in range(dil):
        rows = pl.ds(c, B, stride=dil) if dil > 1 else slice(None)
        for h in range(heads):
            hs = slice(h * ATT_HEAD_DIM, (h + 1) * ATT_HEAD_DIM)
            q = q_ref[rows, hs]
            qn = q * lax.rsqrt(jnp.mean(q * q, axis=-1, keepdims=True) + NORM_EPS) * g_ref[...]
            kcat = jnp.concatenate([kp_ref[rows, hs], kc_ref[rows, hs]], axis=0).astype(BF16)
            vcat = jnp.concatenate([vp_ref[rows, hs], vc_ref[rows, hs]], axis=0).astype(BF16)
            s = lax.dot_general(qn.astype(BF16), kcat, _NT, preferred_element_type=F32) * scale
            s = jnp.where(valid, s - sl_ref[h0 + h] * dist, -1e30)
            m = jnp.max(s, axis=-1, keepdims=True)
            p = jnp.exp(s - m)
            l = jnp.sum(p, axis=-1, keepdims=True)
            o = jnp.dot(p.astype(BF16), vcat, preferred_element_type=F32)
            o_ref[rows, hs] = o / l
            lse_ref[rows, hs] = jnp.broadcast_to(m + jnp.log(l), (B, ATT_HEAD_DIM))


def _dilated_group(y, gi, dil, k, v, q_gain):
    T, W = y.shape
    n_heads_total = N_DIL_GROUPS * DIL_HEADS
    slopes = jnp.asarray([2.0 ** (-8.0 * (gi * DIL_HEADS + h + 1) / n_heads_total) for h in range(DIL_HEADS)], F32)
    rows = ATT_BLOCK * dil
    cols = KV_WIDTH if dil == 1 else ATT_HEAD_DIM
    ncb = KV_WIDTH // cols
    blk = (rows, cols)
    cur = lambda n, hb: (n, hb)
    prev = lambda n, hb: (jnp.maximum(n - 1, 0), hb)
    o, lse = pl.pallas_call(
        functools.partial(_dil_attn_kernel, dil=dil, heads=cols // ATT_HEAD_DIM),
        grid=(T // rows, ncb),
        in_specs=[
            pl.BlockSpec(memory_space=pltpu.SMEM),
            pl.BlockSpec(blk, lambda n, hb: (n, gi * ncb + hb)),
            pl.BlockSpec(blk, prev), pl.BlockSpec(blk, cur),
            pl.BlockSpec(blk, prev), pl.BlockSpec(blk, cur),
            pl.BlockSpec((1, ATT_HEAD_DIM), lambda n, hb: (0, 0)),
        ],
        out_specs=[pl.BlockSpec(blk, cur), pl.BlockSpec(blk, cur)],
        out_shape=[jax.ShapeDtypeStruct((T, KV_WIDTH), F32)] * 2,
        compiler_params=_cparams("parallel", "parallel"),
        name=f"dilated_attn_d{dil}",
    )(slopes, y, k, k, v, v, q_gain.reshape(1, ATT_HEAD_DIM))
    return o, lse


def _merge_kernel(o0, l0, o1, l1, o2, l2, out_ref):
    a0, a1, a2 = l0[...], l1[...], l2[...]
    m = jnp.maximum(jnp.maximum(a0, a1), a2)
    w0, w1, w2 = jnp.exp(a0 - m), jnp.exp(a1 - m), jnp.exp(a2 - m)
    out_ref[...] = ((w0 * o0[...] + w1 * o1[...] + w2 * o2[...]) / (w0 + w1 + w2)).astype(out_ref.dtype)


def _merge_groups(parts, tm=512):
    T, W = parts[0][0].shape
    tm = min(tm, T)
    spec = pl.BlockSpec((tm, W), lambda i: (i, 0))
    flat = [t for pr in parts for t in pr]
    return pl.pallas_call(
        _merge_kernel,
        grid=(T // tm,),
        in_specs=[spec] * 6,
        out_specs=spec,
        out_shape=jax.ShapeDtypeStruct((T, W), BF16),
        compiler_params=_cparams("parallel"),
        name="dilated_merge",
    )(*flat)


def _moe_up_kernel(tok_ref, be_ref, bf_ref, bv_ref, x_hbm, wg_ref, wu_ref, o_ref, xbuf, sem, x16, wg16, wu16):
    i = pl.program_id(0)
    n_blk = pl.num_programs(0)
    R = MOE_ROWS
    n_tiles = xbuf.shape[1] // R
    unroll = SUBLANES

    def row_copy(tok, slot, r):
        src = x_hbm.at[pl.ds(pl.multiple_of(tok * n_tiles, n_tiles), n_tiles)]
        dst = xbuf.at[slot, pl.ds(pl.multiple_of(r * n_tiles, n_tiles), n_tiles)]
        return pltpu.make_async_copy(src, dst, sem.at[slot])

    def trips(blk):
        return (bv_ref[blk] + (unroll - 1)) // unroll

    def start_gather(blk, slot):
        def body(g, carry):
            for u in range(unroll):
                r = g * unroll + u
                row_copy(tok_ref[blk * R + r], slot, r).start()
            return carry
        lax.fori_loop(0, trips(blk), body, 0)

    @pl.when(i == 0)
    def _():
        xbuf[...] = jnp.zeros_like(xbuf)
        start_gather(0, 0)

    nxt = jnp.minimum(i + 1, n_blk - 1)

    @pl.when((i + 1 < n_blk) & (bv_ref[nxt] > 0))
    def _():
        start_gather(i + 1, (i + 1) & 1)

    @pl.when(bf_ref[i] > 0)
    def _():
        wg16[...] = wg_ref[...].astype(BF16)
        wu16[...] = wu_ref[...].astype(BF16)

    def unpack(slot):
        def wait_rows(g, carry):
            for u in range(unroll):
                row_copy(0, slot, g * unroll + u).wait()
            return carry
        lax.fori_loop(0, trips(i), wait_rows, 0)
        x16[...] = _unpack_halves([xbuf[slot, pl.ds(c, R, stride=n_tiles), :] for c in range(n_tiles)])

    for slot in range(2):
        pl.when((bv_ref[i] > 0) & ((i & 1) == slot))(functools.partial(unpack, slot))

    @pl.when(bv_ref[i] > 0)
    def _():
        x = x16[...]
        hg = jnp.dot(x, wg16[...], preferred_element_type=F32)
        hu = jnp.dot(x, wu16[...], preferred_element_type=F32)
        o_ref[...] = (hg * jax.nn.sigmoid(hg) * hu).T.astype(BF16)

    @pl.when(bv_ref[i] == 0)
    def _():
        o_ref[...] = jnp.zeros_like(o_ref)


def _moe_down_kernel(be_ref, bf_ref, bv_ref, h_ref, wd_ref, g_ref, o_ref, wdt16):
    i = pl.program_id(0)

    @pl.when(bf_ref[i] > 0)
    def _():
        wdt16[...] = wd_ref[...].T.astype(BF16)

    @pl.when(bv_ref[i] > 0)
    def _():
        yt = jnp.dot(wdt16[...], h_ref[...], preferred_element_type=F32) * g_ref[...]
        o_ref[...] = _pack_halves(yt.T.astype(BF16))

    @pl.when(bv_ref[i] == 0)
    def _():
        o_ref[...] = jnp.zeros_like(o_ref)


def _moe_experts(xn, slot_tok, slot_gate, blk_expert, blk_first, blk_valid, w_gate, w_up, w_down, layer):
    D = w_gate.shape[2]
    NR = slot_tok.shape[0]
    R = MOE_ROWS
    n_blk = NR // R
    F = w_gate.shape[3]
    hb = pl.pallas_call(
        _moe_up_kernel,
        grid_spec=pltpu.PrefetchScalarGridSpec(
            num_scalar_prefetch=4,
            grid=(n_blk,),
            in_specs=[pl.BlockSpec(memory_space=pl.ANY),
                      pl.BlockSpec((None, None, D, F), lambda i, tok, be, bf, bv: (layer, be[i], 0, 0)),
                      pl.BlockSpec((None, None, D, F), lambda i, tok, be, bf, bv: (layer, be[i], 0, 0))],
            out_specs=pl.BlockSpec((F, R), lambda i, tok, be, bf, bv: (0, i)),
            scratch_shapes=[pltpu.VMEM((2, R * (D // 2 // LANES), LANES), jnp.uint32), pltpu.SemaphoreType.DMA((2,)),
                            pltpu.VMEM((R, D), BF16), pltpu.VMEM((D, F), BF16), pltpu.VMEM((D, F), BF16)],
        ),
        out_shape=jax.ShapeDtypeStruct((F, NR), BF16),
        compiler_params=_cparams("arbitrary"),
        name="moe_up",
    )(slot_tok, blk_expert, blk_first, blk_valid, xn, w_gate, w_up)
    row = lambda i, be, bf, bv: (i, 0)
    wsel = lambda i, be, bf, bv: (layer, be[i], 0, 0)
    return pl.pallas_call(
        _moe_down_kernel,
        grid_spec=pltpu.PrefetchScalarGridSpec(
            num_scalar_prefetch=3,
            grid=(n_blk,),
            in_specs=[pl.BlockSpec((F, R), lambda i, be, bf, bv: (0, i)), pl.BlockSpec((None, None, F, D), wsel),
                      pl.BlockSpec((1, R), lambda i, be, bf, bv: (0, i))],
            out_specs=pl.BlockSpec((R, D // 2), row),
            scratch_shapes=[pltpu.VMEM((D, F), BF16)],
        ),
        out_shape=jax.ShapeDtypeStruct((NR, D // 2), jnp.uint32),
        compiler_params=_cparams("arbitrary"),
        name="moe_down",
    )(blk_expert, blk_first, blk_valid, hb, w_down, slot_gate.reshape(1, NR))


def _hier_moe(h, ln_g, w_rg, w_re, w_gate, w_up, w_down, layer):
    T, D = h.shape
    R = MOE_ROWS
    n_rout = N_GROUPS + N_EXPERTS
    w_r = jnp.pad(jnp.concatenate([w_rg, w_re], axis=1), ((0, 0), (0, LANES - n_rout)))
    logits, xn_packed = _rms_router(h, ln_g, w_r)
    gl = logits[:, :N_GROUPS]
    g_sel = jnp.argmax(gl, axis=-1)
    p_group = jnp.take_along_axis(jax.nn.softmax(gl, axis=-1), g_sel[:, None], axis=-1)
    el = logits[:, N_GROUPS:n_rout].reshape(T, N_GROUPS, EXPERTS_PER_GROUP)
    el = jnp.take_along_axis(el, g_sel[:, None, None], axis=1)[:, 0]
    top_v, top_i = lax.top_k(el, TOP_K)
    gate = p_group * jax.nn.softmax(top_v, axis=-1)
    expert = (g_sel[:, None] * EXPERTS_PER_GROUP + top_i).reshape(-1).astype(jnp.int32)
    n_assign = T * TOP_K
    tok = jnp.arange(n_assign, dtype=jnp.int32) // TOP_K
    order = jnp.argsort(expert)
    e_sorted = expert[order]
    counts = jnp.bincount(expert, length=N_EXPERTS)
    starts = jnp.cumsum(counts) - counts
    padded = (counts + R - 1) // R * R
    p_end = jnp.cumsum(padded)
    p_start = p_end - padded
    dest = (p_start[e_sorted] + jnp.arange(n_assign, dtype=jnp.int32) - starts[e_sorted]).astype(jnp.int32)
    n_blk = -(-n_assign // R) + N_EXPERTS
    slot_tok = jnp.zeros((n_blk * R,), jnp.int32).at[dest].set(tok[order])
    slot_gate = jnp.zeros((n_blk * R,), F32).at[dest].set(gate.reshape(-1)[order])
    blk_start = jnp.arange(n_blk, dtype=jnp.int32) * R
    blk_valid = (blk_start < p_end[-1]).astype(jnp.int32)
    last_valid = jnp.maximum(p_end[-1] // R - 1, 0)
    blk_expert = jnp.minimum(jnp.searchsorted(p_end, blk_start, side='right'), N_EXPERTS - 1).astype(jnp.int32)
    blk_expert = jnp.where(blk_valid > 0, blk_expert, blk_expert[last_valid])
    blk_first = jnp.concatenate([jnp.ones((1,), jnp.int32), (blk_expert[1:] != blk_expert[:-1]).astype(jnp.int32)])
    blk_rows = jnp.clip((p_start + counts)[blk_expert] - blk_start, 0, R).astype(jnp.int32) * blk_valid
    pos = jnp.zeros((n_assign,), jnp.int32).at[order].set(dest)

    ys = _moe_experts(xn_packed, slot_tok, slot_gate, blk_expert, blk_first, blk_rows, w_gate, w_up, w_down, layer)
    pos2 = pos.reshape(T, TOP_K)
    rows_of = lambda idx: ys.at[idx].get(mode="promise_in_bounds")
    unpack = lambda w: jnp.concatenate([lax.bitcast_convert_type(w << 16, F32),
                                        lax.bitcast_convert_type(w & jnp.uint32(0xFFFF0000), F32)], axis=1)
    return h + (unpack(rows_of(pos2[:, 0])) + unpack(rows_of(pos2[:, 1])))


def _mem_kv(mem, g, w_kv, k_gain):
    M = mem.shape[0]
    (mn,) = _rmsnorm(mem, g, (BF16,))
    kv = _matmul(mn, w_kv.astype(BF16))
    k = kv[:, :MEM_WIDTH].reshape(M, MEM_HEADS, MEM_HEAD_DIM)
    k = k * lax.rsqrt(jnp.mean(k * k, axis=-1, keepdims=True) + NORM_EPS) * k_gain
    return k.reshape(M, MEM_WIDTH).astype(BF16), kv[:, MEM_WIDTH:].astype(BF16)


def _pad_cols(w, n):
    return jnp.pad(w, ((0, 0), (0, n - w.shape[1])))


def _a_layer_weights(w_in, mu, w_vdown, mu_vres):
    D = w_in.shape[0]
    c = np.cumsum([RWKV_WIDTH, RWKV_WIDTH, RWKV_WIDTH, LORA_DECAY, LORA_AAA, LORA_GATE, MEM_WIDTH])
    rkv, xw, xa, xg, qm = (slice(0, c[2]), slice(c[2], c[3]), slice(c[3], c[4]), slice(c[4], c[5]), slice(c[5], c[6]))
    if w_vdown is None:
        w_vdown = jnp.zeros((D, LORA_MV), w_in.dtype)
        mu_vres = jnp.zeros((LORA_MV,), mu.dtype)
    tail = A_IN_PAD - A_OFF_XV
    w = jnp.concatenate([w_in[:, rkv], w_in[:, qm], _pad_cols(w_in[:, xg], A_XG_PAD), w_in[:, xw], w_in[:, xa],
                         _pad_cols(w_vdown, tail)], axis=1)
    m = jnp.concatenate([mu[rkv], mu[qm], jnp.pad(mu[xg], (0, A_XG_PAD - LORA_GATE)), mu[xw], mu[xa],
                         jnp.pad(mu_vres, (0, tail - LORA_MV))])
    return w.astype(BF16), m


def _rwkv_layer(h, mk, mv, mq_gain, ln1, w_in, mu, w_vdown, mu_vres, y_first, w0, w_up, a0, a_up, g_up,
                v0, v_up, k_k, k_a, r_k, lnx_g, lnx_b, w_out):
    (xn,) = _rmsnorm(h, ln1, (BF16,))
    w_pad, mu_pad = _a_layer_weights(w_in, mu, w_vdown, mu_vres)
    y = _matmul_shift(xn, w_pad, mu_pad)
    pad_rows = lambda w, n: jnp.pad(w, ((0, n - w.shape[0]), (0, 0))).astype(BF16)
    has_vres = y_first is not None
    vecs = jnp.stack([w0, a0, v0 if has_vres else jnp.zeros_like(w0), k_k, k_a, r_k.reshape(-1), lnx_g, lnx_b])
    mix_out = _rwkv_mix(y, y_first, w_up.astype(BF16), a_up.astype(BF16), pad_rows(g_up, A_XG_PAD),
                        pad_rows(v_up, A_XV_PAD) if has_vres else None, vecs)
    mem_out = _mem_attend(y, A_OFF_QM // MEM_WIDTH, mk, mv, mq_gain)
    return _matmul2_res(mix_out, mem_out, w_out.astype(BF16), h), y


def _dilated_layer(h, mk, mv, mq_gain, ln1, w_in, q_gain, ks, vs, w_out):
    (xn,) = _rmsnorm(h, ln1, (BF16,))
    y = _matmul(xn, w_in.astype(BF16))
    parts = [_dilated_group(y, gi, dil, ks, vs, q_gain) for gi, (_, dil) in enumerate(DIL_PATTERNS)]
    att = _merge_groups(parts)
    mem_out = _mem_attend(y, DIL_Q_WIDTH // MEM_WIDTH, mk, mv, mq_gain)
    return _matmul2_res(att, mem_out, w_out.astype(BF16), h)


def _shared_kv(h, g, w_kv, k_gain):
    T = h.shape[0]
    (xn,) = _rmsnorm(h, g, (BF16,))
    kv = _matmul(xn, w_kv.astype(BF16))
    k = kv[:, :KV_WIDTH].reshape(T, DIL_HEADS, ATT_HEAD_DIM)
    k = k * lax.rsqrt(jnp.mean(k * k, axis=-1, keepdims=True) + NORM_EPS) * k_gain
    return k.reshape(T, KV_WIDTH), kv[:, KV_WIDTH:]


def kernel(x, mem, a_ln1, a_w_in, a_w_vdown, a_mu, a_mu_vres, a_w0, a_w_up, a_a0, a_a_up, a_g_up, a_v0, a_v_up, a_k_k, a_k_a, a_r_k, a_lnx_g, a_lnx_b, a_w_out, b_ln1, b_w_in, b_q_norm, b_w_out, s_kv_norm, s_w_kv, s_k_norm, m_norm, m_w_kv, m_q_norm, m_k_norm, moe_ln, moe_router_group, moe_router_expert, moe_w_gate, moe_w_up, moe_w_down):
    Bsz, S, D = x.shape
    assert Bsz == 1 and S % (max(d for _, d in DIL_PATTERNS) * ATT_BLOCK) == 0
    depth = moe_ln.shape[0]
    h = x.reshape(S, D)
    mem2 = mem.reshape(mem.shape[1], D)
    y_first = None
    shared = None
    for l in range(depth):
        mk, mv = _mem_kv(mem2, m_norm[l], m_w_kv[l], m_k_norm[l])
        if l < N_A:
            i = l
            has_vres = i > 0
            h, y_i = _rwkv_layer(
                h, mk, mv, m_q_norm[l], a_ln1[i], a_w_in[i], a_mu[i],
                a_w_vdown[i - 1] if has_vres else None, a_mu_vres[i - 1] if has_vres else None,
                y_first, a_w0[i], a_w_up[i], a_a0[i], a_a_up[i], a_g_up[i],
                a_v0[i - 1] if has_vres else None, a_v_up[i - 1] if has_vres else None,
                a_k_k[i], a_k_a[i], a_r_k[i],
                a_lnx_g[i], a_lnx_b[i], a_w_out[i])
            if i == 0:
                y_first = y_i
        else:
            j = l - N_A
            if shared is None:
                shared = _shared_kv(h, s_kv_norm, s_w_kv, s_k_norm)
            h = _dilated_layer(h, mk, mv, m_q_norm[l], b_ln1[j], b_w_in[j], b_q_norm[j], shared[0], shared[1], b_w_out[j])
        h = _hier_moe(h, moe_ln[l], moe_router_group[l], moe_router_expert[l], moe_w_gate, moe_w_up, moe_w_down, l)
    return h.reshape(Bsz, S, D)
```

```python
import functools

import jax
import jax.numpy as jnp
import numpy as np
from jax import lax
from jax.experimental import pallas as pl
from jax.experimental.pallas import tpu as pltpu

F32 = jnp.float32
BF16 = jnp.bfloat16

D_MODEL = 4096
RWKV_HEAD = 64
RWKV_WIDTH = 3072
LORA_DECAY = 128
LORA_AAA = 128
LORA_MV = 96
LORA_GATE = 480
LNX_EPS = 64e-5
MEM_HEADS = 4
MEM_WIDTH = 1024
MEM_HEAD_DIM = 256
ATT_HEAD_DIM = 128
DIL_PATTERNS = ((128, 1), (512, 4), (2048, 16))
N_DIL_GROUPS = 3
DIL_HEADS = 8
DIL_Q_WIDTH = 3072
KV_WIDTH = 1024
ATT_BLOCK = 128
N_GROUPS = 4
EXPERTS_PER_GROUP = 8
N_EXPERTS = 32
TOP_K = 2
D_FF = 384
NORM_EPS = 1e-6
N_A = 2

LANES = 128
SUBLANES = 8
VMEM_LIMIT_BYTES = 56 * 1024 * 1024

A_OFF_R = 0
A_OFF_K = RWKV_WIDTH
A_OFF_V = 2 * RWKV_WIDTH
A_OFF_QM = 3 * RWKV_WIDTH
A_OFF_XG = A_OFF_QM + MEM_WIDTH
A_XG_PAD = 512
A_OFF_XW = A_OFF_XG + A_XG_PAD
A_OFF_XA = A_OFF_XW + LORA_DECAY
A_OFF_XV = A_OFF_XA + LORA_AAA
A_XV_PAD = 128
A_IN_PAD = 11264

SCAN_CHUNK = 64
SCAN_TBLOCK = 128
MOE_ROWS = 256

_NN = (((1,), (0,)), ((), ()))
_NT = (((1,), (1,)), ((), ()))


def _cparams(*sem):
    return pltpu.CompilerParams(dimension_semantics=sem, vmem_limit_bytes=VMEM_LIMIT_BYTES)


def _rms_kernel(x_ref, g_ref, *o_refs):
    x = x_ref[...]
    y = x * lax.rsqrt(jnp.mean(x * x, axis=-1, keepdims=True) + NORM_EPS) * g_ref[...]
    for o_ref in o_refs:
        o_ref[...] = y.astype(o_ref.dtype)


def _rmsnorm(x, g, dtypes, tm=256):
    T, D = x.shape
    tm = min(tm, T)
    outs = pl.pallas_call(
        _rms_kernel,
        grid=(T // tm,),
        in_specs=[pl.BlockSpec((tm, D), lambda i: (i, 0)), pl.BlockSpec((1, D), lambda i: (0, 0))],
        out_specs=[pl.BlockSpec((tm, D), lambda i: (i, 0)) for _ in dtypes],
        out_shape=[jax.ShapeDtypeStruct((T, D), dt) for dt in dtypes],
        compiler_params=_cparams("parallel"),
        name="rmsnorm",
    )(x, g.reshape(1, D))
    return outs


def _pack_halves(x16):
    bits = lax.bitcast_convert_type(x16.astype(F32), jnp.uint32)
    half = bits.shape[1] // 2
    return (bits[:, :half] >> 16) | bits[:, half:]


def _unpack_halves(words):
    lo = [lax.bitcast_convert_type(w << 16, F32) for w in words]
    hi = [lax.bitcast_convert_type(w & jnp.uint32(0xFFFF0000), F32) for w in words]
    return jnp.concatenate(lo + hi, axis=1).astype(BF16)


def _rms_router_kernel(x_ref, g_ref, w_ref, logit_ref, packed_ref):
    x = x_ref[...]
    y = x * lax.rsqrt(jnp.mean(x * x, axis=-1, keepdims=True) + NORM_EPS) * g_ref[...]
    yh = y.astype(BF16)
    yl = (y - yh.astype(F32)).astype(BF16)
    w = w_ref[...]
    wh = w.astype(BF16)
    wl = (w - wh.astype(F32)).astype(BF16)
    d = lambda a, b: jnp.dot(a, b, preferred_element_type=F32)
    logit_ref[...] = d(yh, wh) + (d(yh, wl) + d(yl, wh))
    packed = _pack_halves(yh)
    tm = packed.shape[0]
    n_tiles = packed.shape[1] // LANES
    for c in range(n_tiles):
        packed_ref[pl.ds(c, tm, stride=n_tiles), :] = packed[:, c * LANES:(c + 1) * LANES]


def _rms_router(x, g, w_r, tm=256):
    T, D = x.shape
    N = w_r.shape[1]
    n_tiles = D // 2 // LANES
    return pl.pallas_call(
        _rms_router_kernel,
        grid=(T // tm,),
        in_specs=[pl.BlockSpec((tm, D), lambda i: (i, 0)), pl.BlockSpec((1, D), lambda i: (0, 0)),
                  pl.BlockSpec((D, N), lambda i: (0, 0))],
        out_specs=[pl.BlockSpec((tm, N), lambda i: (i, 0)), pl.BlockSpec((tm * n_tiles, LANES), lambda i: (i, 0))],
        out_shape=[jax.ShapeDtypeStruct((T, N), F32), jax.ShapeDtypeStruct((T * n_tiles, LANES), jnp.uint32)],
        compiler_params=_cparams("parallel"),
        name="rms_router",
    )(x, g.reshape(1, D), w_r)


def _mm_kernel(x_ref, w_ref, o_ref):
    o_ref[...] = jnp.dot(x_ref[...], w_ref[...], preferred_element_type=F32)


def _mm_res_kernel(x_ref, w_ref, r_ref, o_ref):
    o_ref[...] = r_ref[...] + jnp.dot(x_ref[...], w_ref[...], preferred_element_type=F32)


def _mm_shift_kernel(x_ref, xp_ref, w_ref, mu_ref, o_ref):
    w = w_ref[...]
    y = jnp.dot(x_ref[...], w, preferred_element_type=F32)
    tail = jnp.dot(xp_ref[...], w, preferred_element_type=F32)
    last = jnp.where(pl.program_id(0) > 0, tail[tail.shape[0] - 1:, :], 0.0)
    row = lax.broadcasted_iota(jnp.int32, y.shape, 0)
    y_prev = jnp.where(row == 0, last, pltpu.roll(y, 1, axis=0))
    o_ref[...] = y + mu_ref[...] * (y_prev - y)


def _matmul_shift(x, w, mu, tm=1024, tn=512):
    M, K = x.shape
    N = w.shape[1]
    tm = min(tm, M)
    tp = 2 * SUBLANES
    assert M % tm == 0 and N % tn == 0 and tm % tp == 0
    return pl.pallas_call(
        _mm_shift_kernel,
        grid=(M // tm, N // tn),
        in_specs=[pl.BlockSpec((tm, K), lambda i, j: (i, 0)),
                  pl.BlockSpec((tp, K), lambda i, j: (jnp.maximum(i * (tm // tp) - 1, 0), 0)),
                  pl.BlockSpec((K, tn), lambda i, j: (0, j)),
                  pl.BlockSpec((1, tn), lambda i, j: (0, j))],
        out_specs=pl.BlockSpec((tm, tn), lambda i, j: (i, j)),
        out_shape=jax.ShapeDtypeStruct((M, N), F32),
        compiler_params=_cparams("parallel", "parallel"),
        name="matmul_shift",
    )(x, x, w, mu.reshape(1, N))


def _mm2_res_kernel(x1_ref, x2_ref, w1_ref, w2_ref, r_ref, o_ref):
    acc = jnp.dot(x1_ref[...], w1_ref[...], preferred_element_type=F32)
    o_ref[...] = r_ref[...] + (acc + jnp.dot(x2_ref[...], w2_ref[...], preferred_element_type=F32))


def _matmul2_res(x1, x2, w, res, tm=1024, tn=512):
    M, K1 = x1.shape
    K2 = x2.shape[1]
    N = w.shape[1]
    tm = min(tm, M)
    assert M % tm == 0 and N % tn == 0 and K1 % K2 == 0 and w.shape[0] == K1 + K2
    return pl.pallas_call(
        _mm2_res_kernel,
        grid=(M // tm, N // tn),
        in_specs=[pl.BlockSpec((tm, K1), lambda i, j: (i, 0)), pl.BlockSpec((tm, K2), lambda i, j: (i, 0)),
                  pl.BlockSpec((K1, tn), lambda i, j: (0, j)), pl.BlockSpec((K2, tn), lambda i, j: (K1 // K2, j)),
                  pl.BlockSpec((tm, tn), lambda i, j: (i, j))],
        out_specs=pl.BlockSpec((tm, tn), lambda i, j: (i, j)),
        out_shape=jax.ShapeDtypeStruct((M, N), F32),
        compiler_params=_cparams("parallel", "parallel"),
        name="matmul_out",
    )(x1, x2, w, w, res)


def _matmul(x, w, res=None, tm=1024, tn=512):
    M, K = x.shape
    N = w.shape[1]
    tm = min(tm, M)
    tn = min(tn, N)
    assert M % tm == 0 and N % tn == 0
    in_specs = [pl.BlockSpec((tm, K), lambda i, j: (i, 0)), pl.BlockSpec((K, tn), lambda i, j: (0, j))]
    args = [x, w]
    if res is None:
        body = _mm_kernel
    else:
        body = _mm_res_kernel
        in_specs.append(pl.BlockSpec((tm, tn), lambda i, j: (i, j)))
        args.append(res)
    return pl.pallas_call(
        body,
        grid=(M // tm, N // tn),
        in_specs=in_specs,
        out_specs=pl.BlockSpec((tm, tn), lambda i, j: (i, j)),
        out_shape=jax.ShapeDtypeStruct((M, N), F32),
        compiler_params=_cparams("parallel", "parallel"),
        name="matmul",
    )(*args)


def _split(x):
    hi = x.astype(BF16)
    lo = (x - hi.astype(F32)).astype(BF16)
    return hi, lo


def _dot3(a, b, dims=_NN):
    ah, al = a
    bh, bl = b
    d = lambda x, y: lax.dot_general(x, y, dims, preferred_element_type=F32)
    return d(ah, bh) + (d(ah, bl) + d(al, bh))


def _dot1(a, b, dims=_NN):
    return lax.dot_general(a.astype(BF16), b.astype(BF16), dims, preferred_element_type=F32)


def _round_robin(gens):
    outs = [None] * len(gens)
    active = list(range(len(gens)))
    while active:
        for i in list(active):
            try:
                next(gens[i])
            except StopIteration as stop:
                outs[i] = stop.value
                active.remove(i)
    return outs


def _rwkv_mix_kernel(*refs, n_chunks, n_pairs, has_vres):
    if has_vres:
        (r_ref, k_ref, v_ref, xg_ref, xw_ref, xa_ref, xv_ref, vf_ref, wup_ref, aup_ref, gup_ref, vup_ref, vec_ref,
         o_ref, s_ref, lw_s, k_s, v_s, a_s, b_s, y_s) = refs
    else:
        (r_ref, k_ref, v_ref, xg_ref, xw_ref, xa_ref, wup_ref, aup_ref, gup_ref, vec_ref,
         o_ref, s_ref, lw_s, k_s, v_s, a_s, b_s, y_s) = refs
    C = SCAN_CHUNK
    H2 = 2 * C

    @pl.when(pl.program_id(0) == 0)
    def _():
        s_ref[...] = jnp.zeros_like(s_ref)

    lane = lax.broadcasted_iota(jnp.int32, (C, LANES), 1)
    head0 = lane < RWKV_HEAD
    trow = lax.broadcasted_iota(jnp.int32, (C, LANES), 0)
    tcol = lane & (C - 1)
    lane2 = lax.broadcasted_iota(jnp.int32, (H2, LANES), 1)
    row2 = lax.broadcasted_iota(jnp.int32, (H2, LANES), 0)
    blockdiag = (row2 < C) == (lane2 < RWKV_HEAD)
    head_ones = jnp.where(blockdiag, 1.0, 0.0).astype(BF16)
    tri_r = lax.broadcasted_iota(jnp.int32, (C, C), 0)
    tri_c = lax.broadcasted_iota(jnp.int32, (C, C), 1)
    ltri = jnp.where(tri_r >= tri_c, 1.0, 0.0).astype(BF16)
    arow = lax.broadcasted_iota(jnp.int32, (H2, 2 * H2), 0)
    acol = lax.broadcasted_iota(jnp.int32, (H2, 2 * H2), 1)
    a_t = arow & (C - 1)
    a_j = acol & (C - 1)
    amask = a_t + jnp.where(arow < C, 0, 1) > a_j
    n_levels = C.bit_length() - 1
    eye2 = jnp.where(trow == tcol, 1.0, 0.0)
    dd = lambda x, y: jnp.dot(x, y, preferred_element_type=F32)
    tiles = [slice(p * LANES, (p + 1) * LANES) for p in range(n_pairs)]
    vec = lambda i, ls: vec_ref[i:i + 1, ls]

    def head_sum(x):
        hi, lo = _split(x)
        return dd(hi, head_ones) + dd(lo, head_ones)

    def level_mask(lvl):
        tb_, jb_ = trow >> lvl, tcol >> lvl
        return (tb_ - jb_) * (tb_ & 1) == 1

    def stack2(x):
        z = jnp.zeros_like(x)
        return jnp.concatenate([jnp.where(head0, x, z), jnp.where(head0, z, x)], axis=0)

    xw = jnp.tanh(xw_ref[...]).astype(BF16)
    xa = xa_ref[...].astype(BF16)
    if has_vres:
        xv = xv_ref[...].astype(BF16)
    for ls in tiles:
        z = vec(_V_W0, ls) + dd(xw, wup_ref[:, ls])
        lw_s[:, ls] = jax.nn.sigmoid(z) * (-float(np.exp(-0.5)))
        a = jax.nn.sigmoid(vec(_V_A0, ls) + dd(xa, aup_ref[:, ls]))
        k = k_ref[:, ls]
        kk = k * vec(_V_KK, ls)
        kk = kk * lax.rsqrt(jnp.maximum(head_sum(kk * kk), 1e-24))
        v = v_ref[:, ls]
        if has_vres:
            mix = jax.nn.sigmoid(vec(_V_V0, ls) + dd(xv, vup_ref[:, ls]))
            v = v + (vf_ref[:, ls] - v) * mix
        k_s[:, ls] = k * (1.0 + (a - 1.0) * vec(_V_KA, ls))
        v_s[:, ls] = v
        a_s[:, ls] = -kk
        b_s[:, ls] = kk * a

    def pair_chunk(r, lw, k, v, a, b, s):
        l1 = lw.astype(BF16)
        rem = lw - l1.astype(F32)
        l2 = rem.astype(BF16)
        l3 = (rem - l2.astype(F32)).astype(BF16)
        cum = dd(ltri, l1) + (dd(ltri, l2) + dd(ltri, l3))
        yield
        p_inc = jnp.exp(cum)
        p_exc = jnp.exp(cum - lw)
        p_inv = jnp.exp(-cum)
        at = a * p_exc
        rt = r * p_inc
        bt = b * p_inv
        kt = k * p_inv
        p_tot = p_inc[C - 1:C, :]

        x_ar = _split(jnp.concatenate([at, rt], axis=0))
        w_bk = jnp.concatenate([stack2(bt), stack2(kt)], axis=0).astype(BF16)
        amat = jnp.where(amask, lax.dot_general(x_ar[0], w_bk, _NT, preferred_element_type=F32), 0.0)
        xs = _dot3(x_ar, _split(s), _NT)
        yield

        n_ab = amat[:C, :H2]
        a_ak = amat[:C, H2:]
        a_r = amat[C:, :]
        v2 = stack2(v)
        rhs = xs[:C] + _dot1(a_ak, v2)
        yield

        tinv = eye2 + jnp.where(level_mask(0), n_ab, 0.0)
        for lvl in range(1, n_levels):
            nt = _dot1(jnp.where(level_mask(lvl), n_ab, 0.0), stack2(tinv))
            yield
            tinv = tinv + _dot1(tinv, stack2(nt))
            yield
        u = _dot1(tinv, stack2(rhs))
        yield

        uv = jnp.concatenate([stack2(u), v2], axis=0)
        y = xs[C:] + _dot1(a_r, uv)
        yield

        uv_t = jnp.concatenate([u, v], axis=0).T
        bk = jnp.concatenate([bt, kt], axis=0)
        ds = _dot3(_split(uv_t), _split(bk))
        return y, (s + jnp.where(blockdiag, ds, 0.0)) * p_tot

    def chunk(ci, carry):
        sl = pl.ds(pl.multiple_of(ci * C, C), C)
        ins = [tuple(ref[sl, ls] for ref in (r_ref, lw_s, k_s, v_s, a_s, b_s)) + (s_ref[p],)
               for p, ls in enumerate(tiles)]
        outs = _round_robin([pair_chunk(*args) for args in ins])
        for p, ls in enumerate(tiles):
            y_s[sl, ls] = outs[p][0]
            s_ref[p] = outs[p][1]
        return carry

    lax.fori_loop(0, n_chunks, chunk, 0)

    xg = jax.nn.sigmoid(xg_ref[...]).astype(BF16)
    inv_n = 1.0 / RWKV_HEAD
    for ls in tiles:
        y = y_s[:, ls]
        d = y - head_sum(y) * inv_n
        var = head_sum(d * d) * inv_n
        yn = d * lax.rsqrt(var + LNX_EPS) * vec(_V_LNG, ls) + vec(_V_LNB, ls)
        bonus = head_sum(r_ref[:, ls] * k_s[:, ls] * vec(_V_RK, ls)) * v_s[:, ls]
        o_ref[:, ls] = ((yn + bonus) * dd(xg, gup_ref[:, ls])).astype(BF16)


_V_W0, _V_A0, _V_V0, _V_KK, _V_KA, _V_RK, _V_LNG, _V_LNB = range(8)


def _rwkv_mix(y, y_first, w_up, a_up, g_up, v_up, vecs):
    T = y.shape[0]
    W = RWKV_WIDTH
    tb = min(SCAN_TBLOCK, T)
    has_vres = y_first is not None
    assert T % tb == 0 and tb % SCAN_CHUNK == 0
    wide = lambda blk: pl.BlockSpec((tb, W), lambda t: (t, blk))
    narrow = lambda off, width: pl.BlockSpec((tb, width), lambda t: (t, off // width))
    full = lambda arr: pl.BlockSpec(arr.shape, lambda t: (0, 0))
    in_specs = [wide(A_OFF_R // W), wide(A_OFF_K // W), wide(A_OFF_V // W),
                narrow(A_OFF_XG, A_XG_PAD), narrow(A_OFF_XW, LORA_DECAY), narrow(A_OFF_XA, LORA_AAA)]
    args = [y, y, y, y, y, y]
    if has_vres:
        in_specs += [narrow(A_OFF_XV, A_XV_PAD), wide(A_OFF_V // W)]
        args += [y, y_first]
    weights = [w_up, a_up, g_up] + ([v_up] if has_vres else []) + [vecs]
    in_specs += [full(w) for w in weights]
    args += weights
    return pl.pallas_call(
        functools.partial(_rwkv_mix_kernel, n_chunks=tb // SCAN_CHUNK, n_pairs=W // LANES, has_vres=has_vres),
        grid=(T // tb,),
        in_specs=in_specs,
        out_specs=pl.BlockSpec((tb, W), lambda t: (t, 0)),
        out_shape=jax.ShapeDtypeStruct((T, W), BF16),
        scratch_shapes=[pltpu.VMEM((W // LANES, LANES, LANES), F32)] + [pltpu.VMEM((tb, W), F32)] * 6,
        compiler_params=_cparams("arbitrary"),
        name="rwkv7_mix",
    )(*args)


def _mem_attn_kernel(q_ref, k_ref, v_ref, g_ref, o_ref):
    scale = MEM_HEAD_DIM ** -0.5
    for h in range(MEM_HEADS):
        hs = slice(h * MEM_HEAD_DIM, (h + 1) * MEM_HEAD_DIM)
        q = q_ref[:, hs]
        qn = q * lax.rsqrt(jnp.mean(q * q, axis=-1, keepdims=True) + NORM_EPS) * g_ref[...]
        s = lax.dot_general(qn.astype(BF16), k_ref[:, hs], _NT, preferred_element_type=F32) * scale
        m = jnp.max(s, axis=-1, keepdims=True)
        p = jnp.exp(s - m)
        l = jnp.sum(p, axis=-1, keepdims=True)
        o = jnp.dot(p.astype(BF16), v_ref[:, hs], preferred_element_type=F32)
        o_ref[:, hs] = (o / l).astype(o_ref.dtype)


def _mem_attend(y, col_block, mk, mv, q_gain, tm=512):
    T = y.shape[0]
    M = mk.shape[0]
    tm = min(tm, T)
    return pl.pallas_call(
        _mem_attn_kernel,
        grid=(T // tm,),
        in_specs=[
            pl.BlockSpec((tm, MEM_WIDTH), lambda i: (i, col_block)),
            pl.BlockSpec((M, MEM_WIDTH), lambda i: (0, 0)),
            pl.BlockSpec((M, MEM_WIDTH), lambda i: (0, 0)),
            pl.BlockSpec((1, MEM_HEAD_DIM), lambda i: (0, 0)),
        ],
        out_specs=pl.BlockSpec((tm, MEM_WIDTH), lambda i: (i, 0)),
        out_shape=jax.ShapeDtypeStruct((T, MEM_WIDTH), BF16),
        compiler_params=_cparams("parallel"),
        name="mem_attend",
    )(y, mk, mv, q_gain.reshape(1, MEM_HEAD_DIM))


def _dil_attn_kernel(sl_ref, q_ref, kp_ref, kc_ref, vp_ref, vc_ref, g_ref, o_ref, lse_ref, *, dil, heads):
    n = pl.program_id(0)
    h0 = pl.program_id(1) * heads
    B = ATT_BLOCK
    n_back = B
    scale = ATT_HEAD_DIM ** -0.5
    qi = lax.broadcasted_iota(jnp.int32, (B, 2 * B), 0)
    ki = lax.broadcasted_iota(jnp.int32, (B, 2 * B), 1)
    j = B + qi - ki
    valid = (j >= 0) & (j <= n_back) & ((ki >= B) | (n > 0))
    dist = (j * dil).astype(F32)
    for c in range(dil):
        rows = pl.ds(c, B, stride=dil) if dil > 1 else slice(None)
        for h in range(heads):
            hs = slice(h * ATT_HEAD_DIM, (h + 1) * ATT_HEAD_DIM)
            q = q_ref[rows, hs]
            qn = q * lax.rsqrt(jnp.mean(q * q, axis=-1, keepdims=True) + NORM_EPS) * g_ref[...]
            kcat = jnp.concatenate([kp_ref[rows, hs], kc_ref[rows, hs]], axis=0).astype(BF16)
            vcat = jnp.concatenate([vp_ref[rows, hs], vc_ref[rows, hs]], axis=0).astype(BF16)
            s = lax.dot_general(qn.astype(BF16), kcat, _NT, preferred_element_type=F32) * scale
            s = jnp.where(valid, s - sl_ref[h0 + h] * dist, -1e30)
            m = jnp.max(s, axis=-1, keepdims=True)
            p = jnp.exp(s - m)
            l = jnp.sum(p, axis=-1, keepdims=True)
            o = jnp.dot(p.astype(BF16), vcat, preferred_element_type=F32)
            o_ref[rows, hs] = o / l
            lse_ref[rows, hs] = jnp.broadcast_to(m + jnp.log(l), (B, ATT_HEAD_DIM))


def _dilated_group(y, gi, dil, k, v, q_gain):
    T, W = y.shape
    n_heads_total = N_DIL_GROUPS * DIL_HEADS
    slopes = jnp.asarray([2.0 ** (-8.0 * (gi * DIL_HEADS + h + 1) / n_heads_total) for h in range(DIL_HEADS)], F32)
    rows = ATT_BLOCK * dil
    cols = KV_WIDTH if dil == 1 else ATT_HEAD_DIM
    ncb = KV_WIDTH // cols
    blk = (rows, cols)
    cur = lambda n, hb: (n, hb)
    prev = lambda n, hb: (jnp.maximum(n - 1, 0), hb)
    o, lse = pl.pallas_call(
        functools.partial(_dil_attn_kernel, dil=dil, heads=cols // ATT_HEAD_DIM),
        grid=(T // rows, ncb),
        in_specs=[
            pl.BlockSpec(memory_space=pltpu.SMEM),
            pl.BlockSpec(blk, lambda n, hb: (n, gi * ncb + hb)),
            pl.BlockSpec(blk, prev), pl.BlockSpec(blk, cur),
            pl.BlockSpec(blk, prev), pl.BlockSpec(blk, cur),
            pl.BlockSpec((1, ATT_HEAD_DIM), lambda n, hb: (0, 0)),
        ],
        out_specs=[pl.BlockSpec(blk, cur), pl.BlockSpec(blk, cur)],
        out_shape=[jax.ShapeDtypeStruct((T, KV_WIDTH), F32)] * 2,
        compiler_params=_cparams("parallel", "parallel"),
        name=f"dilated_attn_d{dil}",
    )(slopes, y, k, k, v, v, q_gain.reshape(1, ATT_HEAD_DIM))
    return o, lse


def _merge_kernel(o0, l0, o1, l1, o2, l2, out_ref):
    a0, a1, a2 = l0[...], l1[...], l2[...]
    m = jnp.maximum(jnp.maximum(a0, a1), a2)
    w0, w1, w2 = jnp.exp(a0 - m), jnp.exp(a1 - m), jnp.exp(a2 - m)
    out_ref[...] = ((w0 * o0[...] + w1 * o1[...] + w2 * o2[...]) / (w0 + w1 + w2)).astype(out_ref.dtype)


def _merge_groups(parts, tm=512):
    T, W = parts[0][0].shape
    tm = min(tm, T)
    spec = pl.BlockSpec((tm, W), lambda i: (i, 0))
    flat = [t for pr in parts for t in pr]
    return pl.pallas_call(
        _merge_kernel,
        grid=(T // tm,),
        in_specs=[spec] * 6,
        out_specs=spec,
        out_shape=jax.ShapeDtypeStruct((T, W), BF16),
        compiler_params=_cparams("parallel"),
        name="dilated_merge",
    )(*flat)


def _moe_up_kernel(tok_ref, be_ref, bf_ref, bv_ref, x_hbm, wg_ref, wu_ref, o_ref, xbuf, sem, x16, wg16, wu16):
    i = pl.program_id(0)
    n_blk = pl.num_programs(0)
    R = MOE_ROWS
    n_tiles = xbuf.shape[1] // R
    unroll = SUBLANES

    def row_copy(tok, slot, r):
        src = x_hbm.at[pl.ds(pl.multiple_of(tok * n_tiles, n_tiles), n_tiles)]
        dst = xbuf.at[slot, pl.ds(pl.multiple_of(r * n_tiles, n_tiles), n_tiles)]
        return pltpu.make_async_copy(src, dst, sem.at[slot])

    def trips(blk):
        return (bv_ref[blk] + (unroll - 1)) // unroll

    def start_gather(blk, slot):
        def body(g, carry):
            for u in range(unroll):
                r = g * unroll + u
                row_copy(tok_ref[blk * R + r], slot, r).start()
            return carry
        lax.fori_loop(0, trips(blk), body, 0)

    @pl.when(i == 0)
    def _():
        xbuf[...] = jnp.zeros_like(xbuf)
        start_gather(0, 0)

    nxt = jnp.minimum(i + 1, n_blk - 1)

    @pl.when((i + 1 < n_blk) & (bv_ref[nxt] > 0))
    def _():
        start_gather(i + 1, (i + 1) & 1)

    @pl.when(bf_ref[i] > 0)
    def _():
        wg16[...] = wg_ref[...].astype(BF16)
        wu16[...] = wu_ref[...].astype(BF16)

    def unpack(slot):
        def wait_rows(g, carry):
            for u in range(unroll):
                row_copy(0, slot, g * unroll + u).wait()
            return carry
        lax.fori_loop(0, trips(i), wait_rows, 0)
        x16[...] = _unpack_halves([xbuf[slot, pl.ds(c, R, stride=n_tiles), :] for c in range(n_tiles)])

    for slot in range(2):
        pl.when((bv_ref[i] > 0) & ((i & 1) == slot))(functools.partial(unpack, slot))

    @pl.when(bv_ref[i] > 0)
    def _():
        x = x16[...]
        hg = jnp.dot(x, wg16[...], preferred_element_type=F32)
        hu = jnp.dot(x, wu16[...], preferred_element_type=F32)
        o_ref[...] = (hg * jax.nn.sigmoid(hg) * hu).T.astype(BF16)

    @pl.when(bv_ref[i] == 0)
    def _():
        o_ref[...] = jnp.zeros_like(o_ref)


def _moe_down_kernel(be_ref, bf_ref, bv_ref, h_ref, wd_ref, g_ref, o_ref, wdt16):
    i = pl.program_id(0)

    @pl.when(bf_ref[i] > 0)
    def _():
        wdt16[...] = wd_ref[...].T.astype(BF16)

    @pl.when(bv_ref[i] > 0)
    def _():
        yt = jnp.dot(wdt16[...], h_ref[...], preferred_element_type=F32) * g_ref[...]
        o_ref[...] = _pack_halves(yt.T.astype(BF16))

    @pl.when(bv_ref[i] == 0)
    def _():
        o_ref[...] = jnp.zeros_like(o_ref)


def _moe_experts(xn, slot_tok, slot_gate, blk_expert, blk_first, blk_valid, w_gate, w_up, w_down, layer):
    D = w_gate.shape[2]
    NR = slot_tok.shape[0]
    R = MOE_ROWS
    n_blk = NR // R
    F = w_gate.shape[3]
    hb = pl.pallas_call(
        _moe_up_kernel,
        grid_spec=pltpu.PrefetchScalarGridSpec(
            num_scalar_prefetch=4,
            grid=(n_blk,),
            in_specs=[pl.BlockSpec(memory_space=pl.ANY),
                      pl.BlockSpec((None, None, D, F), lambda i, tok, be, bf, bv: (layer, be[i], 0, 0)),
                      pl.BlockSpec((None, None, D, F), lambda i, tok, be, bf, bv: (layer, be[i], 0, 0))],
            out_specs=pl.BlockSpec((F, R), lambda i, tok, be, bf, bv: (0, i)),
            scratch_shapes=[pltpu.VMEM((2, R * (D // 2 // LANES), LANES), jnp.uint32), pltpu.SemaphoreType.DMA((2,)),
                            pltpu.VMEM((R, D), BF16), pltpu.VMEM((D, F), BF16), pltpu.VMEM((D, F), BF16)],
        ),
        out_shape=jax.ShapeDtypeStruct((F, NR), BF16),
        compiler_params=_cparams("arbitrary"),
        name="moe_up",
    )(slot_tok, blk_expert, blk_first, blk_valid, xn, w_gate, w_up)
    row = lambda i, be, bf, bv: (i, 0)
    wsel = lambda i, be, bf, bv: (layer, be[i], 0, 0)
    return pl.pallas_call(
        _moe_down_kernel,
        grid_spec=pltpu.PrefetchScalarGridSpec(
            num_scalar_prefetch=3,
            grid=(n_blk,),
            in_specs=[pl.BlockSpec((F, R), lambda i, be, bf, bv: (0, i)), pl.BlockSpec((None, None, F, D), wsel),
                      pl.BlockSpec((1, R), lambda i, be, bf, bv: (0, i))],
            out_specs=pl.BlockSpec((R, D // 2), row),
            scratch_shapes=[pltpu.VMEM((D, F), BF16)],
        ),
        out_shape=jax.ShapeDtypeStruct((NR, D // 2), jnp.uint32),
        compiler_params=_cparams("arbitrary"),
        name="moe_down",
    )(blk_expert, blk_first, blk_valid, hb, w_down, slot_gate.reshape(1, NR))


def _moe_combine_kernel(h_ref, a_ref, b_ref, o_ref):
    half = a_ref.shape[1]
    a = a_ref[...]
    b = b_ref[...]
    top = jnp.uint32(0xFFFF0000)
    f32 = lambda w: lax.bitcast_convert_type(w, F32)
    o_ref[:, :half] = h_ref[:, :half] + (f32(a << 16) + f32(b << 16))
    o_ref[:, half:] = h_ref[:, half:] + (f32(a & top) + f32(b & top))


def _moe_combine(h, ya, yb, tm=256):
    T, D = h.shape
    packed = pl.BlockSpec((tm, D // 2), lambda i: (i, 0))
    return pl.pallas_call(
        _moe_combine_kernel,
        grid=(T // tm,),
        in_specs=[pl.BlockSpec((tm, D), lambda i: (i, 0)), packed, packed],
        out_specs=pl.BlockSpec((tm, D), lambda i: (i, 0)),
        out_shape=jax.ShapeDtypeStruct((T, D), F32),
        compiler_params=_cparams("parallel"),
        name="moe_combine",
    )(h, ya, yb)


def _hier_moe(h, ln_g, w_rg, w_re, w_gate, w_up, w_down, layer):
    T, D = h.shape
    R = MOE_ROWS
    n_rout = N_GROUPS + N_EXPERTS
    w_r = jnp.pad(jnp.concatenate([w_rg, w_re], axis=1), ((0, 0), (0, LANES - n_rout)))
    logits, xn_packed = _rms_router(h, ln_g, w_r)
    gl = logits[:, :N_GROUPS]
    g_sel = jnp.argmax(gl, axis=-1)
    p_group = jnp.take_along_axis(jax.nn.softmax(gl, axis=-1), g_sel[:, None], axis=-1)
    el = logits[:, N_GROUPS:n_rout].reshape(T, N_GROUPS, EXPERTS_PER_GROUP)
    el = jnp.take_along_axis(el, g_sel[:, None, None], axis=1)[:, 0]
    top_v, top_i = lax.top_k(el, TOP_K)
    gate = p_group * jax.nn.softmax(top_v, axis=-1)
    expert = (g_sel[:, None] * EXPERTS_PER_GROUP + top_i).reshape(-1).astype(jnp.int32)
    n_assign = T * TOP_K
    order = jnp.argsort(expert)
    rank = jnp.argsort(order)
    counts = jnp.bincount(expert, length=N_EXPERTS)
    starts = jnp.cumsum(counts) - counts
    padded = (counts + R - 1) // R * R
    p_end = jnp.cumsum(padded)
    p_start = p_end - padded
    n_blk = -(-n_assign // R) + N_EXPERTS
    blk_start = jnp.arange(n_blk, dtype=jnp.int32) * R
    blk_valid = (blk_start < p_end[-1]).astype(jnp.int32)
    last_valid = jnp.maximum(p_end[-1] // R - 1, 0)
    blk_expert = jnp.minimum(jnp.searchsorted(p_end, blk_start, side='right'), N_EXPERTS - 1).astype(jnp.int32)
    blk_expert = jnp.where(blk_valid > 0, blk_expert, blk_expert[last_valid])
    blk_first = jnp.concatenate([jnp.ones((1,), jnp.int32), (blk_expert[1:] != blk_expert[:-1]).astype(jnp.int32)])
    blk_rows = jnp.clip((p_start + counts)[blk_expert] - blk_start, 0, R).astype(jnp.int32) * blk_valid
    pos = (p_start[expert] + rank - starts[expert]).astype(jnp.int32)
    slot = jnp.arange(n_blk * R, dtype=jnp.int32)
    e_slot = blk_expert[slot // R]
    j_slot = slot - p_start[e_slot]
    live = (j_slot < counts[e_slot]) & (blk_valid[slot // R] > 0)
    a_slot = order[jnp.clip(starts[e_slot] + j_slot, 0, n_assign - 1)]
    slot_tok = jnp.where(live, a_slot // TOP_K, 0).astype(jnp.int32)
    slot_gate = jnp.where(live, gate.reshape(-1)[a_slot], 0.0)

    ys = _moe_experts(xn_packed, slot_tok, slot_gate, blk_expert, blk_first, blk_rows, w_gate, w_up, w_down, layer)
    pos2 = pos.reshape(T, TOP_K)
    rows_of = lambda idx: ys.at[idx].get(mode="promise_in_bounds")
    return _moe_combine(h, rows_of(pos2[:, 0]), rows_of(pos2[:, 1]))


def _mem_kv(mem, g, w_kv, k_gain):
    M = mem.shape[0]
    (mn,) = _rmsnorm(mem, g, (BF16,))
    kv = _matmul(mn, w_kv.astype(BF16))
    k = kv[:, :MEM_WIDTH].reshape(M, MEM_HEADS, MEM_HEAD_DIM)
    k = k * lax.rsqrt(jnp.mean(k * k, axis=-1, keepdims=True) + NORM_EPS) * k_gain
    return k.reshape(M, MEM_WIDTH).astype(BF16), kv[:, MEM_WIDTH:].astype(BF16)


def _pad_cols(w, n):
    return jnp.pad(w, ((0, 0), (0, n - w.shape[1])))


def _a_layer_weights(w_in, mu, w_vdown, mu_vres):
    D = w_in.shape[0]
    c = np.cumsum([RWKV_WIDTH, RWKV_WIDTH, RWKV_WIDTH, LORA_DECAY, LORA_AAA, LORA_GATE, MEM_WIDTH])
    rkv, xw, xa, xg, qm = (slice(0, c[2]), slice(c[2], c[3]), slice(c[3], c[4]), slice(c[4], c[5]), slice(c[5], c[6]))
    if w_vdown is None:
        w_vdown = jnp.zeros((D, LORA_MV), w_in.dtype)
        mu_vres = jnp.zeros((LORA_MV,), mu.dtype)
    tail = A_IN_PAD - A_OFF_XV
    w = jnp.concatenate([w_in[:, rkv], w_in[:, qm], _pad_cols(w_in[:, xg], A_XG_PAD), w_in[:, xw], w_in[:, xa],
                         _pad_cols(w_vdown, tail)], axis=1)
    m = jnp.concatenate([mu[rkv], mu[qm], jnp.pad(mu[xg], (0, A_XG_PAD - LORA_GATE)), mu[xw], mu[xa],
                         jnp.pad(mu_vres, (0, tail - LORA_MV))])
    return w.astype(BF16), m


def _rwkv_layer(h, mk, mv, mq_gain, ln1, w_in, mu, w_vdown, mu_vres, y_first, w0, w_up, a0, a_up, g_up,
                v0, v_up, k_k, k_a, r_k, lnx_g, lnx_b, w_out):
    (xn,) = _rmsnorm(h, ln1, (BF16,))
    w_pad, mu_pad = _a_layer_weights(w_in, mu, w_vdown, mu_vres)
    y = _matmul_shift(xn, w_pad, mu_pad)
    pad_rows = lambda w, n: jnp.pad(w, ((0, n - w.shape[0]), (0, 0))).astype(BF16)
    has_vres = y_first is not None
    vecs = jnp.stack([w0, a0, v0 if has_vres else jnp.zeros_like(w0), k_k, k_a, r_k.reshape(-1), lnx_g, lnx_b])
    mix_out = _rwkv_mix(y, y_first, w_up.astype(BF16), a_up.astype(BF16), pad_rows(g_up, A_XG_PAD),
                        pad_rows(v_up, A_XV_PAD) if has_vres else None, vecs)
    mem_out = _mem_attend(y, A_OFF_QM // MEM_WIDTH, mk, mv, mq_gain)
    return _matmul2_res(mix_out, mem_out, w_out.astype(BF16), h), y


def _dilated_layer(h, mk, mv, mq_gain, ln1, w_in, q_gain, ks, vs, w_out):
    (xn,) = _rmsnorm(h, ln1, (BF16,))
    y = _matmul(xn, w_in.astype(BF16))
    parts = [_dilated_group(y, gi, dil, ks, vs, q_gain) for gi, (_, dil) in enumerate(DIL_PATTERNS)]
    att = _merge_groups(parts)
    mem_out = _mem_attend(y, DIL_Q_WIDTH // MEM_WIDTH, mk, mv, mq_gain)
    return _matmul2_res(att, mem_out, w_out.astype(BF16), h)


def _shared_kv(h, g, w_kv, k_gain):
    T = h.shape[0]
    (xn,) = _rmsnorm(h, g, (BF16,))
    kv = _matmul(xn, w_kv.astype(BF16))
    k = kv[:, :KV_WIDTH].reshape(T, DIL_HEADS, ATT_HEAD_DIM)
    k = k * lax.rsqrt(jnp.mean(k * k, axis=-1, keepdims=True) + NORM_EPS) * k_gain
    return k.reshape(T, KV_WIDTH), kv[:, KV_WIDTH:]


def kernel(x, mem, a_ln1, a_w_in, a_w_vdown, a_mu, a_mu_vres, a_w0, a_w_up, a_a0, a_a_up, a_g_up, a_v0, a_v_up, a_k_k, a_k_a, a_r_k, a_lnx_g, a_lnx_b, a_w_out, b_ln1, b_w_in, b_q_norm, b_w_out, s_kv_norm, s_w_kv, s_k_norm, m_norm, m_w_kv, m_q_norm, m_k_norm, moe_ln, moe_router_group, moe_router_expert, moe_w_gate, moe_w_up, moe_w_down):
    Bsz, S, D = x.shape
    assert Bsz == 1 and S % (max(d for _, d in DIL_PATTERNS) * ATT_BLOCK) == 0
    depth = moe_ln.shape[0]
    h = x.reshape(S, D)
    mem2 = mem.reshape(mem.shape[1], D)
    y_first = None
    shared = None
    for l in range(depth):
        mk, mv = _mem_kv(mem2, m_norm[l], m_w_kv[l], m_k_norm[l])
        if l < N_A:
            i = l
            has_vres = i > 0
            h, y_i = _rwkv_layer(
                h, mk, mv, m_q_norm[l], a_ln1[i], a_w_in[i], a_mu[i],
                a_w_vdown[i - 1] if has_vres else None, a_mu_vres[i - 1] if has_vres else None,
                y_first, a_w0[i], a_w_up[i], a_a0[i], a_a_up[i], a_g_up[i],
                a_v0[i - 1] if has_vres else None, a_v_up[i - 1] if has_vres else None,
                a_k_k[i], a_k_a[i], a_r_k[i],
                a_lnx_g[i], a_lnx_b[i], a_w_out[i])
            if i == 0:
                y_first = y_i
        else:
            j = l - N_A
            if shared is None:
                shared = _shared_kv(h, s_kv_norm, s_w_kv, s_k_norm)
            h = _dilated_layer(h, mk, mv, m_q_norm[l], b_ln1[j], b_w_in[j], b_q_norm[j], shared[0], shared[1], b_w_out[j])
        h = _hier_moe(h, moe_ln[l], moe_router_group[l], moe_router_expert[l], moe_w_gate, moe_w_up, moe_w_down, l)
    return h.reshape(Bsz, S, D)
```

```python
import functools

import jax
import jax.numpy as jnp
import numpy as np
from jax import lax
from jax.experimental import pallas as pl
from jax.experimental.pallas import tpu as pltpu

F32 = jnp.float32
BF16 = jnp.bfloat16

D_MODEL = 4096
RWKV_HEAD = 64
RWKV_WIDTH = 3072
LORA_DECAY = 128
LORA_AAA = 128
LORA_MV = 96
LORA_GATE = 480
LNX_EPS = 64e-5
MEM_HEADS = 4
MEM_WIDTH = 1024
MEM_HEAD_DIM = 256
ATT_HEAD_DIM = 128
DIL_PATTERNS = ((128, 1), (512, 4), (2048, 16))
N_DIL_GROUPS = 3
DIL_HEADS = 8
DIL_Q_WIDTH = 3072
KV_WIDTH = 1024
ATT_BLOCK = 128
N_GROUPS = 4
EXPERTS_PER_GROUP = 8
N_EXPERTS = 32
TOP_K = 2
D_FF = 384
NORM_EPS = 1e-6
N_A = 2

LANES = 128
SUBLANES = 8
VMEM_LIMIT_BYTES = 56 * 1024 * 1024

A_OFF_R = 0
A_OFF_K = RWKV_WIDTH
A_OFF_V = 2 * RWKV_WIDTH
A_OFF_QM = 3 * RWKV_WIDTH
A_OFF_XG = A_OFF_QM + MEM_WIDTH
A_XG_PAD = 512
A_OFF_XW = A_OFF_XG + A_XG_PAD
A_OFF_XA = A_OFF_XW + LORA_DECAY
A_OFF_XV = A_OFF_XA + LORA_AAA
A_XV_PAD = 128
A_IN_PAD = 11264

SCAN_CHUNK = 64
SCAN_TBLOCK = 128
MOE_ROWS = 256

_NN = (((1,), (0,)), ((), ()))
_NT = (((1,), (1,)), ((), ()))


def _cparams(*sem):
    return pltpu.CompilerParams(dimension_semantics=sem, vmem_limit_bytes=VMEM_LIMIT_BYTES)


def _rms_kernel(x_ref, g_ref, *o_refs):
    x = x_ref[...]
    y = x * lax.rsqrt(jnp.mean(x * x, axis=-1, keepdims=True) + NORM_EPS) * g_ref[...]
    for o_ref in o_refs:
        o_ref[...] = y.astype(o_ref.dtype)


def _rmsnorm(x, g, dtypes, tm=256):
    T, D = x.shape
    tm = min(tm, T)
    outs = pl.pallas_call(
        _rms_kernel,
        grid=(T // tm,),
        in_specs=[pl.BlockSpec((tm, D), lambda i: (i, 0)), pl.BlockSpec((1, D), lambda i: (0, 0))],
        out_specs=[pl.BlockSpec((tm, D), lambda i: (i, 0)) for _ in dtypes],
        out_shape=[jax.ShapeDtypeStruct((T, D), dt) for dt in dtypes],
        compiler_params=_cparams("parallel"),
        name="rmsnorm",
    )(x, g.reshape(1, D))
    return outs


def _pack_halves(x16):
    bits = lax.bitcast_convert_type(x16.astype(F32), jnp.uint32)
    half = bits.shape[1] // 2
    return (bits[:, :half] >> 16) | bits[:, half:]


def _unpack_halves(words):
    lo = [lax.bitcast_convert_type(w << 16, F32) for w in words]
    hi = [lax.bitcast_convert_type(w & jnp.uint32(0xFFFF0000), F32) for w in words]
    return jnp.concatenate(lo + hi, axis=1).astype(BF16)


def _rms_router_kernel(x_ref, g_ref, w_ref, logit_ref, packed_ref):
    x = x_ref[...]
    y = x * lax.rsqrt(jnp.mean(x * x, axis=-1, keepdims=True) + NORM_EPS) * g_ref[...]
    yh = y.astype(BF16)
    yl = (y - yh.astype(F32)).astype(BF16)
    w = w_ref[...]
    wh = w.astype(BF16)
    wl = (w - wh.astype(F32)).astype(BF16)
    d = lambda a, b: jnp.dot(a, b, preferred_element_type=F32)
    logit_ref[...] = d(yh, wh) + (d(yh, wl) + d(yl, wh))
    packed = _pack_halves(yh)
    tm = packed.shape[0]
    n_tiles = packed.shape[1] // LANES
    for c in range(n_tiles):
        packed_ref[pl.ds(c, tm, stride=n_tiles), :] = packed[:, c * LANES:(c + 1) * LANES]


def _rms_router(x, g, w_r, tm=256):
    T, D = x.shape
    N = w_r.shape[1]
    n_tiles = D // 2 // LANES
    return pl.pallas_call(
        _rms_router_kernel,
        grid=(T // tm,),
        in_specs=[pl.BlockSpec((tm, D), lambda i: (i, 0)), pl.BlockSpec((1, D), lambda i: (0, 0)),
                  pl.BlockSpec((D, N), lambda i: (0, 0))],
        out_specs=[pl.BlockSpec((tm, N), lambda i: (i, 0)), pl.BlockSpec((tm * n_tiles, LANES), lambda i: (i, 0))],
        out_shape=[jax.ShapeDtypeStruct((T, N), F32), jax.ShapeDtypeStruct((T * n_tiles, LANES), jnp.uint32)],
        compiler_params=_cparams("parallel"),
        name="rms_router",
    )(x, g.reshape(1, D), w_r)


def _mm_kernel(x_ref, w_ref, o_ref):
    o_ref[...] = jnp.dot(x_ref[...], w_ref[...], preferred_element_type=F32)


def _mm_res_kernel(x_ref, w_ref, r_ref, o_ref):
    o_ref[...] = r_ref[...] + jnp.dot(x_ref[...], w_ref[...], preferred_element_type=F32)


def _mm_shift_kernel(x_ref, xp_ref, w_ref, mu_ref, o_ref):
    w = w_ref[...]
    y = jnp.dot(x_ref[...], w, preferred_element_type=F32)
    tail = jnp.dot(xp_ref[...], w, preferred_element_type=F32)
    last = jnp.where(pl.program_id(0) > 0, tail[tail.shape[0] - 1:, :], 0.0)
    row = lax.broadcasted_iota(jnp.int32, y.shape, 0)
    y_prev = jnp.where(row == 0, last, pltpu.roll(y, 1, axis=0))
    o_ref[...] = y + mu_ref[...] * (y_prev - y)


def _matmul_shift(x, w, mu, tm=1024, tn=512):
    M, K = x.shape
    N = w.shape[1]
    tm = min(tm, M)
    tp = 2 * SUBLANES
    assert M % tm == 0 and N % tn == 0 and tm % tp == 0
    return pl.pallas_call(
        _mm_shift_kernel,
        grid=(M // tm, N // tn),
        in_specs=[pl.BlockSpec((tm, K), lambda i, j: (i, 0)),
                  pl.BlockSpec((tp, K), lambda i, j: (jnp.maximum(i * (tm // tp) - 1, 0), 0)),
                  pl.BlockSpec((K, tn), lambda i, j: (0, j)),
                  pl.BlockSpec((1, tn), lambda i, j: (0, j))],
        out_specs=pl.BlockSpec((tm, tn), lambda i, j: (i, j)),
        out_shape=jax.ShapeDtypeStruct((M, N), F32),
        compiler_params=_cparams("parallel", "parallel"),
        name="matmul_shift",
    )(x, x, w, mu.reshape(1, N))


def _mm2_res_kernel(x1_ref, x2_ref, w1_ref, w2_ref, r_ref, o_ref):
    acc = jnp.dot(x1_ref[...], w1_ref[...], preferred_element_type=F32)
    o_ref[...] = r_ref[...] + (acc + jnp.dot(x2_ref[...], w2_ref[...], preferred_element_type=F32))


def _matmul2_res(x1, x2, w, res, tm=1024, tn=512):
    M, K1 = x1.shape
    K2 = x2.shape[1]
    N = w.shape[1]
    tm = min(tm, M)
    assert M % tm == 0 and N % tn == 0 and K1 % K2 == 0 and w.shape[0] == K1 + K2
    return pl.pallas_call(
        _mm2_res_kernel,
        grid=(M // tm, N // tn),
        in_specs=[pl.BlockSpec((tm, K1), lambda i, j: (i, 0)), pl.BlockSpec((tm, K2), lambda i, j: (i, 0)),
                  pl.BlockSpec((K1, tn), lambda i, j: (0, j)), pl.BlockSpec((K2, tn), lambda i, j: (K1 // K2, j)),
                  pl.BlockSpec((tm, tn), lambda i, j: (i, j))],
        out_specs=pl.BlockSpec((tm, tn), lambda i, j: (i, j)),
        out_shape=jax.ShapeDtypeStruct((M, N), F32),
        compiler_params=_cparams("parallel", "parallel"),
        name="matmul_out",
    )(x1, x2, w, w, res)


def _matmul(x, w, res=None, tm=1024, tn=512):
    M, K = x.shape
    N = w.shape[1]
    tm = min(tm, M)
    tn = min(tn, N)
    assert M % tm == 0 and N % tn == 0
    in_specs = [pl.BlockSpec((tm, K), lambda i, j: (i, 0)), pl.BlockSpec((K, tn), lambda i, j: (0, j))]
    args = [x, w]
    if res is None:
        body = _mm_kernel
    else:
        body = _mm_res_kernel
        in_specs.append(pl.BlockSpec((tm, tn), lambda i, j: (i, j)))
        args.append(res)
    return pl.pallas_call(
        body,
        grid=(M // tm, N // tn),
        in_specs=in_specs,
        out_specs=pl.BlockSpec((tm, tn), lambda i, j: (i, j)),
        out_shape=jax.ShapeDtypeStruct((M, N), F32),
        compiler_params=_cparams("parallel", "parallel"),
        name="matmul",
    )(*args)


def _split(x):
    hi = x.astype(BF16)
    lo = (x - hi.astype(F32)).astype(BF16)
    return hi, lo


def _dot3(a, b, dims=_NN):
    ah, al = a
    bh, bl = b
    d = lambda x, y: lax.dot_general(x, y, dims, preferred_element_type=F32)
    return d(ah, bh) + (d(ah, bl) + d(al, bh))


def _dot1(a, b, dims=_NN):
    return lax.dot_general(a.astype(BF16), b.astype(BF16), dims, preferred_element_type=F32)


def _round_robin(gens):
    outs = [None] * len(gens)
    active = list(range(len(gens)))
    while active:
        for i in list(active):
            try:
                next(gens[i])
            except StopIteration as stop:
                outs[i] = stop.value
                active.remove(i)
    return outs


def _rwkv_mix_kernel(*refs, n_chunks, n_pairs, has_vres):
    if has_vres:
        (r_ref, k_ref, v_ref, xg_ref, xw_ref, xa_ref, xv_ref, vf_ref, wup_ref, aup_ref, gup_ref, vup_ref, vec_ref,
         o_ref, s_ref, lw_s, k_s, v_s, a_s, b_s, y_s) = refs
    else:
        (r_ref, k_ref, v_ref, xg_ref, xw_ref, xa_ref, wup_ref, aup_ref, gup_ref, vec_ref,
         o_ref, s_ref, lw_s, k_s, v_s, a_s, b_s, y_s) = refs
    C = SCAN_CHUNK
    H2 = 2 * C

    @pl.when(pl.program_id(0) == 0)
    def _():
        s_ref[...] = jnp.zeros_like(s_ref)

    lane = lax.broadcasted_iota(jnp.int32, (C, LANES), 1)
    head0 = lane < RWKV_HEAD
    trow = lax.broadcasted_iota(jnp.int32, (C, LANES), 0)
    tcol = lane & (C - 1)
    lane2 = lax.broadcasted_iota(jnp.int32, (H2, LANES), 1)
    row2 = lax.broadcasted_iota(jnp.int32, (H2, LANES), 0)
    blockdiag = (row2 < C) == (lane2 < RWKV_HEAD)
    head_ones = jnp.where(blockdiag, 1.0, 0.0).astype(BF16)
    tri_r = lax.broadcasted_iota(jnp.int32, (C, C), 0)
    tri_c = lax.broadcasted_iota(jnp.int32, (C, C), 1)
    ltri = jnp.where(tri_r >= tri_c, 1.0, 0.0).astype(BF16)
    arow = lax.broadcasted_iota(jnp.int32, (H2, 2 * H2), 0)
    acol = lax.broadcasted_iota(jnp.int32, (H2, 2 * H2), 1)
    a_t = arow & (C - 1)
    a_j = acol & (C - 1)
    amask = a_t + jnp.where(arow < C, 0, 1) > a_j
    n_levels = C.bit_length() - 1
    eye2 = jnp.where(trow == tcol, 1.0, 0.0)
    dd = lambda x, y: jnp.dot(x, y, preferred_element_type=F32)
    tiles = [slice(p * LANES, (p + 1) * LANES) for p in range(n_pairs)]
    vec = lambda i, ls: vec_ref[i:i + 1, ls]

    def head_sum(x):
        hi, lo = _split(x)
        return dd(hi, head_ones) + dd(lo, head_ones)

    def level_mask(lvl):
        tb_, jb_ = trow >> lvl, tcol >> lvl
        return (tb_ - jb_) * (tb_ & 1) == 1

    def stack2(x):
        z = jnp.zeros_like(x)
        return jnp.concatenate([jnp.where(head0, x, z), jnp.where(head0, z, x)], axis=0)

    xw = jnp.tanh(xw_ref[...]).astype(BF16)
    xa = xa_ref[...].astype(BF16)
    if has_vres:
        xv = xv_ref[...].astype(BF16)
    for ls in tiles:
        z = vec(_V_W0, ls) + dd(xw, wup_ref[:, ls])
        lw_s[:, ls] = jax.nn.sigmoid(z) * (-float(np.exp(-0.5)))
        a = jax.nn.sigmoid(vec(_V_A0, ls) + dd(xa, aup_ref[:, ls]))
        k = k_ref[:, ls]
        kk = k * vec(_V_KK, ls)
        kk = kk * lax.rsqrt(jnp.maximum(head_sum(kk * kk), 1e-24))
        v = v_ref[:, ls]
        if has_vres:
            mix = jax.nn.sigmoid(vec(_V_V0, ls) + dd(xv, vup_ref[:, ls]))
            v = v + (vf_ref[:, ls] - v) * mix
        k_s[:, ls] = k * (1.0 + (a - 1.0) * vec(_V_KA, ls))
        v_s[:, ls] = v
        a_s[:, ls] = -kk
        b_s[:, ls] = kk * a

    def pair_chunk(r, lw, k, v, a, b, s):
        l1 = lw.astype(BF16)
        rem = lw - l1.astype(F32)
        l2 = rem.astype(BF16)
        l3 = (rem - l2.astype(F32)).astype(BF16)
        cum = dd(ltri, l1) + (dd(ltri, l2) + dd(ltri, l3))
        yield
        p_inc = jnp.exp(cum)
        p_exc = jnp.exp(cum - lw)
        p_inv = jnp.exp(-cum)
        at = a * p_exc
        rt = r * p_inc
        bt = b * p_inv
        kt = k * p_inv
        p_tot = p_inc[C - 1:C, :]

        x_ar = _split(jnp.concatenate([at, rt], axis=0))
        w_bk = jnp.concatenate([stack2(bt), stack2(kt)], axis=0).astype(BF16)
        amat = jnp.where(amask, lax.dot_general(x_ar[0], w_bk, _NT, preferred_element_type=F32), 0.0)
        xs = _dot3(x_ar, _split(s), _NT)
        yield

        n_ab = amat[:C, :H2]
        a_ak = amat[:C, H2:]
        a_r = amat[C:, :]
        v2 = stack2(v)
        rhs = xs[:C] + _dot1(a_ak, v2)
        yield

        tinv = eye2 + jnp.where(level_mask(0), n_ab, 0.0)
        for lvl in range(1, n_levels):
            nt = _dot1(jnp.where(level_mask(lvl), n_ab, 0.0), stack2(tinv))
            yield
            tinv = tinv + _dot1(tinv, stack2(nt))
            yield
        u = _dot1(tinv, stack2(rhs))
        yield

        uv = jnp.concatenate([stack2(u), v2], axis=0)
        y = xs[C:] + _dot1(a_r, uv)
        yield

        uv_t = jnp.concatenate([u, v], axis=0).T
        bk = jnp.concatenate([bt, kt], axis=0)
        ds = _dot3(_split(uv_t), _split(bk))
        return y, (s + jnp.where(blockdiag, ds, 0.0)) * p_tot

    def chunk(ci, carry):
        sl = pl.ds(pl.multiple_of(ci * C, C), C)
        ins = [tuple(ref[sl, ls] for ref in (r_ref, lw_s, k_s, v_s, a_s, b_s)) + (s_ref[p],)
               for p, ls in enumerate(tiles)]
        outs = _round_robin([pair_chunk(*args) for args in ins])
        for p, ls in enumerate(tiles):
            y_s[sl, ls] = outs[p][0]
            s_ref[p] = outs[p][1]
        return carry

    lax.fori_loop(0, n_chunks, chunk, 0)

    xg = jax.nn.sigmoid(xg_ref[...]).astype(BF16)
    inv_n = 1.0 / RWKV_HEAD
    for ls in tiles:
        y = y_s[:, ls]
        d = y - head_sum(y) * inv_n
        var = head_sum(d * d) * inv_n
        yn = d * lax.rsqrt(var + LNX_EPS) * vec(_V_LNG, ls) + vec(_V_LNB, ls)
        bonus = head_sum(r_ref[:, ls] * k_s[:, ls] * vec(_V_RK, ls)) * v_s[:, ls]
        o_ref[:, ls] = ((yn + bonus) * dd(xg, gup_ref[:, ls])).astype(BF16)


_V_W0, _V_A0, _V_V0, _V_KK, _V_KA, _V_RK, _V_LNG, _V_LNB = range(8)


def _rwkv_mix(y, y_first, w_up, a_up, g_up, v_up, vecs):
    T = y.shape[0]
    W = RWKV_WIDTH
    tb = min(SCAN_TBLOCK, T)
    has_vres = y_first is not None
    assert T % tb == 0 and tb % SCAN_CHUNK == 0
    wide = lambda blk: pl.BlockSpec((tb, W), lambda t: (t, blk))
    narrow = lambda off, width: pl.BlockSpec((tb, width), lambda t: (t, off // width))
    full = lambda arr: pl.BlockSpec(arr.shape, lambda t: (0, 0))
    in_specs = [wide(A_OFF_R // W), wide(A_OFF_K // W), wide(A_OFF_V // W),
                narrow(A_OFF_XG, A_XG_PAD), narrow(A_OFF_XW, LORA_DECAY), narrow(A_OFF_XA, LORA_AAA)]
    args = [y, y, y, y, y, y]
    if has_vres:
        in_specs += [narrow(A_OFF_XV, A_XV_PAD), wide(A_OFF_V // W)]
        args += [y, y_first]
    weights = [w_up, a_up, g_up] + ([v_up] if has_vres else []) + [vecs]
    in_specs += [full(w) for w in weights]
    args += weights
    return pl.pallas_call(
        functools.partial(_rwkv_mix_kernel, n_chunks=tb // SCAN_CHUNK, n_pairs=W // LANES, has_vres=has_vres),
        grid=(T // tb,),
        in_specs=in_specs,
        out_specs=pl.BlockSpec((tb, W), lambda t: (t, 0)),
        out_shape=jax.ShapeDtypeStruct((T, W), BF16),
        scratch_shapes=[pltpu.VMEM((W // LANES, LANES, LANES), F32)] + [pltpu.VMEM((tb, W), F32)] * 6,
        compiler_params=_cparams("arbitrary"),
        name="rwkv7_mix",
    )(*args)


def _mem_attn_kernel(q_ref, k_ref, v_ref, g_ref, o_ref):
    scale = MEM_HEAD_DIM ** -0.5
    for h in range(MEM_HEADS):
        hs = slice(h * MEM_HEAD_DIM, (h + 1) * MEM_HEAD_DIM)
        q = q_ref[:, hs]
        qn = q * lax.rsqrt(jnp.mean(q * q, axis=-1, keepdims=True) + NORM_EPS) * g_ref[...]
        s = lax.dot_general(qn.astype(BF16), k_ref[:, hs], _NT, preferred_element_type=F32) * scale
        m = jnp.max(s, axis=-1, keepdims=True)
        p = jnp.exp(s - m)
        l = jnp.sum(p, axis=-1, keepdims=True)
        o = jnp.dot(p.astype(BF16), v_ref[:, hs], preferred_element_type=F32)
        o_ref[:, hs] = (o / l).astype(o_ref.dtype)


def _mem_attend(y, col_block, mk, mv, q_gain, tm=512):
    T = y.shape[0]
    M = mk.shape[0]
    tm = min(tm, T)
    return pl.pallas_call(
        _mem_attn_kernel,
        grid=(T // tm,),
        in_specs=[
            pl.BlockSpec((tm, MEM_WIDTH), lambda i: (i, col_block)),
            pl.BlockSpec((M, MEM_WIDTH), lambda i: (0, 0)),
            pl.BlockSpec((M, MEM_WIDTH), lambda i: (0, 0)),
            pl.BlockSpec((1, MEM_HEAD_DIM), lambda i: (0, 0)),
        ],
        out_specs=pl.BlockSpec((tm, MEM_WIDTH), lambda i: (i, 0)),
        out_shape=jax.ShapeDtypeStruct((T, MEM_WIDTH), BF16),
        compiler_params=_cparams("parallel"),
        name="mem_attend",
    )(y, mk, mv, q_gain.reshape(1, MEM_HEAD_DIM))


def _dil_attn_kernel(sl_ref, q_ref, kp_ref, kc_ref, vp_ref, vc_ref, g_ref, o_ref, lse_ref, *, dil, heads):
    n = pl.program_id(0)
    h0 = pl.program_id(1) * heads
    B = ATT_BLOCK
    n_back = B
    scale = ATT_HEAD_DIM ** -0.5
    qi = lax.broadcasted_iota(jnp.int32, (B, 2 * B), 0)
    ki = lax.broadcasted_iota(jnp.int32, (B, 2 * B), 1)
    j = B + qi - ki
    valid = (j >= 0) & (j <= n_back) & ((ki >= B) | (n > 0))
    dist = (j * dil).astype(F32)
    for c in range(dil):
        rows = pl.ds(c, B, stride=dil) if dil > 1 else slice(None)
        for h in range(heads):
            hs = slice(h * ATT_HEAD_DIM, (h + 1) * ATT_HEAD_DIM)
            q = q_ref[rows, hs]
            qn = q * lax.rsqrt(jnp.mean(q * q, axis=-1, keepdims=True) + NORM_EPS) * g_ref[...]
            kcat = jnp.concatenate([kp_ref[rows, hs], kc_ref[rows, hs]], axis=0).astype(BF16)
            vcat = jnp.concatenate([vp_ref[rows, hs], vc_ref[rows, hs]], axis=0).astype(BF16)
            s = lax.dot_general(qn.astype(BF16), kcat, _NT, preferred_element_type=F32) * scale
            s = jnp.where(valid, s - sl_ref[h0 + h] * dist, -1e30)
            m = jnp.max(s, axis=-1, keepdims=True)
            p = jnp.exp(s - m)
            l = jnp.sum(p, axis=-1, keepdims=True)
            o = jnp.dot(p.astype(BF16), vcat, preferred_element_type=F32)
            o_ref[rows, hs] = o / l
            lse_ref[rows, hs] = jnp.broadcast_to(m + jnp.log(l), (B, ATT_HEAD_DIM))


def _dilated_group(y, gi, dil, k, v, q_gain):
    T, W = y.shape
    n_heads_total = N_DIL_GROUPS * DIL_HEADS
    slopes = jnp.asarray([2.0 ** (-8.0 * (gi * DIL_HEADS + h + 1) / n_heads_total) for h in range(DIL_HEADS)], F32)
    rows = ATT_BLOCK * dil
    cols = KV_WIDTH if dil == 1 else ATT_HEAD_DIM
    ncb = KV_WIDTH // cols
    blk = (rows, cols)
    cur = lambda n, hb: (n, hb)
    prev = lambda n, hb: (jnp.maximum(n - 1, 0), hb)
    o, lse = pl.pallas_call(
        functools.partial(_dil_attn_kernel, dil=dil, heads=cols // ATT_HEAD_DIM),
        grid=(T // rows, ncb),
        in_specs=[
            pl.BlockSpec(memory_space=pltpu.SMEM),
            pl.BlockSpec(blk, lambda n, hb: (n, gi * ncb + hb)),
            pl.BlockSpec(blk, prev), pl.BlockSpec(blk, cur),
            pl.BlockSpec(blk, prev), pl.BlockSpec(blk, cur),
            pl.BlockSpec((1, ATT_HEAD_DIM), lambda n, hb: (0, 0)),
        ],
        out_specs=[pl.BlockSpec(blk, cur), pl.BlockSpec(blk, cur)],
        out_shape=[jax.ShapeDtypeStruct((T, KV_WIDTH), F32)] * 2,
        compiler_params=_cparams("parallel", "parallel"),
        name=f"dilated_attn_d{dil}",
    )(slopes, y, k, k, v, v, q_gain.reshape(1, ATT_HEAD_DIM))
    return o, lse


def _merge_kernel(o0, l0, o1, l1, o2, l2, out_ref):
    a0, a1, a2 = l0[...], l1[...], l2[...]
    m = jnp.maximum(jnp.maximum(a0, a1), a2)
    w0, w1, w2 = jnp.exp(a0 - m), jnp.exp(a1 - m), jnp.exp(a2 - m)
    out_ref[...] = ((w0 * o0[...] + w1 * o1[...] + w2 * o2[...]) / (w0 + w1 + w2)).astype(out_ref.dtype)


def _merge_groups(parts, tm=512):
    T, W = parts[0][0].shape
    tm = min(tm, T)
    spec = pl.BlockSpec((tm, W), lambda i: (i, 0))
    flat = [t for pr in parts for t in pr]
    return pl.pallas_call(
        _merge_kernel,
        grid=(T // tm,),
        in_specs=[spec] * 6,
        out_specs=spec,
        out_shape=jax.ShapeDtypeStruct((T, W), BF16),
        compiler_params=_cparams("parallel"),
        name="dilated_merge",
    )(*flat)


def _moe_up_kernel(tok_ref, be_ref, bf_ref, bv_ref, x_hbm, wg_ref, wu_ref, o_ref, xbuf, sem, x16, wg16, wu16):
    i = pl.program_id(0)
    n_blk = pl.num_programs(0)
    R = MOE_ROWS
    n_tiles = xbuf.shape[1] // R
    unroll = SUBLANES

    def row_copy(tok, slot, r):
        src = x_hbm.at[pl.ds(pl.multiple_of(tok * n_tiles, n_tiles), n_tiles)]
        dst = xbuf.at[slot, pl.ds(pl.multiple_of(r * n_tiles, n_tiles), n_tiles)]
        return pltpu.make_async_copy(src, dst, sem.at[slot])

    def trips(blk):
        return (bv_ref[blk] + (unroll - 1)) // unroll

    def start_gather(blk, slot):
        def body(g, carry):
            for u in range(unroll):
                r = g * unroll + u
                row_copy(tok_ref[blk * R + r], slot, r).start()
            return carry
        lax.fori_loop(0, trips(blk), body, 0)

    @pl.when(i == 0)
    def _():
        xbuf[...] = jnp.zeros_like(xbuf)
        start_gather(0, 0)

    nxt = jnp.minimum(i + 1, n_blk - 1)

    @pl.when((i + 1 < n_blk) & (bv_ref[nxt] > 0))
    def _():
        start_gather(i + 1, (i + 1) & 1)

    @pl.when(bf_ref[i] > 0)
    def _():
        wg16[...] = wg_ref[...].astype(BF16)
        wu16[...] = wu_ref[...].astype(BF16)

    def unpack(slot):
        def wait_rows(g, carry):
            for u in range(unroll):
                row_copy(0, slot, g * unroll + u).wait()
            return carry
        lax.fori_loop(0, trips(i), wait_rows, 0)
        x16[...] = _unpack_halves([xbuf[slot, pl.ds(c, R, stride=n_tiles), :] for c in range(n_tiles)])

    for slot in range(2):
        pl.when((bv_ref[i] > 0) & ((i & 1) == slot))(functools.partial(unpack, slot))

    @pl.when(bv_ref[i] > 0)
    def _():
        x = x16[...]
        hg = jnp.dot(x, wg16[...], preferred_element_type=F32)
        hu = jnp.dot(x, wu16[...], preferred_element_type=F32)
        o_ref[...] = (hg * jax.nn.sigmoid(hg) * hu).T.astype(BF16)

    @pl.when(bv_ref[i] == 0)
    def _():
        o_ref[...] = jnp.zeros_like(o_ref)


def _moe_down_kernel(be_ref, bf_ref, bv_ref, h_ref, wd_ref, g_ref, o_ref, wdt16):
    i = pl.program_id(0)

    @pl.when(bf_ref[i] > 0)
    def _():
        wdt16[...] = wd_ref[...].T.astype(BF16)

    @pl.when(bv_ref[i] > 0)
    def _():
        yt = jnp.dot(wdt16[...], h_ref[...], preferred_element_type=F32) * g_ref[...]
        o_ref[...] = _pack_halves(yt.T.astype(BF16))

    @pl.when(bv_ref[i] == 0)
    def _():
        o_ref[...] = jnp.zeros_like(o_ref)


def _moe_experts(xn, slot_tok, slot_gate, blk_expert, blk_first, blk_valid, w_gate, w_up, w_down, layer):
    D = w_gate.shape[2]
    NR = slot_tok.shape[0]
    R = MOE_ROWS
    n_blk = NR // R
    F = w_gate.shape[3]
    hb = pl.pallas_call(
        _moe_up_kernel,
        grid_spec=pltpu.PrefetchScalarGridSpec(
            num_scalar_prefetch=4,
            grid=(n_blk,),
            in_specs=[pl.BlockSpec(memory_space=pl.ANY),
                      pl.BlockSpec((None, None, D, F), lambda i, tok, be, bf, bv: (layer, be[i], 0, 0)),
                      pl.BlockSpec((None, None, D, F), lambda i, tok, be, bf, bv: (layer, be[i], 0, 0))],
            out_specs=pl.BlockSpec((F, R), lambda i, tok, be, bf, bv: (0, i)),
            scratch_shapes=[pltpu.VMEM((2, R * (D // 2 // LANES), LANES), jnp.uint32), pltpu.SemaphoreType.DMA((2,)),
                            pltpu.VMEM((R, D), BF16), pltpu.VMEM((D, F), BF16), pltpu.VMEM((D, F), BF16)],
        ),
        out_shape=jax.ShapeDtypeStruct((F, NR), BF16),
        compiler_params=_cparams("arbitrary"),
        name="moe_up",
    )(slot_tok, blk_expert, blk_first, blk_valid, xn, w_gate, w_up)
    row = lambda i, be, bf, bv: (i, 0)
    wsel = lambda i, be, bf, bv: (layer, be[i], 0, 0)
    return pl.pallas_call(
        _moe_down_kernel,
        grid_spec=pltpu.PrefetchScalarGridSpec(
            num_scalar_prefetch=3,
            grid=(n_blk,),
            in_specs=[pl.BlockSpec((F, R), lambda i, be, bf, bv: (0, i)), pl.BlockSpec((None, None, F, D), wsel),
                      pl.BlockSpec((1, R), lambda i, be, bf, bv: (0, i))],
            out_specs=pl.BlockSpec((R, D // 2), row),
            scratch_shapes=[pltpu.VMEM((D, F), BF16)],
        ),
        out_shape=jax.ShapeDtypeStruct((NR, D // 2), jnp.uint32),
        compiler_params=_cparams("arbitrary"),
        name="moe_down",
    )(blk_expert, blk_first, blk_valid, hb, w_down, slot_gate.reshape(1, NR))


def _moe_combine_kernel(h_ref, a_ref, b_ref, o_ref):
    half = a_ref.shape[1]
    a = a_ref[...]
    b = b_ref[...]
    top = jnp.uint32(0xFFFF0000)
    f32 = lambda w: lax.bitcast_convert_type(w, F32)
    o_ref[:, :half] = h_ref[:, :half] + (f32(a << 16) + f32(b << 16))
    o_ref[:, half:] = h_ref[:, half:] + (f32(a & top) + f32(b & top))


def _moe_combine(h, ya, yb, tm=256):
    T, D = h.shape
    packed = pl.BlockSpec((tm, D // 2), lambda i: (i, 0))
    return pl.pallas_call(
        _moe_combine_kernel,
        grid=(T // tm,),
        in_specs=[pl.BlockSpec((tm, D), lambda i: (i, 0)), packed, packed],
        out_specs=pl.BlockSpec((tm, D), lambda i: (i, 0)),
        out_shape=jax.ShapeDtypeStruct((T, D), F32),
        compiler_params=_cparams("parallel"),
        name="moe_combine",
    )(h, ya, yb)


def _hier_moe(h, ln_g, w_rg, w_re, w_gate, w_up, w_down, layer):
    T, D = h.shape
    R = MOE_ROWS
    n_rout = N_GROUPS + N_EXPERTS
    w_r = jnp.pad(jnp.concatenate([w_rg, w_re], axis=1), ((0, 0), (0, LANES - n_rout)))
    logits, xn_packed = _rms_router(h, ln_g, w_r)
    gl = logits[:, :N_GROUPS]
    g_sel = jnp.argmax(gl, axis=-1)
    p_group = jnp.take_along_axis(jax.nn.softmax(gl, axis=-1), g_sel[:, None], axis=-1)
    el = logits[:, N_GROUPS:n_rout].reshape(T, N_GROUPS, EXPERTS_PER_GROUP)
    el = jnp.take_along_axis(el, g_sel[:, None, None], axis=1)[:, 0]
    top_v, top_i = lax.top_k(el, TOP_K)
    gate = p_group * jax.nn.softmax(top_v, axis=-1)
    expert = (g_sel[:, None] * EXPERTS_PER_GROUP + top_i).reshape(-1).astype(jnp.int32)
    n_assign = T * TOP_K
    counts = jnp.sum(expert[:, None] == jnp.arange(N_EXPERTS, dtype=jnp.int32)[None, :], axis=0).astype(jnp.int32)
    padded = (counts + R - 1) // R * R
    p_end = jnp.cumsum(padded)
    p_start = p_end - padded
    n_blk = -(-n_assign // R) + N_EXPERTS
    blk_start = jnp.arange(n_blk, dtype=jnp.int32) * R
    blk_valid = (blk_start < p_end[-1]).astype(jnp.int32)
    last_valid = jnp.maximum(p_end[-1] // R - 1, 0)
    blk_expert = jnp.minimum(jnp.searchsorted(p_end, blk_start, side='right'), N_EXPERTS - 1).astype(jnp.int32)
    blk_expert = jnp.where(blk_valid > 0, blk_expert, blk_expert[last_valid])
    blk_first = jnp.concatenate([jnp.ones((1,), jnp.int32), (blk_expert[1:] != blk_expert[:-1]).astype(jnp.int32)])
    blk_rows = jnp.clip((p_start + counts)[blk_expert] - blk_start, 0, R).astype(jnp.int32) * blk_valid
    n_slots = n_blk * R
    n_pad = n_slots - n_assign
    pad_end = jnp.cumsum(padded - counts)
    pad_key = jnp.sum(jnp.arange(n_pad, dtype=jnp.int32)[:, None] >= pad_end[None, :], axis=1).astype(jnp.int32)
    keys = jnp.concatenate([expert, pad_key])
    ids = jnp.concatenate([jnp.arange(n_assign, dtype=jnp.int32), jnp.full((n_pad,), -1, jnp.int32)])
    gates = jnp.concatenate([gate.reshape(-1), jnp.zeros((n_pad,), F32)])
    _, slot_id, slot_gate = lax.sort((keys, ids, gates), num_keys=1, is_stable=True)
    slot_tok = jnp.maximum(slot_id, 0) // TOP_K
    _, slot_of = lax.sort((slot_id, jnp.arange(n_slots, dtype=jnp.int32)), num_keys=1)
    pos = slot_of[n_pad:]

    ys = _moe_experts(xn_packed, slot_tok, slot_gate, blk_expert, blk_first, blk_rows, w_gate, w_up, w_down, layer)
    pos2 = pos.reshape(T, TOP_K)
    rows_of = lambda idx: ys.at[idx].get(mode="promise_in_bounds")
    return _moe_combine(h, rows_of(pos2[:, 0]), rows_of(pos2[:, 1]))


def _mem_kv(mem, g, w_kv, k_gain):
    M = mem.shape[0]
    (mn,) = _rmsnorm(mem, g, (BF16,))
    kv = _matmul(mn, w_kv.astype(BF16))
    k = kv[:, :MEM_WIDTH].reshape(M, MEM_HEADS, MEM_HEAD_DIM)
    k = k * lax.rsqrt(jnp.mean(k * k, axis=-1, keepdims=True) + NORM_EPS) * k_gain
    return k.reshape(M, MEM_WIDTH).astype(BF16), kv[:, MEM_WIDTH:].astype(BF16)


def _pad_cols(w, n):
    return jnp.pad(w, ((0, 0), (0, n - w.shape[1])))


def _a_layer_weights(w_in, mu, w_vdown, mu_vres):
    D = w_in.shape[0]
    c = np.cumsum([RWKV_WIDTH, RWKV_WIDTH, RWKV_WIDTH, LORA_DECAY, LORA_AAA, LORA_GATE, MEM_WIDTH])
    rkv, xw, xa, xg, qm = (slice(0, c[2]), slice(c[2], c[3]), slice(c[3], c[4]), slice(c[4], c[5]), slice(c[5], c[6]))
    if w_vdown is None:
        w_vdown = jnp.zeros((D, LORA_MV), w_in.dtype)
        mu_vres = jnp.zeros((LORA_MV,), mu.dtype)
    tail = A_IN_PAD - A_OFF_XV
    w = jnp.concatenate([w_in[:, rkv], w_in[:, qm], _pad_cols(w_in[:, xg], A_XG_PAD), w_in[:, xw], w_in[:, xa],
                         _pad_cols(w_vdown, tail)], axis=1)
    m = jnp.concatenate([mu[rkv], mu[qm], jnp.pad(mu[xg], (0, A_XG_PAD - LORA_GATE)), mu[xw], mu[xa],
                         jnp.pad(mu_vres, (0, tail - LORA_MV))])
    return w.astype(BF16), m


def _rwkv_layer(h, mk, mv, mq_gain, ln1, w_in, mu, w_vdown, mu_vres, y_first, w0, w_up, a0, a_up, g_up,
                v0, v_up, k_k, k_a, r_k, lnx_g, lnx_b, w_out):
    (xn,) = _rmsnorm(h, ln1, (BF16,))
    w_pad, mu_pad = _a_layer_weights(w_in, mu, w_vdown, mu_vres)
    y = _matmul_shift(xn, w_pad, mu_pad)
    pad_rows = lambda w, n: jnp.pad(w, ((0, n - w.shape[0]), (0, 0))).astype(BF16)
    has_vres = y_first is not None
    vecs = jnp.stack([w0, a0, v0 if has_vres else jnp.zeros_like(w0), k_k, k_a, r_k.reshape(-1), lnx_g, lnx_b])
    mix_out = _rwkv_mix(y, y_first, w_up.astype(BF16), a_up.astype(BF16), pad_rows(g_up, A_XG_PAD),
                        pad_rows(v_up, A_XV_PAD) if has_vres else None, vecs)
    mem_out = _mem_attend(y, A_OFF_QM // MEM_WIDTH, mk, mv, mq_gain)
    return _matmul2_res(mix_out, mem_out, w_out.astype(BF16), h), y


def _dilated_layer(h, mk, mv, mq_gain, ln1, w_in, q_gain, ks, vs, w_out):
    (xn,) = _rmsnorm(h, ln1, (BF16,))
    y = _matmul(xn, w_in.astype(BF16))
    parts = [_dilated_group(y, gi, dil, ks, vs, q_gain) for gi, (_, dil) in enumerate(DIL_PATTERNS)]
    att = _merge_groups(parts)
    mem_out = _mem_attend(y, DIL_Q_WIDTH // MEM_WIDTH, mk, mv, mq_gain)
    return _matmul2_res(att, mem_out, w_out.astype(BF16), h)


def _shared_kv(h, g, w_kv, k_gain):
    T = h.shape[0]
    (xn,) = _rmsnorm(h, g, (BF16,))
    kv = _matmul(xn, w_kv.astype(BF16))
    k = kv[:, :KV_WIDTH].reshape(T, DIL_HEADS, ATT_HEAD_DIM)
    k = k * lax.rsqrt(jnp.mean(k * k, axis=-1, keepdims=True) + NORM_EPS) * k_gain
    return k.reshape(T, KV_WIDTH), kv[:, KV_WIDTH:]


def kernel(x, mem, a_ln1, a_w_in, a_w_vdown, a_mu, a_mu_vres, a_w0, a_w_up, a_a0, a_a_up, a_g_up, a_v0, a_v_up, a_k_k, a_k_a, a_r_k, a_lnx_g, a_lnx_b, a_w_out, b_ln1, b_w_in, b_q_norm, b_w_out, s_kv_norm, s_w_kv, s_k_norm, m_norm, m_w_kv, m_q_norm, m_k_norm, moe_ln, moe_router_group, moe_router_expert, moe_w_gate, moe_w_up, moe_w_down):
    Bsz, S, D = x.shape
    assert Bsz == 1 and S % (max(d for _, d in DIL_PATTERNS) * ATT_BLOCK) == 0
    depth = moe_ln.shape[0]
    h = x.reshape(S, D)
    mem2 = mem.reshape(mem.shape[1], D)
    y_first = None
    shared = None
    for l in range(depth):
        mk, mv = _mem_kv(mem2, m_norm[l], m_w_kv[l], m_k_norm[l])
        if l < N_A:
            i = l
            has_vres = i > 0
            h, y_i = _rwkv_layer(
                h, mk, mv, m_q_norm[l], a_ln1[i], a_w_in[i], a_mu[i],
                a_w_vdown[i - 1] if has_vres else None, a_mu_vres[i - 1] if has_vres else None,
                y_first, a_w0[i], a_w_up[i], a_a0[i], a_a_up[i], a_g_up[i],
                a_v0[i - 1] if has_vres else None, a_v_up[i - 1] if has_vres else None,
                a_k_k[i], a_k_a[i], a_r_k[i],
                a_lnx_g[i], a_lnx_b[i], a_w_out[i])
            if i == 0:
                y_first = y_i
        else:
            j = l - N_A
            if shared is None:
                shared = _shared_kv(h, s_kv_norm, s_w_kv, s_k_norm)
            h = _dilated_layer(h, mk, mv, m_q_norm[l], b_ln1[j], b_w_in[j], b_q_norm[j], shared[0], shared[1], b_w_out[j])
        h = _hier_moe(h, moe_ln[l], moe_router_group[l], moe_router_expert[l], moe_w_gate, moe_w_up, moe_w_down, l)
    return h.reshape(Bsz, S, D)
```

```python
import functools

import jax
import jax.numpy as jnp
import numpy as np
from jax import lax
from jax.experimental import pallas as pl
from jax.experimental.pallas import tpu as pltpu

F32 = jnp.float32
BF16 = jnp.bfloat16

D_MODEL = 4096
RWKV_HEAD = 64
RWKV_WIDTH = 3072
LORA_DECAY = 128
LORA_AAA = 128
LORA_MV = 96
LORA_GATE = 480
LNX_EPS = 64e-5
MEM_HEADS = 4
MEM_WIDTH = 1024
MEM_HEAD_DIM = 256
ATT_HEAD_DIM = 128
DIL_PATTERNS = ((128, 1), (512, 4), (2048, 16))
N_DIL_GROUPS = 3
DIL_HEADS = 8
DIL_Q_WIDTH = 3072
KV_WIDTH = 1024
ATT_BLOCK = 128
N_GROUPS = 4
EXPERTS_PER_GROUP = 8
N_EXPERTS = 32
TOP_K = 2
D_FF = 384
NORM_EPS = 1e-6
N_A = 2

LANES = 128
SUBLANES = 8
VMEM_LIMIT_BYTES = 56 * 1024 * 1024

A_OFF_R = 0
A_OFF_K = RWKV_WIDTH
A_OFF_V = 2 * RWKV_WIDTH
A_OFF_QM = 3 * RWKV_WIDTH
A_OFF_XG = A_OFF_QM + MEM_WIDTH
A_XG_PAD = 512
A_OFF_XW = A_OFF_XG + A_XG_PAD
A_OFF_XA = A_OFF_XW + LORA_DECAY
A_OFF_XV = A_OFF_XA + LORA_AAA
A_XV_PAD = 128
A_IN_PAD = 11264

SCAN_CHUNK = 64
SCAN_TBLOCK = 128
MOE_ROWS = 256

_NN = (((1,), (0,)), ((), ()))
_NT = (((1,), (1,)), ((), ()))


def _cparams(*sem):
    return pltpu.CompilerParams(dimension_semantics=sem, vmem_limit_bytes=VMEM_LIMIT_BYTES)


def _rms_kernel(x_ref, g_ref, *o_refs):
    x = x_ref[...]
    y = x * lax.rsqrt(jnp.mean(x * x, axis=-1, keepdims=True) + NORM_EPS) * g_ref[...]
    for o_ref in o_refs:
        o_ref[...] = y.astype(o_ref.dtype)


def _rmsnorm(x, g, dtypes, tm=256):
    T, D = x.shape
    tm = min(tm, T)
    outs = pl.pallas_call(
        _rms_kernel,
        grid=(T // tm,),
        in_specs=[pl.BlockSpec((tm, D), lambda i: (i, 0)), pl.BlockSpec((1, D), lambda i: (0, 0))],
        out_specs=[pl.BlockSpec((tm, D), lambda i: (i, 0)) for _ in dtypes],
        out_shape=[jax.ShapeDtypeStruct((T, D), dt) for dt in dtypes],
        compiler_params=_cparams("parallel"),
        name="rmsnorm",
    )(x, g.reshape(1, D))
    return outs


def _pack_halves(x16):
    bits = lax.bitcast_convert_type(x16.astype(F32), jnp.uint32)
    half = bits.shape[1] // 2
    return (bits[:, :half] >> 16) | bits[:, half:]


def _unpack_halves(words):
    lo = [lax.bitcast_convert_type(w << 16, F32) for w in words]
    hi = [lax.bitcast_convert_type(w & jnp.uint32(0xFFFF0000), F32) for w in words]
    return jnp.concatenate(lo + hi, axis=1).astype(BF16)


def _rms_router_kernel(x_ref, g_ref, w_ref, logit_ref, packed_ref):
    x = x_ref[...]
    y = x * lax.rsqrt(jnp.mean(x * x, axis=-1, keepdims=True) + NORM_EPS) * g_ref[...]
    yh = y.astype(BF16)
    yl = (y - yh.astype(F32)).astype(BF16)
    w = w_ref[...]
    wh = w.astype(BF16)
    wl = (w - wh.astype(F32)).astype(BF16)
    d = lambda a, b: jnp.dot(a, b, preferred_element_type=F32)
    logit_ref[...] = d(yh, wh) + (d(yh, wl) + d(yl, wh))
    packed = _pack_halves(yh)
    tm = packed.shape[0]
    n_tiles = packed.shape[1] // LANES
    for c in range(n_tiles):
        packed_ref[pl.ds(c, tm, stride=n_tiles), :] = packed[:, c * LANES:(c + 1) * LANES]


def _rms_router(x, g, w_r, tm=256):
    T, D = x.shape
    N = w_r.shape[1]
    n_tiles = D // 2 // LANES
    return pl.pallas_call(
        _rms_router_kernel,
        grid=(T // tm,),
        in_specs=[pl.BlockSpec((tm, D), lambda i: (i, 0)), pl.BlockSpec((1, D), lambda i: (0, 0)),
                  pl.BlockSpec((D, N), lambda i: (0, 0))],
        out_specs=[pl.BlockSpec((tm, N), lambda i: (i, 0)), pl.BlockSpec((tm * n_tiles, LANES), lambda i: (i, 0))],
        out_shape=[jax.ShapeDtypeStruct((T, N), F32), jax.ShapeDtypeStruct((T * n_tiles, LANES), jnp.uint32)],
        compiler_params=_cparams("parallel"),
        name="rms_router",
    )(x, g.reshape(1, D), w_r)


def _mm_kernel(x_ref, w_ref, o_ref):
    o_ref[...] = jnp.dot(x_ref[...], w_ref[...].astype(BF16), preferred_element_type=F32)


def _mm_shift_kernel(x_ref, xp_ref, w_ref, mu_ref, o_ref):
    w = w_ref[...]
    y = jnp.dot(x_ref[...], w, preferred_element_type=F32)
    tail = jnp.dot(xp_ref[...], w, preferred_element_type=F32)
    last = jnp.where(pl.program_id(0) > 0, tail[tail.shape[0] - 1:, :], 0.0)
    row = lax.broadcasted_iota(jnp.int32, y.shape, 0)
    y_prev = jnp.where(row == 0, last, pltpu.roll(y, 1, axis=0))
    o_ref[...] = y + mu_ref[...] * (y_prev - y)


def _matmul_shift(x, w, mu, tm=1024, tn=512):
    M, K = x.shape
    N = w.shape[1]
    tm = min(tm, M)
    tp = 2 * SUBLANES
    assert M % tm == 0 and N % tn == 0 and tm % tp == 0
    return pl.pallas_call(
        _mm_shift_kernel,
        grid=(M // tm, N // tn),
        in_specs=[pl.BlockSpec((tm, K), lambda i, j: (i, 0)),
                  pl.BlockSpec((tp, K), lambda i, j: (jnp.maximum(i * (tm // tp) - 1, 0), 0)),
                  pl.BlockSpec((K, tn), lambda i, j: (0, j)),
                  pl.BlockSpec((1, tn), lambda i, j: (0, j))],
        out_specs=pl.BlockSpec((tm, tn), lambda i, j: (i, j)),
        out_shape=jax.ShapeDtypeStruct((M, N), F32),
        compiler_params=_cparams("parallel", "parallel"),
        name="matmul_shift",
    )(x, x, w, mu.reshape(1, N))


def _mm2_res_kernel(x1_ref, x2_ref, w1_ref, w2_ref, r_ref, o_ref):
    acc = jnp.dot(x1_ref[...], w1_ref[...].astype(BF16), preferred_element_type=F32)
    o_ref[...] = r_ref[...] + (acc + jnp.dot(x2_ref[...], w2_ref[...].astype(BF16), preferred_element_type=F32))


def _matmul2_res(x1, x2, w, layer, res, tm=1024, tn=512):
    M, K1 = x1.shape
    K2 = x2.shape[1]
    N = w.shape[2]
    tm = min(tm, M)
    assert M % tm == 0 and N % tn == 0 and K1 % K2 == 0 and w.shape[1] == K1 + K2
    return pl.pallas_call(
        _mm2_res_kernel,
        grid=(M // tm, N // tn),
        in_specs=[pl.BlockSpec((tm, K1), lambda i, j: (i, 0)), pl.BlockSpec((tm, K2), lambda i, j: (i, 0)),
                  pl.BlockSpec((None, K1, tn), lambda i, j: (layer, 0, j)),
                  pl.BlockSpec((None, K2, tn), lambda i, j: (layer, K1 // K2, j)),
                  pl.BlockSpec((tm, tn), lambda i, j: (i, j))],
        out_specs=pl.BlockSpec((tm, tn), lambda i, j: (i, j)),
        out_shape=jax.ShapeDtypeStruct((M, N), F32),
        compiler_params=_cparams("parallel", "parallel"),
        name="matmul_out",
    )(x1, x2, w, w, res)


def _matmul(x, w, layer, tm=1024, tn=512):
    M, K = x.shape
    N = w.shape[2]
    tm = min(tm, M)
    tn = min(tn, N)
    assert M % tm == 0 and N % tn == 0
    return pl.pallas_call(
        _mm_kernel,
        grid=(M // tm, N // tn),
        in_specs=[pl.BlockSpec((tm, K), lambda i, j: (i, 0)), pl.BlockSpec((None, K, tn), lambda i, j: (layer, 0, j))],
        out_specs=pl.BlockSpec((tm, tn), lambda i, j: (i, j)),
        out_shape=jax.ShapeDtypeStruct((M, N), F32),
        compiler_params=_cparams("parallel", "parallel"),
        name="matmul",
    )(x, w)


def _split(x):
    hi = x.astype(BF16)
    lo = (x - hi.astype(F32)).astype(BF16)
    return hi, lo


def _dot3(a, b, dims=_NN):
    ah, al = a
    bh, bl = b
    d = lambda x, y: lax.dot_general(x, y, dims, preferred_element_type=F32)
    return d(ah, bh) + (d(ah, bl) + d(al, bh))


def _dot1(a, b, dims=_NN):
    return lax.dot_general(a.astype(BF16), b.astype(BF16), dims, preferred_element_type=F32)


def _round_robin(gens):
    outs = [None] * len(gens)
    active = list(range(len(gens)))
    while active:
        for i in list(active):
            try:
                next(gens[i])
            except StopIteration as stop:
                outs[i] = stop.value
                active.remove(i)
    return outs


def _rwkv_mix_kernel(*refs, n_chunks, n_pairs, has_vres):
    if has_vres:
        (r_ref, k_ref, v_ref, xg_ref, xw_ref, xa_ref, xv_ref, vf_ref, wup_ref, aup_ref, gup_ref, vup_ref, vec_ref,
         o_ref, s_ref, lw_s, k_s, v_s, a_s, b_s, y_s) = refs
    else:
        (r_ref, k_ref, v_ref, xg_ref, xw_ref, xa_ref, wup_ref, aup_ref, gup_ref, vec_ref,
         o_ref, s_ref, lw_s, k_s, v_s, a_s, b_s, y_s) = refs
    C = SCAN_CHUNK
    H2 = 2 * C

    @pl.when(pl.program_id(0) == 0)
    def _():
        s_ref[...] = jnp.zeros_like(s_ref)

    lane = lax.broadcasted_iota(jnp.int32, (C, LANES), 1)
    head0 = lane < RWKV_HEAD
    trow = lax.broadcasted_iota(jnp.int32, (C, LANES), 0)
    tcol = lane & (C - 1)
    lane2 = lax.broadcasted_iota(jnp.int32, (H2, LANES), 1)
    row2 = lax.broadcasted_iota(jnp.int32, (H2, LANES), 0)
    blockdiag = (row2 < C) == (lane2 < RWKV_HEAD)
    head_ones = jnp.where(blockdiag, 1.0, 0.0).astype(BF16)
    tri_r = lax.broadcasted_iota(jnp.int32, (C, C), 0)
    tri_c = lax.broadcasted_iota(jnp.int32, (C, C), 1)
    ltri = jnp.where(tri_r >= tri_c, 1.0, 0.0).astype(BF16)
    arow = lax.broadcasted_iota(jnp.int32, (H2, 2 * H2), 0)
    acol = lax.broadcasted_iota(jnp.int32, (H2, 2 * H2), 1)
    a_t = arow & (C - 1)
    a_j = acol & (C - 1)
    amask = a_t + jnp.where(arow < C, 0, 1) > a_j
    n_levels = C.bit_length() - 1
    eye2 = jnp.where(trow == tcol, 1.0, 0.0)
    dd = lambda x, y: jnp.dot(x, y, preferred_element_type=F32)
    tiles = [slice(p * LANES, (p + 1) * LANES) for p in range(n_pairs)]
    vec = lambda i, ls: vec_ref[i:i + 1, ls]

    def head_sum(x):
        hi, lo = _split(x)
        return dd(hi, head_ones) + dd(lo, head_ones)

    def level_mask(lvl):
        tb_, jb_ = trow >> lvl, tcol >> lvl
        return (tb_ - jb_) * (tb_ & 1) == 1

    def stack2(x):
        z = jnp.zeros_like(x)
        return jnp.concatenate([jnp.where(head0, x, z), jnp.where(head0, z, x)], axis=0)

    xw = jnp.tanh(xw_ref[...]).astype(BF16)
    xa = xa_ref[...].astype(BF16)
    if has_vres:
        xv = xv_ref[...].astype(BF16)
    for ls in tiles:
        z = vec(_V_W0, ls) + dd(xw, wup_ref[:, ls])
        lw_s[:, ls] = jax.nn.sigmoid(z) * (-float(np.exp(-0.5)))
        a = jax.nn.sigmoid(vec(_V_A0, ls) + dd(xa, aup_ref[:, ls]))
        k = k_ref[:, ls]
        kk = k * vec(_V_KK, ls)
        kk = kk * lax.rsqrt(jnp.maximum(head_sum(kk * kk), 1e-24))
        v = v_ref[:, ls]
        if has_vres:
            mix = jax.nn.sigmoid(vec(_V_V0, ls) + dd(xv, vup_ref[:, ls]))
            v = v + (vf_ref[:, ls] - v) * mix
        k_s[:, ls] = k * (1.0 + (a - 1.0) * vec(_V_KA, ls))
        v_s[:, ls] = v
        a_s[:, ls] = -kk
        b_s[:, ls] = kk * a

    def pair_chunk(r, lw, k, v, a, b, s):
        l1 = lw.astype(BF16)
        rem = lw - l1.astype(F32)
        l2 = rem.astype(BF16)
        l3 = (rem - l2.astype(F32)).astype(BF16)
        cum = dd(ltri, l1) + (dd(ltri, l2) + dd(ltri, l3))
        yield
        p_inc = jnp.exp(cum)
        p_exc = jnp.exp(cum - lw)
        p_inv = jnp.exp(-cum)
        at = a * p_exc
        rt = r * p_inc
        bt = b * p_inv
        kt = k * p_inv
        p_tot = p_inc[C - 1:C, :]

        x_ar = _split(jnp.concatenate([at, rt], axis=0))
        w_bk = jnp.concatenate([stack2(bt), stack2(kt)], axis=0).astype(BF16)
        amat = jnp.where(amask, lax.dot_general(x_ar[0], w_bk, _NT, preferred_element_type=F32), 0.0)
        xs = _dot3(x_ar, _split(s), _NT)
        yield

        n_ab = amat[:C, :H2]
        a_ak = amat[:C, H2:]
        a_r = amat[C:, :]
        v2 = stack2(v)
        rhs = xs[:C] + _dot1(a_ak, v2)
        yield

        tinv = eye2 + jnp.where(level_mask(0), n_ab, 0.0)
        for lvl in range(1, n_levels):
            nt = _dot1(jnp.where(level_mask(lvl), n_ab, 0.0), stack2(tinv))
            yield
            tinv = tinv + _dot1(tinv, stack2(nt))
            yield
        u = _dot1(tinv, stack2(rhs))
        yield

        uv = jnp.concatenate([stack2(u), v2], axis=0)
        y = xs[C:] + _dot1(a_r, uv)
        yield

        uv_t = jnp.concatenate([u, v], axis=0).T
        bk = jnp.concatenate([bt, kt], axis=0)
        ds = _dot3(_split(uv_t), _split(bk))
        return y, (s + jnp.where(blockdiag, ds, 0.0)) * p_tot

    def chunk(ci, carry):
        sl = pl.ds(pl.multiple_of(ci * C, C), C)
        ins = [tuple(ref[sl, ls] for ref in (r_ref, lw_s, k_s, v_s, a_s, b_s)) + (s_ref[p],)
               for p, ls in enumerate(tiles)]
        outs = _round_robin([pair_chunk(*args) for args in ins])
        for p, ls in enumerate(tiles):
            y_s[sl, ls] = outs[p][0]
            s_ref[p] = outs[p][1]
        return carry

    lax.fori_loop(0, n_chunks, chunk, 0)

    xg = jax.nn.sigmoid(xg_ref[...]).astype(BF16)
    inv_n = 1.0 / RWKV_HEAD
    for ls in tiles:
        y = y_s[:, ls]
        d = y - head_sum(y) * inv_n
        var = head_sum(d * d) * inv_n
        yn = d * lax.rsqrt(var + LNX_EPS) * vec(_V_LNG, ls) + vec(_V_LNB, ls)
        bonus = head_sum(r_ref[:, ls] * k_s[:, ls] * vec(_V_RK, ls)) * v_s[:, ls]
        o_ref[:, ls] = ((yn + bonus) * dd(xg, gup_ref[:, ls])).astype(BF16)


_V_W0, _V_A0, _V_V0, _V_KK, _V_KA, _V_RK, _V_LNG, _V_LNB = range(8)


def _rwkv_mix(y, y_first, w_up, a_up, g_up, v_up, vecs):
    T = y.shape[0]
    W = RWKV_WIDTH
    tb = min(SCAN_TBLOCK, T)
    has_vres = y_first is not None
    assert T % tb == 0 and tb % SCAN_CHUNK == 0
    wide = lambda blk: pl.BlockSpec((tb, W), lambda t: (t, blk))
    narrow = lambda off, width: pl.BlockSpec((tb, width), lambda t: (t, off // width))
    full = lambda arr: pl.BlockSpec(arr.shape, lambda t: (0, 0))
    in_specs = [wide(A_OFF_R // W), wide(A_OFF_K // W), wide(A_OFF_V // W),
                narrow(A_OFF_XG, A_XG_PAD), narrow(A_OFF_XW, LORA_DECAY), narrow(A_OFF_XA, LORA_AAA)]
    args = [y, y, y, y, y, y]
    if has_vres:
        in_specs += [narrow(A_OFF_XV, A_XV_PAD), wide(A_OFF_V // W)]
        args += [y, y_first]
    weights = [w_up, a_up, g_up] + ([v_up] if has_vres else []) + [vecs]
    in_specs += [full(w) for w in weights]
    args += weights
    return pl.pallas_call(
        functools.partial(_rwkv_mix_kernel, n_chunks=tb // SCAN_CHUNK, n_pairs=W // LANES, has_vres=has_vres),
        grid=(T // tb,),
        in_specs=in_specs,
        out_specs=pl.BlockSpec((tb, W), lambda t: (t, 0)),
        out_shape=jax.ShapeDtypeStruct((T, W), BF16),
        scratch_shapes=[pltpu.VMEM((W // LANES, LANES, LANES), F32)] + [pltpu.VMEM((tb, W), F32)] * 6,
        compiler_params=_cparams("arbitrary"),
        name="rwkv7_mix",
    )(*args)


def _mem_attn_kernel(q_ref, k_ref, v_ref, g_ref, o_ref):
    scale = MEM_HEAD_DIM ** -0.5
    for h in range(MEM_HEADS):
        hs = slice(h * MEM_HEAD_DIM, (h + 1) * MEM_HEAD_DIM)
        q = q_ref[:, hs]
        qn = q * lax.rsqrt(jnp.mean(q * q, axis=-1, keepdims=True) + NORM_EPS) * g_ref[...]
        s = lax.dot_general(qn.astype(BF16), k_ref[:, hs], _NT, preferred_element_type=F32) * scale
        m = jnp.max(s, axis=-1, keepdims=True)
        p = jnp.exp(s - m)
        l = jnp.sum(p, axis=-1, keepdims=True)
        o = jnp.dot(p.astype(BF16), v_ref[:, hs], preferred_element_type=F32)
        o_ref[:, hs] = (o / l).astype(o_ref.dtype)


def _mem_attend(y, col_block, mk, mv, q_gain, tm=512):
    T = y.shape[0]
    M = mk.shape[0]
    tm = min(tm, T)
    return pl.pallas_call(
        _mem_attn_kernel,
        grid=(T // tm,),
        in_specs=[
            pl.BlockSpec((tm, MEM_WIDTH), lambda i: (i, col_block)),
            pl.BlockSpec((M, MEM_WIDTH), lambda i: (0, 0)),
            pl.BlockSpec((M, MEM_WIDTH), lambda i: (0, 0)),
            pl.BlockSpec((1, MEM_HEAD_DIM), lambda i: (0, 0)),
        ],
        out_specs=pl.BlockSpec((tm, MEM_WIDTH), lambda i: (i, 0)),
        out_shape=jax.ShapeDtypeStruct((T, MEM_WIDTH), BF16),
        compiler_params=_cparams("parallel"),
        name="mem_attend",
    )(y, mk, mv, q_gain.reshape(1, MEM_HEAD_DIM))


def _dil_attn_kernel(sl_ref, q_ref, kp_ref, kc_ref, vp_ref, vc_ref, g_ref, o_ref, lse_ref, *, dil, heads):
    n = pl.program_id(0)
    h0 = pl.program_id(1) * heads
    B = ATT_BLOCK
    n_back = B
    scale = ATT_HEAD_DIM ** -0.5
    qi = lax.broadcasted_iota(jnp.int32, (B, 2 * B), 0)
    ki = lax.broadcasted_iota(jnp.int32, (B, 2 * B), 1)
    j = B + qi - ki
    valid = (j >= 0) & (j <= n_back) & ((ki >= B) | (n > 0))
    dist = (j * dil).astype(F32)
    for c in range(dil):
        rows = pl.ds(c, B, stride=dil) if dil > 1 else slice(None)
        for h in range(heads):
            hs = slice(h * ATT_HEAD_DIM, (h + 1) * ATT_HEAD_DIM)
            q = q_ref[rows, hs]
            qn = q * lax.rsqrt(jnp.mean(q * q, axis=-1, keepdims=True) + NORM_EPS) * g_ref[...]
            kcat = jnp.concatenate([kp_ref[rows, hs], kc_ref[rows, hs]], axis=0).astype(BF16)
            vcat = jnp.concatenate([vp_ref[rows, hs], vc_ref[rows, hs]], axis=0).astype(BF16)
            s = lax.dot_general(qn.astype(BF16), kcat, _NT, preferred_element_type=F32) * scale
            s = jnp.where(valid, s - sl_ref[h0 + h] * dist, -1e30)
            m = jnp.max(s, axis=-1, keepdims=True)
            p = jnp.exp(s - m)
            l = jnp.sum(p, axis=-1, keepdims=True)
            o = jnp.dot(p.astype(BF16), vcat, preferred_element_type=F32)
            o_ref[rows, hs] = o / l
            lse_ref[rows, hs] = jnp.broadcast_to(m + jnp.log(l), (B, ATT_HEAD_DIM))


def _dilated_group(y, gi, dil, k, v, q_gain):
    T, W = y.shape
    n_heads_total = N_DIL_GROUPS * DIL_HEADS
    slopes = jnp.asarray([2.0 ** (-8.0 * (gi * DIL_HEADS + h + 1) / n_heads_total) for h in range(DIL_HEADS)], F32)
    rows = ATT_BLOCK * dil
    cols = KV_WIDTH if dil == 1 else ATT_HEAD_DIM
    ncb = KV_WIDTH // cols
    blk = (rows, cols)
    cur = lambda n, hb: (n, hb)
    prev = lambda n, hb: (jnp.maximum(n - 1, 0), hb)
    o, lse = pl.pallas_call(
        functools.partial(_dil_attn_kernel, dil=dil, heads=cols // ATT_HEAD_DIM),
        grid=(T // rows, ncb),
        in_specs=[
            pl.BlockSpec(memory_space=pltpu.SMEM),
            pl.BlockSpec(blk, lambda n, hb: (n, gi * ncb + hb)),
            pl.BlockSpec(blk, prev), pl.BlockSpec(blk, cur),
            pl.BlockSpec(blk, prev), pl.BlockSpec(blk, cur),
            pl.BlockSpec((1, ATT_HEAD_DIM), lambda n, hb: (0, 0)),
        ],
        out_specs=[pl.BlockSpec(blk, cur), pl.BlockSpec(blk, cur)],
        out_shape=[jax.ShapeDtypeStruct((T, KV_WIDTH), F32)] * 2,
        compiler_params=_cparams("parallel", "parallel"),
        name=f"dilated_attn_d{dil}",
    )(slopes, y, k, k, v, v, q_gain.reshape(1, ATT_HEAD_DIM))
    return o, lse


def _merge_kernel(o0, l0, o1, l1, o2, l2, out_ref):
    a0, a1, a2 = l0[...], l1[...], l2[...]
    m = jnp.maximum(jnp.maximum(a0, a1), a2)
    w0, w1, w2 = jnp.exp(a0 - m), jnp.exp(a1 - m), jnp.exp(a2 - m)
    out_ref[...] = ((w0 * o0[...] + w1 * o1[...] + w2 * o2[...]) / (w0 + w1 + w2)).astype(out_ref.dtype)


def _merge_groups(parts, tm=512):
    T, W = parts[0][0].shape
    tm = min(tm, T)
    spec = pl.BlockSpec((tm, W), lambda i: (i, 0))
    flat = [t for pr in parts for t in pr]
    return pl.pallas_call(
        _merge_kernel,
        grid=(T // tm,),
        in_specs=[spec] * 6,
        out_specs=spec,
        out_shape=jax.ShapeDtypeStruct((T, W), BF16),
        compiler_params=_cparams("parallel"),
        name="dilated_merge",
    )(*flat)


def _moe_up_kernel(tok_ref, be_ref, bf_ref, bv_ref, x_hbm, wg_ref, wu_ref, o_ref, xbuf, sem, x16, wt16):
    i = pl.program_id(0)
    n_blk = pl.num_programs(0)
    R = MOE_ROWS
    n_tiles = xbuf.shape[1] // R
    unroll = SUBLANES

    def row_copy(tok, slot, r):
        src = x_hbm.at[pl.ds(pl.multiple_of(tok * n_tiles, n_tiles), n_tiles)]
        dst = xbuf.at[slot, pl.ds(pl.multiple_of(r * n_tiles, n_tiles), n_tiles)]
        return pltpu.make_async_copy(src, dst, sem.at[slot])

    def trips(blk):
        return (bv_ref[blk] + (unroll - 1)) // unroll

    def start_gather(blk, slot):
        def body(g, carry):
            for u in range(unroll):
                r = g * unroll + u
                row_copy(tok_ref[blk * R + r], slot, r).start()
            return carry
        lax.fori_loop(0, trips(blk), body, 0)

    @pl.when(i == 0)
    def _():
        xbuf[...] = jnp.zeros_like(xbuf)
        start_gather(0, 0)

    nxt = jnp.minimum(i + 1, n_blk - 1)

    @pl.when((i + 1 < n_blk) & (bv_ref[nxt] > 0))
    def _():
        start_gather(i + 1, (i + 1) & 1)

    @pl.when(bf_ref[i] > 0)
    def _():
        F = wg_ref.shape[1]
        wt16[:F, :] = wg_ref[...].T.astype(BF16)
        wt16[F:, :] = wu_ref[...].T.astype(BF16)

    def unpack(slot):
        def wait_rows(g, carry):
            for u in range(unroll):
                row_copy(0, slot, g * unroll + u).wait()
            return carry
        lax.fori_loop(0, trips(i), wait_rows, 0)
        x16[...] = _unpack_halves([xbuf[slot, pl.ds(c, R, stride=n_tiles), :] for c in range(n_tiles)])

    for slot in range(2):
        pl.when((bv_ref[i] > 0) & ((i & 1) == slot))(functools.partial(unpack, slot))

    @pl.when(bv_ref[i] > 0)
    def _():
        F = o_ref.shape[0]
        ht = lax.dot_general(wt16[...], x16[...], _NT, preferred_element_type=F32)
        hg = ht[:F]
        o_ref[...] = (hg * jax.nn.sigmoid(hg) * ht[F:]).astype(BF16)

    @pl.when(bv_ref[i] == 0)
    def _():
        o_ref[...] = jnp.zeros_like(o_ref)


def _moe_down_kernel(be_ref, bf_ref, bv_ref, h_ref, wd_ref, g_ref, o_ref, wdt16):
    i = pl.program_id(0)

    @pl.when(bf_ref[i] > 0)
    def _():
        wdt16[...] = wd_ref[...].T.astype(BF16)

    @pl.when(bv_ref[i] > 0)
    def _():
        yt = jnp.dot(wdt16[...], h_ref[...], preferred_element_type=F32) * g_ref[...]
        o_ref[...] = _pack_halves(yt.T.astype(BF16))

    @pl.when(bv_ref[i] == 0)
    def _():
        o_ref[...] = jnp.zeros_like(o_ref)


def _moe_experts(xn, slot_tok, slot_gate, blk_expert, blk_first, blk_valid, w_gate, w_up, w_down, layer):
    D = w_gate.shape[2]
    NR = slot_tok.shape[0]
    R = MOE_ROWS
    n_blk = NR // R
    F = w_gate.shape[3]
    hb = pl.pallas_call(
        _moe_up_kernel,
        grid_spec=pltpu.PrefetchScalarGridSpec(
            num_scalar_prefetch=4,
            grid=(n_blk,),
            in_specs=[pl.BlockSpec(memory_space=pl.ANY),
                      pl.BlockSpec((None, None, D, F), lambda i, tok, be, bf, bv: (layer, be[i], 0, 0)),
                      pl.BlockSpec((None, None, D, F), lambda i, tok, be, bf, bv: (layer, be[i], 0, 0))],
            out_specs=pl.BlockSpec((F, R), lambda i, tok, be, bf, bv: (0, i)),
            scratch_shapes=[pltpu.VMEM((2, R * (D // 2 // LANES), LANES), jnp.uint32), pltpu.SemaphoreType.DMA((2,)),
                            pltpu.VMEM((R, D), BF16), pltpu.VMEM((2 * F, D), BF16)],
        ),
        out_shape=jax.ShapeDtypeStruct((F, NR), BF16),
        compiler_params=_cparams("arbitrary"),
        name="moe_up",
    )(slot_tok, blk_expert, blk_first, blk_valid, xn, w_gate, w_up)
    row = lambda i, be, bf, bv: (i, 0)
    wsel = lambda i, be, bf, bv: (layer, be[i], 0, 0)
    return pl.pallas_call(
        _moe_down_kernel,
        grid_spec=pltpu.PrefetchScalarGridSpec(
            num_scalar_prefetch=3,
            grid=(n_blk,),
            in_specs=[pl.BlockSpec((F, R), lambda i, be, bf, bv: (0, i)), pl.BlockSpec((None, None, F, D), wsel),
                      pl.BlockSpec((1, R), lambda i, be, bf, bv: (0, i))],
            out_specs=pl.BlockSpec((R, D // 2), row),
            scratch_shapes=[pltpu.VMEM((D, F), BF16)],
        ),
        out_shape=jax.ShapeDtypeStruct((NR, D // 2), jnp.uint32),
        compiler_params=_cparams("arbitrary"),
        name="moe_down",
    )(blk_expert, blk_first, blk_valid, hb, w_down, slot_gate.reshape(1, NR))


def _moe_combine_kernel(h_ref, a_ref, b_ref, o_ref):
    half = a_ref.shape[1]
    a = a_ref[...]
    b = b_ref[...]
    top = jnp.uint32(0xFFFF0000)
    f32 = lambda w: lax.bitcast_convert_type(w, F32)
    o_ref[:, :half] = h_ref[:, :half] + (f32(a << 16) + f32(b << 16))
    o_ref[:, half:] = h_ref[:, half:] + (f32(a & top) + f32(b & top))


def _moe_combine(h, ya, yb, tm=256):
    T, D = h.shape
    packed = pl.BlockSpec((tm, D // 2), lambda i: (i, 0))
    return pl.pallas_call(
        _moe_combine_kernel,
        grid=(T // tm,),
        in_specs=[pl.BlockSpec((tm, D), lambda i: (i, 0)), packed, packed],
        out_specs=pl.BlockSpec((tm, D), lambda i: (i, 0)),
        out_shape=jax.ShapeDtypeStruct((T, D), F32),
        compiler_params=_cparams("parallel"),
        name="moe_combine",
    )(h, ya, yb)


def _hier_moe(h, ln_g, w_rg, w_re, w_gate, w_up, w_down, layer):
    T, D = h.shape
    R = MOE_ROWS
    n_rout = N_GROUPS + N_EXPERTS
    w_r = jnp.pad(jnp.concatenate([w_rg, w_re], axis=1), ((0, 0), (0, LANES - n_rout)))
    logits, xn_packed = _rms_router(h, ln_g, w_r)
    gl = logits[:, :N_GROUPS]
    g_sel = jnp.argmax(gl, axis=-1)
    p_group = jnp.take_along_axis(jax.nn.softmax(gl, axis=-1), g_sel[:, None], axis=-1)
    el = logits[:, N_GROUPS:n_rout].reshape(T, N_GROUPS, EXPERTS_PER_GROUP)
    el = jnp.take_along_axis(el, g_sel[:, None, None], axis=1)[:, 0]
    top_v, top_i = lax.top_k(el, TOP_K)
    gate = p_group * jax.nn.softmax(top_v, axis=-1)
    expert = (g_sel[:, None] * EXPERTS_PER_GROUP + top_i).reshape(-1).astype(jnp.int32)
    n_assign = T * TOP_K
    counts = jnp.sum(expert[:, None] == jnp.arange(N_EXPERTS, dtype=jnp.int32)[None, :], axis=0).astype(jnp.int32)
    padded = (counts + R - 1) // R * R
    p_end = jnp.cumsum(padded)
    p_start = p_end - padded
    n_blk = -(-n_assign // R) + N_EXPERTS
    blk_start = jnp.arange(n_blk, dtype=jnp.int32) * R
    blk_valid = (blk_start < p_end[-1]).astype(jnp.int32)
    last_valid = jnp.maximum(p_end[-1] // R - 1, 0)
    blk_expert = jnp.minimum(jnp.searchsorted(p_end, blk_start, side='right'), N_EXPERTS - 1).astype(jnp.int32)
    blk_expert = jnp.where(blk_valid > 0, blk_expert, blk_expert[last_valid])
    blk_first = jnp.concatenate([jnp.ones((1,), jnp.int32), (blk_expert[1:] != blk_expert[:-1]).astype(jnp.int32)])
    blk_rows = jnp.clip((p_start + counts)[blk_expert] - blk_start, 0, R).astype(jnp.int32) * blk_valid
    n_slots = n_blk * R
    n_pad = n_slots - n_assign
    pad_end = jnp.cumsum(padded - counts)
    pad_key = jnp.sum(jnp.arange(n_pad, dtype=jnp.int32)[:, None] >= pad_end[None, :], axis=1).astype(jnp.int32)
    keys = jnp.concatenate([expert, pad_key])
    ids = jnp.concatenate([jnp.arange(n_assign, dtype=jnp.int32), jnp.full((n_pad,), -1, jnp.int32)])
    gates = jnp.concatenate([gate.reshape(-1), jnp.zeros((n_pad,), F32)])
    _, slot_id, slot_gate = lax.sort((keys, ids, gates), num_keys=1, is_stable=True)
    slot_tok = jnp.maximum(slot_id, 0) // TOP_K
    _, slot_of = lax.sort((slot_id, jnp.arange(n_slots, dtype=jnp.int32)), num_keys=1)
    pos = slot_of[n_pad:]

    ys = _moe_experts(xn_packed, slot_tok, slot_gate, blk_expert, blk_first, blk_rows, w_gate, w_up, w_down, layer)
    pos2 = pos.reshape(T, TOP_K)
    rows_of = lambda idx: ys.at[idx].get(mode="promise_in_bounds")
    return _moe_combine(h, rows_of(pos2[:, 0]), rows_of(pos2[:, 1]))


def _mem_kv(mem, g, w_kv, layer, k_gain):
    M = mem.shape[0]
    (mn,) = _rmsnorm(mem, g, (BF16,))
    kv = _matmul(mn, w_kv, layer)
    k = kv[:, :MEM_WIDTH].reshape(M, MEM_HEADS, MEM_HEAD_DIM)
    k = k * lax.rsqrt(jnp.mean(k * k, axis=-1, keepdims=True) + NORM_EPS) * k_gain
    return k.reshape(M, MEM_WIDTH).astype(BF16), kv[:, MEM_WIDTH:].astype(BF16)


def _pad_cols(w, n):
    return jnp.pad(w, ((0, 0), (0, n - w.shape[1])))


def _a_layer_weights(w_in, mu, w_vdown, mu_vres):
    D = w_in.shape[0]
    c = np.cumsum([RWKV_WIDTH, RWKV_WIDTH, RWKV_WIDTH, LORA_DECAY, LORA_AAA, LORA_GATE, MEM_WIDTH])
    rkv, xw, xa, xg, qm = (slice(0, c[2]), slice(c[2], c[3]), slice(c[3], c[4]), slice(c[4], c[5]), slice(c[5], c[6]))
    if w_vdown is None:
        w_vdown = jnp.zeros((D, LORA_MV), w_in.dtype)
        mu_vres = jnp.zeros((LORA_MV,), mu.dtype)
    tail = A_IN_PAD - A_OFF_XV
    w = jnp.concatenate([w_in[:, rkv], w_in[:, qm], _pad_cols(w_in[:, xg], A_XG_PAD), w_in[:, xw], w_in[:, xa],
                         _pad_cols(w_vdown, tail)], axis=1)
    m = jnp.concatenate([mu[rkv], mu[qm], jnp.pad(mu[xg], (0, A_XG_PAD - LORA_GATE)), mu[xw], mu[xa],
                         jnp.pad(mu_vres, (0, tail - LORA_MV))])
    return w.astype(BF16), m


def _rwkv_layer(h, mk, mv, mq_gain, ln1, w_in, mu, w_vdown, mu_vres, y_first, w0, w_up, a0, a_up, g_up,
                v0, v_up, k_k, k_a, r_k, lnx_g, lnx_b, w_out, layer):
    (xn,) = _rmsnorm(h, ln1, (BF16,))
    w_pad, mu_pad = _a_layer_weights(w_in, mu, w_vdown, mu_vres)
    y = _matmul_shift(xn, w_pad, mu_pad)
    pad_rows = lambda w, n: jnp.pad(w, ((0, n - w.shape[0]), (0, 0))).astype(BF16)
    has_vres = y_first is not None
    vecs = jnp.stack([w0, a0, v0 if has_vres else jnp.zeros_like(w0), k_k, k_a, r_k.reshape(-1), lnx_g, lnx_b])
    mix_out = _rwkv_mix(y, y_first, w_up.astype(BF16), a_up.astype(BF16), pad_rows(g_up, A_XG_PAD),
                        pad_rows(v_up, A_XV_PAD) if has_vres else None, vecs)
    mem_out = _mem_attend(y, A_OFF_QM // MEM_WIDTH, mk, mv, mq_gain)
    return _matmul2_res(mix_out, mem_out, w_out, layer, h), y


def _dilated_layer(h, mk, mv, mq_gain, ln1, w_in, q_gain, ks, vs, w_out, layer):
    (xn,) = _rmsnorm(h, ln1, (BF16,))
    y = _matmul(xn, w_in, layer)
    parts = [_dilated_group(y, gi, dil, ks, vs, q_gain) for gi, (_, dil) in enumerate(DIL_PATTERNS)]
    att = _merge_groups(parts)
    mem_out = _mem_attend(y, DIL_Q_WIDTH // MEM_WIDTH, mk, mv, mq_gain)
    return _matmul2_res(att, mem_out, w_out, layer, h)


def _shared_kv(h, g, w_kv, k_gain):
    T = h.shape[0]
    (xn,) = _rmsnorm(h, g, (BF16,))
    kv = _matmul(xn, w_kv[None], 0)
    k = kv[:, :KV_WIDTH].reshape(T, DIL_HEADS, ATT_HEAD_DIM)
    k = k * lax.rsqrt(jnp.mean(k * k, axis=-1, keepdims=True) + NORM_EPS) * k_gain
    return k.reshape(T, KV_WIDTH), kv[:, KV_WIDTH:]


def kernel(x, mem, a_ln1, a_w_in, a_w_vdown, a_mu, a_mu_vres, a_w0, a_w_up, a_a0, a_a_up, a_g_up, a_v0, a_v_up, a_k_k, a_k_a, a_r_k, a_lnx_g, a_lnx_b, a_w_out, b_ln1, b_w_in, b_q_norm, b_w_out, s_kv_norm, s_w_kv, s_k_norm, m_norm, m_w_kv, m_q_norm, m_k_norm, moe_ln, moe_router_group, moe_router_expert, moe_w_gate, moe_w_up, moe_w_down):
    Bsz, S, D = x.shape
    assert Bsz == 1 and S % (max(d for _, d in DIL_PATTERNS) * ATT_BLOCK) == 0
    depth = moe_ln.shape[0]
    h = x.reshape(S, D)
    mem2 = mem.reshape(mem.shape[1], D)
    y_first = None
    shared = None
    for l in range(depth):
        mk, mv = _mem_kv(mem2, m_norm[l], m_w_kv, l, m_k_norm[l])
        if l < N_A:
            i = l
            has_vres = i > 0
            h, y_i = _rwkv_layer(
                h, mk, mv, m_q_norm[l], a_ln1[i], a_w_in[i], a_mu[i],
                a_w_vdown[i - 1] if has_vres else None, a_mu_vres[i - 1] if has_vres else None,
                y_first, a_w0[i], a_w_up[i], a_a0[i], a_a_up[i], a_g_up[i],
                a_v0[i - 1] if has_vres else None, a_v_up[i - 1] if has_vres else None,
                a_k_k[i], a_k_a[i], a_r_k[i],
                a_lnx_g[i], a_lnx_b[i], a_w_out, i)
            if i == 0:
                y_first = y_i
        else:
            j = l - N_A
            if shared is None:
                shared = _shared_kv(h, s_kv_norm, s_w_kv, s_k_norm)
            h = _dilated_layer(h, mk, mv, m_q_norm[l], b_ln1[j], b_w_in, b_q_norm[j], shared[0], shared[1], b_w_out, j)
        h = _hier_moe(h, moe_ln[l], moe_router_group[l], moe_router_expert[l], moe_w_gate, moe_w_up, moe_w_down, l)
    return h.reshape(Bsz, S, D)
```

```python
import functools

import jax
import jax.numpy as jnp
import numpy as np
from jax import lax
from jax.experimental import pallas as pl
from jax.experimental.pallas import tpu as pltpu

F32 = jnp.float32
BF16 = jnp.bfloat16

D_MODEL = 4096
RWKV_HEAD = 64
RWKV_WIDTH = 3072
LORA_DECAY = 128
LORA_AAA = 128
LORA_MV = 96
LORA_GATE = 480
LNX_EPS = 64e-5
MEM_HEADS = 4
MEM_WIDTH = 1024
MEM_HEAD_DIM = 256
ATT_HEAD_DIM = 128
DIL_PATTERNS = ((128, 1), (512, 4), (2048, 16))
N_DIL_GROUPS = 3
DIL_HEADS = 8
DIL_Q_WIDTH = 3072
KV_WIDTH = 1024
ATT_BLOCK = 128
N_GROUPS = 4
EXPERTS_PER_GROUP = 8
N_EXPERTS = 32
TOP_K = 2
D_FF = 384
NORM_EPS = 1e-6
N_A = 2

LANES = 128
SUBLANES = 8
VMEM_LIMIT_BYTES = 56 * 1024 * 1024

A_OFF_R = 0
A_OFF_K = RWKV_WIDTH
A_OFF_V = 2 * RWKV_WIDTH
A_OFF_QM = 3 * RWKV_WIDTH
A_OFF_XG = A_OFF_QM + MEM_WIDTH
A_XG_PAD = 512
A_OFF_XW = A_OFF_XG + A_XG_PAD
A_OFF_XA = A_OFF_XW + LORA_DECAY
A_OFF_XV = A_OFF_XA + LORA_AAA
A_XV_PAD = 128
A_IN_PAD = 11264

SCAN_CHUNK = 64
SCAN_TBLOCK = 128
MOE_ROWS = 256

_NN = (((1,), (0,)), ((), ()))
_NT = (((1,), (1,)), ((), ()))


def _cparams(*sem):
    return pltpu.CompilerParams(dimension_semantics=sem, vmem_limit_bytes=VMEM_LIMIT_BYTES)


def _rms_kernel(x_ref, g_ref, *o_refs):
    x = x_ref[...]
    y = x * lax.rsqrt(jnp.mean(x * x, axis=-1, keepdims=True) + NORM_EPS) * g_ref[...]
    for o_ref in o_refs:
        o_ref[...] = y.astype(o_ref.dtype)


def _rmsnorm(x, g, dtypes, tm=256):
    T, D = x.shape
    tm = min(tm, T)
    outs = pl.pallas_call(
        _rms_kernel,
        grid=(T // tm,),
        in_specs=[pl.BlockSpec((tm, D), lambda i: (i, 0)), pl.BlockSpec((1, D), lambda i: (0, 0))],
        out_specs=[pl.BlockSpec((tm, D), lambda i: (i, 0)) for _ in dtypes],
        out_shape=[jax.ShapeDtypeStruct((T, D), dt) for dt in dtypes],
        compiler_params=_cparams("parallel"),
        name="rmsnorm",
    )(x, g.reshape(1, D))
    return outs


def _pack_halves(x16):
    bits = lax.bitcast_convert_type(x16.astype(F32), jnp.uint32)
    half = bits.shape[1] // 2
    return (bits[:, :half] >> 16) | bits[:, half:]


def _unpack_halves(words):
    lo = [lax.bitcast_convert_type(w << 16, F32) for w in words]
    hi = [lax.bitcast_convert_type(w & jnp.uint32(0xFFFF0000), F32) for w in words]
    return jnp.concatenate(lo + hi, axis=1).astype(BF16)


def _rms_router_kernel(x_ref, g_ref, w_ref, logit_ref, packed_ref):
    x = x_ref[...]
    y = x * lax.rsqrt(jnp.mean(x * x, axis=-1, keepdims=True) + NORM_EPS) * g_ref[...]
    yh = y.astype(BF16)
    yl = (y - yh.astype(F32)).astype(BF16)
    w = w_ref[...]
    wh = w.astype(BF16)
    wl = (w - wh.astype(F32)).astype(BF16)
    d = lambda a, b: jnp.dot(a, b, preferred_element_type=F32)
    logit_ref[...] = d(yh, wh) + (d(yh, wl) + d(yl, wh))
    packed = _pack_halves(yh)
    tm = packed.shape[0]
    n_tiles = packed.shape[1] // LANES
    for c in range(n_tiles):
        packed_ref[pl.ds(c, tm, stride=n_tiles), :] = packed[:, c * LANES:(c + 1) * LANES]


def _rms_router(x, g, w_r, tm=512):
    T, D = x.shape
    N = w_r.shape[1]
    n_tiles = D // 2 // LANES
    return pl.pallas_call(
        _rms_router_kernel,
        grid=(T // tm,),
        in_specs=[pl.BlockSpec((tm, D), lambda i: (i, 0)), pl.BlockSpec((1, D), lambda i: (0, 0)),
                  pl.BlockSpec((D, N), lambda i: (0, 0))],
        out_specs=[pl.BlockSpec((tm, N), lambda i: (i, 0)), pl.BlockSpec((tm * n_tiles, LANES), lambda i: (i, 0))],
        out_shape=[jax.ShapeDtypeStruct((T, N), F32), jax.ShapeDtypeStruct((T * n_tiles, LANES), jnp.uint32)],
        compiler_params=_cparams("parallel"),
        name="rms_router",
    )(x, g.reshape(1, D), w_r)


def _mm_kernel(x_ref, w_ref, o_ref):
    o_ref[...] = jnp.dot(x_ref[...], w_ref[...].astype(BF16), preferred_element_type=F32)


def _mm_shift_kernel(x_ref, xp_ref, w_ref, mu_ref, o_ref):
    w = w_ref[...]
    y = jnp.dot(x_ref[...], w, preferred_element_type=F32)
    tail = jnp.dot(xp_ref[...], w, preferred_element_type=F32)
    last = jnp.where(pl.program_id(0) > 0, tail[tail.shape[0] - 1:, :], 0.0)
    row = lax.broadcasted_iota(jnp.int32, y.shape, 0)
    y_prev = jnp.where(row == 0, last, pltpu.roll(y, 1, axis=0))
    o_ref[...] = y + mu_ref[...] * (y_prev - y)


def _matmul_shift(x, w, mu, tm=1024, tn=512):
    M, K = x.shape
    N = w.shape[1]
    tm = min(tm, M)
    tp = 2 * SUBLANES
    assert M % tm == 0 and N % tn == 0 and tm % tp == 0
    return pl.pallas_call(
        _mm_shift_kernel,
        grid=(M // tm, N // tn),
        in_specs=[pl.BlockSpec((tm, K), lambda i, j: (i, 0)),
                  pl.BlockSpec((tp, K), lambda i, j: (jnp.maximum(i * (tm // tp) - 1, 0), 0)),
                  pl.BlockSpec((K, tn), lambda i, j: (0, j)),
                  pl.BlockSpec((1, tn), lambda i, j: (0, j))],
        out_specs=pl.BlockSpec((tm, tn), lambda i, j: (i, j)),
        out_shape=jax.ShapeDtypeStruct((M, N), F32),
        compiler_params=_cparams("parallel", "parallel"),
        name="matmul_shift",
    )(x, x, w, mu.reshape(1, N))


def _mm2_res_kernel(x1_ref, x2_ref, w1_ref, w2_ref, r_ref, o_ref):
    acc = jnp.dot(x1_ref[...], w1_ref[...].astype(BF16), preferred_element_type=F32)
    o_ref[...] = r_ref[...] + (acc + jnp.dot(x2_ref[...], w2_ref[...].astype(BF16), preferred_element_type=F32))


def _matmul2_res(x1, x2, w, layer, res, tm=1024, tn=512):
    M, K1 = x1.shape
    K2 = x2.shape[1]
    N = w.shape[2]
    tm = min(tm, M)
    assert M % tm == 0 and N % tn == 0 and K1 % K2 == 0 and w.shape[1] == K1 + K2
    return pl.pallas_call(
        _mm2_res_kernel,
        grid=(M // tm, N // tn),
        in_specs=[pl.BlockSpec((tm, K1), lambda i, j: (i, 0)), pl.BlockSpec((tm, K2), lambda i, j: (i, 0)),
                  pl.BlockSpec((None, K1, tn), lambda i, j: (layer, 0, j)),
                  pl.BlockSpec((None, K2, tn), lambda i, j: (layer, K1 // K2, j)),
                  pl.BlockSpec((tm, tn), lambda i, j: (i, j))],
        out_specs=pl.BlockSpec((tm, tn), lambda i, j: (i, j)),
        out_shape=jax.ShapeDtypeStruct((M, N), F32),
        compiler_params=_cparams("parallel", "parallel"),
        name="matmul_out",
    )(x1, x2, w, w, res)


def _matmul(x, w, layer, tm=1024, tn=512):
    M, K = x.shape
    N = w.shape[2]
    tm = min(tm, M)
    tn = min(tn, N)
    assert M % tm == 0 and N % tn == 0
    return pl.pallas_call(
        _mm_kernel,
        grid=(M // tm, N // tn),
        in_specs=[pl.BlockSpec((tm, K), lambda i, j: (i, 0)), pl.BlockSpec((None, K, tn), lambda i, j: (layer, 0, j))],
        out_specs=pl.BlockSpec((tm, tn), lambda i, j: (i, j)),
        out_shape=jax.ShapeDtypeStruct((M, N), F32),
        compiler_params=_cparams("parallel", "parallel"),
        name="matmul",
    )(x, w)


def _split(x):
    hi = x.astype(BF16)
    lo = (x - hi.astype(F32)).astype(BF16)
    return hi, lo


def _dot1(a, b, dims=_NN):
    return lax.dot_general(a.astype(BF16), b.astype(BF16), dims, preferred_element_type=F32)


def _round_robin(gens):
    outs = [None] * len(gens)
    active = list(range(len(gens)))
    while active:
        for i in list(active):
            try:
                next(gens[i])
            except StopIteration as stop:
                outs[i] = stop.value
                active.remove(i)
    return outs


def _rwkv_mix_kernel(*refs, n_chunks, n_pairs, has_vres):
    if has_vres:
        (r_ref, k_ref, v_ref, xg_ref, xw_ref, xa_ref, xv_ref, vf_ref, wup_ref, aup_ref, gup_ref, vup_ref, vec_ref,
         o_ref, s_ref, lw_s, k_s, v_s, a_s, b_s, y_s) = refs
    else:
        (r_ref, k_ref, v_ref, xg_ref, xw_ref, xa_ref, wup_ref, aup_ref, gup_ref, vec_ref,
         o_ref, s_ref, lw_s, k_s, v_s, a_s, b_s, y_s) = refs
    C = SCAN_CHUNK
    H2 = 2 * C

    @pl.when(pl.program_id(0) == 0)
    def _():
        s_ref[...] = jnp.zeros_like(s_ref)

    lane = lax.broadcasted_iota(jnp.int32, (C, LANES), 1)
    head0 = lane < RWKV_HEAD
    trow = lax.broadcasted_iota(jnp.int32, (C, LANES), 0)
    tcol = lane & (C - 1)
    lane2 = lax.broadcasted_iota(jnp.int32, (H2, LANES), 1)
    row2 = lax.broadcasted_iota(jnp.int32, (H2, LANES), 0)
    blockdiag = (row2 < C) == (lane2 < RWKV_HEAD)
    head_ones = jnp.where(blockdiag, 1.0, 0.0).astype(BF16)
    tri_r = lax.broadcasted_iota(jnp.int32, (C, C), 0)
    tri_c = lax.broadcasted_iota(jnp.int32, (C, C), 1)
    ltri = jnp.where(tri_r >= tri_c, 1.0, 0.0).astype(BF16)
    arow = lax.broadcasted_iota(jnp.int32, (H2, 2 * H2), 0)
    acol = lax.broadcasted_iota(jnp.int32, (H2, 2 * H2), 1)
    a_t = arow & (C - 1)
    a_j = acol & (C - 1)
    amask = a_t + jnp.where(arow < C, 0, 1) > a_j
    n_levels = C.bit_length() - 1
    eye2 = jnp.where(trow == tcol, 1.0, 0.0)
    dd = lambda x, y: jnp.dot(x, y, preferred_element_type=F32)
    tiles = [slice(p * LANES, (p + 1) * LANES) for p in range(n_pairs)]
    vec = lambda i, ls: vec_ref[i:i + 1, ls]

    def head_sum(x):
        hi, lo = _split(x)
        return dd(hi, head_ones) + dd(lo, head_ones)

    def level_mask(lvl):
        tb_, jb_ = trow >> lvl, tcol >> lvl
        return (tb_ - jb_) * (tb_ & 1) == 1

    def stack2(x):
        z = jnp.zeros_like(x)
        return jnp.concatenate([jnp.where(head0, x, z), jnp.where(head0, z, x)], axis=0)

    xw = jnp.tanh(xw_ref[...]).astype(BF16)
    xa = xa_ref[...].astype(BF16)
    if has_vres:
        xv = xv_ref[...].astype(BF16)
    for ls in tiles:
        z = vec(_V_W0, ls) + dd(xw, wup_ref[:, ls])
        lw_s[:, ls] = jax.nn.sigmoid(z) * (-float(np.exp(-0.5)))
        a = jax.nn.sigmoid(vec(_V_A0, ls) + dd(xa, aup_ref[:, ls]))
        k = k_ref[:, ls]
        kk = k * vec(_V_KK, ls)
        kk = kk * lax.rsqrt(jnp.maximum(head_sum(kk * kk), 1e-24))
        v = v_ref[:, ls]
        if has_vres:
            mix = jax.nn.sigmoid(vec(_V_V0, ls) + dd(xv, vup_ref[:, ls]))
            v = v + (vf_ref[:, ls] - v) * mix
        k_s[:, ls] = k * (1.0 + (a - 1.0) * vec(_V_KA, ls))
        v_s[:, ls] = v
        a_s[:, ls] = -kk
        b_s[:, ls] = kk * a

    def pair_chunk(r, lw, k, v, a, b, s):
        l1 = lw.astype(BF16)
        rem = lw - l1.astype(F32)
        l2 = rem.astype(BF16)
        l3 = (rem - l2.astype(F32)).astype(BF16)
        cum = dd(ltri, l1) + (dd(ltri, l2) + dd(ltri, l3))
        yield
        p_inc = jnp.exp(cum)
        p_exc = jnp.exp(cum - lw)
        p_inv = jnp.exp(-cum)
        at = a * p_exc
        rt = r * p_inc
        bt = b * p_inv
        kt = k * p_inv
        p_tot = p_inc[C - 1:C, :]

        x_ar = jnp.concatenate([at, rt], axis=0).astype(BF16)
        w_bk = jnp.concatenate([stack2(bt), stack2(kt)], axis=0)
        amat = jnp.where(amask, _dot1(x_ar, w_bk, _NT), 0.0)
        xs = _dot1(x_ar, s, _NT)
        yield

        n_ab = amat[:C, :H2]
        a_ak = amat[:C, H2:]
        a_r = amat[C:, :]
        v2 = stack2(v)
        rhs = xs[:C] + _dot1(a_ak, v2)
        yield

        tinv = eye2 + jnp.where(level_mask(0), n_ab, 0.0)
        for lvl in range(1, n_levels):
            nt = _dot1(jnp.where(level_mask(lvl), n_ab, 0.0), stack2(tinv))
            yield
            tinv = tinv + _dot1(tinv, stack2(nt))
            yield
        u = _dot1(tinv, stack2(rhs))
        yield

        uv = jnp.concatenate([stack2(u), v2], axis=0)
        y = xs[C:] + _dot1(a_r, uv)
        yield

        uv_t = jnp.concatenate([u, v], axis=0).T
        bk = jnp.concatenate([bt, kt], axis=0)
        ds = _dot1(uv_t, bk)
        return y, (s + jnp.where(blockdiag, ds, 0.0)) * p_tot

    def chunk(ci, carry):
        sl = pl.ds(pl.multiple_of(ci * C, C), C)
        ins = [tuple(ref[sl, ls] for ref in (r_ref, lw_s, k_s, v_s, a_s, b_s)) + (s_ref[p],)
               for p, ls in enumerate(tiles)]
        outs = _round_robin([pair_chunk(*args) for args in ins])
        for p, ls in enumerate(tiles):
            y_s[sl, ls] = outs[p][0]
            s_ref[p] = outs[p][1]
        return carry

    lax.fori_loop(0, n_chunks, chunk, 0)

    xg = jax.nn.sigmoid(xg_ref[...]).astype(BF16)
    inv_n = 1.0 / RWKV_HEAD
    for ls in tiles:
        y = y_s[:, ls]
        d = y - head_sum(y) * inv_n
        var = head_sum(d * d) * inv_n
        yn = d * lax.rsqrt(var + LNX_EPS) * vec(_V_LNG, ls) + vec(_V_LNB, ls)
        bonus = head_sum(r_ref[:, ls] * k_s[:, ls] * vec(_V_RK, ls)) * v_s[:, ls]
        o_ref[:, ls] = ((yn + bonus) * dd(xg, gup_ref[:, ls])).astype(BF16)


_V_W0, _V_A0, _V_V0, _V_KK, _V_KA, _V_RK, _V_LNG, _V_LNB = range(8)


def _rwkv_mix(y, y_first, w_up, a_up, g_up, v_up, vecs):
    T = y.shape[0]
    W = RWKV_WIDTH
    tb = min(SCAN_TBLOCK, T)
    has_vres = y_first is not None
    assert T % tb == 0 and tb % SCAN_CHUNK == 0
    wide = lambda blk: pl.BlockSpec((tb, W), lambda t: (t, blk))
    narrow = lambda off, width: pl.BlockSpec((tb, width), lambda t: (t, off // width))
    full = lambda arr: pl.BlockSpec(arr.shape, lambda t: (0, 0))
    in_specs = [wide(A_OFF_R // W), wide(A_OFF_K // W), wide(A_OFF_V // W),
                narrow(A_OFF_XG, A_XG_PAD), narrow(A_OFF_XW, LORA_DECAY), narrow(A_OFF_XA, LORA_AAA)]
    args = [y, y, y, y, y, y]
    if has_vres:
        in_specs += [narrow(A_OFF_XV, A_XV_PAD), wide(A_OFF_V // W)]
        args += [y, y_first]
    weights = [w_up, a_up, g_up] + ([v_up] if has_vres else []) + [vecs]
    in_specs += [full(w) for w in weights]
    args += weights
    return pl.pallas_call(
        functools.partial(_rwkv_mix_kernel, n_chunks=tb // SCAN_CHUNK, n_pairs=W // LANES, has_vres=has_vres),
        grid=(T // tb,),
        in_specs=in_specs,
        out_specs=pl.BlockSpec((tb, W), lambda t: (t, 0)),
        out_shape=jax.ShapeDtypeStruct((T, W), BF16),
        scratch_shapes=[pltpu.VMEM((W // LANES, LANES, LANES), F32)] + [pltpu.VMEM((tb, W), F32)] * 6,
        compiler_params=_cparams("arbitrary"),
        name="rwkv7_mix",
    )(*args)


def _mem_attn_kernel(q_ref, k_ref, v_ref, g_ref, o_ref):
    scale = MEM_HEAD_DIM ** -0.5
    for h in range(MEM_HEADS):
        hs = slice(h * MEM_HEAD_DIM, (h + 1) * MEM_HEAD_DIM)
        q = q_ref[:, hs]
        qn = q * lax.rsqrt(jnp.mean(q * q, axis=-1, keepdims=True) + NORM_EPS) * g_ref[...]
        s = lax.dot_general(qn.astype(BF16), k_ref[:, hs], _NT, preferred_element_type=F32) * scale
        m = jnp.max(s, axis=-1, keepdims=True)
        p = jnp.exp(s - m)
        l = jnp.sum(p, axis=-1, keepdims=True)
        o = jnp.dot(p.astype(BF16), v_ref[:, hs], preferred_element_type=F32)
        o_ref[:, hs] = (o / l).astype(o_ref.dtype)


def _mem_attend(y, col_block, mk, mv, q_gain, tm=512):
    T = y.shape[0]
    M = mk.shape[0]
    tm = min(tm, T)
    return pl.pallas_call(
        _mem_attn_kernel,
        grid=(T // tm,),
        in_specs=[
            pl.BlockSpec((tm, MEM_WIDTH), lambda i: (i, col_block)),
            pl.BlockSpec((M, MEM_WIDTH), lambda i: (0, 0)),
            pl.BlockSpec((M, MEM_WIDTH), lambda i: (0, 0)),
            pl.BlockSpec((1, MEM_HEAD_DIM), lambda i: (0, 0)),
        ],
        out_specs=pl.BlockSpec((tm, MEM_WIDTH), lambda i: (i, 0)),
        out_shape=jax.ShapeDtypeStruct((T, MEM_WIDTH), BF16),
        compiler_params=_cparams("parallel"),
        name="mem_attend",
    )(y, mk, mv, q_gain.reshape(1, MEM_HEAD_DIM))


def _dil_attn_kernel(sl_ref, q_ref, kp_ref, kc_ref, vp_ref, vc_ref, g_ref, o_ref, lse_ref, *, dil, heads):
    n = pl.program_id(0)
    h0 = pl.program_id(1) * heads
    B = ATT_BLOCK
    n_back = B
    scale = ATT_HEAD_DIM ** -0.5
    qi = lax.broadcasted_iota(jnp.int32, (B, 2 * B), 0)
    ki = lax.broadcasted_iota(jnp.int32, (B, 2 * B), 1)
    j = B + qi - ki
    valid = (j >= 0) & (j <= n_back) & ((ki >= B) | (n > 0))
    dist = (j * dil).astype(F32)
    for c in range(dil):
        rows = pl.ds(c, B, stride=dil) if dil > 1 else slice(None)
        for h in range(heads):
            hs = slice(h * ATT_HEAD_DIM, (h + 1) * ATT_HEAD_DIM)
            q = q_ref[rows, hs]
            qn = q * lax.rsqrt(jnp.mean(q * q, axis=-1, keepdims=True) + NORM_EPS) * g_ref[...]
            kcat = jnp.concatenate([kp_ref[rows, hs], kc_ref[rows, hs]], axis=0).astype(BF16)
            vcat = jnp.concatenate([vp_ref[rows, hs], vc_ref[rows, hs]], axis=0).astype(BF16)
            s = lax.dot_general(qn.astype(BF16), kcat, _NT, preferred_element_type=F32) * scale
            s = jnp.where(valid, s - sl_ref[h0 + h] * dist, -1e30)
            m = jnp.max(s, axis=-1, keepdims=True)
            p = jnp.exp(s - m)
            l = jnp.sum(p, axis=-1, keepdims=True)
            o = jnp.dot(p.astype(BF16), vcat, preferred_element_type=F32)
            o_ref[rows, hs] = o / l
            lse_ref[rows, hs] = jnp.broadcast_to(m + jnp.log(l), (B, ATT_HEAD_DIM))


def _dilated_group(y, gi, dil, k, v, q_gain):
    T, W = y.shape
    n_heads_total = N_DIL_GROUPS * DIL_HEADS
    slopes = jnp.asarray([2.0 ** (-8.0 * (gi * DIL_HEADS + h + 1) / n_heads_total) for h in range(DIL_HEADS)], F32)
    rows = ATT_BLOCK * dil
    cols = KV_WIDTH if dil == 1 else ATT_HEAD_DIM
    ncb = KV_WIDTH // cols
    blk = (rows, cols)
    cur = lambda n, hb: (n, hb)
    prev = lambda n, hb: (jnp.maximum(n - 1, 0), hb)
    o, lse = pl.pallas_call(
        functools.partial(_dil_attn_kernel, dil=dil, heads=cols // ATT_HEAD_DIM),
        grid=(T // rows, ncb),
        in_specs=[
            pl.BlockSpec(memory_space=pltpu.SMEM),
            pl.BlockSpec(blk, lambda n, hb: (n, gi * ncb + hb)),
            pl.BlockSpec(blk, prev), pl.BlockSpec(blk, cur),
            pl.BlockSpec(blk, prev), pl.BlockSpec(blk, cur),
            pl.BlockSpec((1, ATT_HEAD_DIM), lambda n, hb: (0, 0)),
        ],
        out_specs=[pl.BlockSpec(blk, cur), pl.BlockSpec(blk, cur)],
        out_shape=[jax.ShapeDtypeStruct((T, KV_WIDTH), F32)] * 2,
        compiler_params=_cparams("parallel", "parallel"),
        name=f"dilated_attn_d{dil}",
    )(slopes, y, k, k, v, v, q_gain.reshape(1, ATT_HEAD_DIM))
    return o, lse


def _merge_kernel(o0, l0, o1, l1, o2, l2, out_ref):
    a0, a1, a2 = l0[...], l1[...], l2[...]
    m = jnp.maximum(jnp.maximum(a0, a1), a2)
    w0, w1, w2 = jnp.exp(a0 - m), jnp.exp(a1 - m), jnp.exp(a2 - m)
    out_ref[...] = ((w0 * o0[...] + w1 * o1[...] + w2 * o2[...]) / (w0 + w1 + w2)).astype(out_ref.dtype)


def _merge_groups(parts, tm=512):
    T, W = parts[0][0].shape
    tm = min(tm, T)
    spec = pl.BlockSpec((tm, W), lambda i: (i, 0))
    flat = [t for pr in parts for t in pr]
    return pl.pallas_call(
        _merge_kernel,
        grid=(T // tm,),
        in_specs=[spec] * 6,
        out_specs=spec,
        out_shape=jax.ShapeDtypeStruct((T, W), BF16),
        compiler_params=_cparams("parallel"),
        name="dilated_merge",
    )(*flat)


def _moe_up_kernel(tok_ref, be_ref, bf_ref, bv_ref, x_hbm, wg_ref, wu_ref, o_ref, xbuf, sem, x16, wt16):
    i = pl.program_id(0)
    n_blk = pl.num_programs(0)
    R = MOE_ROWS
    n_tiles = xbuf.shape[1] // R
    unroll = SUBLANES

    def row_copy(tok, slot, r):
        src = x_hbm.at[pl.ds(pl.multiple_of(tok * n_tiles, n_tiles), n_tiles)]
        dst = xbuf.at[slot, pl.ds(pl.multiple_of(r * n_tiles, n_tiles), n_tiles)]
        return pltpu.make_async_copy(src, dst, sem.at[slot])

    def trips(blk):
        return (bv_ref[blk] + (unroll - 1)) // unroll

    def start_gather(blk, slot):
        def body(g, carry):
            for u in range(unroll):
                r = g * unroll + u
                row_copy(tok_ref[blk * R + r], slot, r).start()
            return carry
        lax.fori_loop(0, trips(blk), body, 0)

    @pl.when(i == 0)
    def _():
        xbuf[...] = jnp.zeros_like(xbuf)
        start_gather(0, 0)

    nxt = jnp.minimum(i + 1, n_blk - 1)

    @pl.when((i + 1 < n_blk) & (bv_ref[nxt] > 0))
    def _():
        start_gather(i + 1, (i + 1) & 1)

    @pl.when(bf_ref[i] > 0)
    def _():
        F = wg_ref.shape[1]
        wt16[:F, :] = wg_ref[...].T.astype(BF16)
        wt16[F:, :] = wu_ref[...].T.astype(BF16)

    def unpack(slot):
        def wait_rows(g, carry):
            for u in range(unroll):
                row_copy(0, slot, g * unroll + u).wait()
            return carry
        lax.fori_loop(0, trips(i), wait_rows, 0)
        x16[...] = _unpack_halves([xbuf[slot, pl.ds(c, R, stride=n_tiles), :] for c in range(n_tiles)])

    for slot in range(2):
        pl.when((bv_ref[i] > 0) & ((i & 1) == slot))(functools.partial(unpack, slot))

    @pl.when(bv_ref[i] > 0)
    def _():
        F = o_ref.shape[0]
        ht = lax.dot_general(wt16[...], x16[...], _NT, preferred_element_type=F32)
        hg = ht[:F]
        o_ref[...] = (hg * jax.nn.sigmoid(hg) * ht[F:]).astype(BF16)

    @pl.when(bv_ref[i] == 0)
    def _():
        o_ref[...] = jnp.zeros_like(o_ref)


def _moe_down_kernel(be_ref, bf_ref, bv_ref, h_ref, wd_ref, g_ref, o_ref, wdt16):
    i = pl.program_id(0)

    @pl.when(bf_ref[i] > 0)
    def _():
        wdt16[...] = wd_ref[...].T.astype(BF16)

    @pl.when(bv_ref[i] > 0)
    def _():
        yt = jnp.dot(wdt16[...], h_ref[...], preferred_element_type=F32) * g_ref[...]
        o_ref[...] = _pack_halves(yt.T.astype(BF16))

    @pl.when(bv_ref[i] == 0)
    def _():
        o_ref[...] = jnp.zeros_like(o_ref)


def _moe_experts(xn, slot_tok, slot_gate, blk_expert, blk_first, blk_valid, w_gate, w_up, w_down, layer):
    D = w_gate.shape[2]
    NR = slot_tok.shape[0]
    R = MOE_ROWS
    n_blk = NR // R
    F = w_gate.shape[3]
    hb = pl.pallas_call(
        _moe_up_kernel,
        grid_spec=pltpu.PrefetchScalarGridSpec(
            num_scalar_prefetch=4,
            grid=(n_blk,),
            in_specs=[pl.BlockSpec(memory_space=pl.ANY),
                      pl.BlockSpec((None, None, D, F), lambda i, tok, be, bf, bv: (layer, be[i], 0, 0)),
                      pl.BlockSpec((None, None, D, F), lambda i, tok, be, bf, bv: (layer, be[i], 0, 0))],
            out_specs=pl.BlockSpec((F, R), lambda i, tok, be, bf, bv: (0, i)),
            scratch_shapes=[pltpu.VMEM((2, R * (D // 2 // LANES), LANES), jnp.uint32), pltpu.SemaphoreType.DMA((2,)),
                            pltpu.VMEM((R, D), BF16), pltpu.VMEM((2 * F, D), BF16)],
        ),
        out_shape=jax.ShapeDtypeStruct((F, NR), BF16),
        compiler_params=_cparams("arbitrary"),
        name="moe_up",
    )(slot_tok, blk_expert, blk_first, blk_valid, xn, w_gate, w_up)
    row = lambda i, be, bf, bv: (i, 0)
    wsel = lambda i, be, bf, bv: (layer, be[i], 0, 0)
    return pl.pallas_call(
        _moe_down_kernel,
        grid_spec=pltpu.PrefetchScalarGridSpec(
            num_scalar_prefetch=3,
            grid=(n_blk,),
            in_specs=[pl.BlockSpec((F, R), lambda i, be, bf, bv: (0, i)), pl.BlockSpec((None, None, F, D), wsel),
                      pl.BlockSpec((1, R), lambda i, be, bf, bv: (0, i))],
            out_specs=pl.BlockSpec((R, D // 2), row),
            scratch_shapes=[pltpu.VMEM((D, F), BF16)],
        ),
        out_shape=jax.ShapeDtypeStruct((NR, D // 2), jnp.uint32),
        compiler_params=_cparams("arbitrary"),
        name="moe_down",
    )(blk_expert, blk_first, blk_valid, hb, w_down, slot_gate.reshape(1, NR))


def _moe_combine_kernel(h_ref, a_ref, b_ref, o_ref):
    half = a_ref.shape[1]
    a = a_ref[...]
    b = b_ref[...]
    top = jnp.uint32(0xFFFF0000)
    f32 = lambda w: lax.bitcast_convert_type(w, F32)
    o_ref[:, :half] = h_ref[:, :half] + (f32(a << 16) + f32(b << 16))
    o_ref[:, half:] = h_ref[:, half:] + (f32(a & top) + f32(b & top))


def _moe_combine(h, ya, yb, tm=256):
    T, D = h.shape
    packed = pl.BlockSpec((tm, D // 2), lambda i: (i, 0))
    return pl.pallas_call(
        _moe_combine_kernel,
        grid=(T // tm,),
        in_specs=[pl.BlockSpec((tm, D), lambda i: (i, 0)), packed, packed],
        out_specs=pl.BlockSpec((tm, D), lambda i: (i, 0)),
        out_shape=jax.ShapeDtypeStruct((T, D), F32),
        compiler_params=_cparams("parallel"),
        name="moe_combine",
    )(h, ya, yb)


def _hier_moe(h, ln_g, w_rg, w_re, w_gate, w_up, w_down, layer):
    T, D = h.shape
    R = MOE_ROWS
    n_rout = N_GROUPS + N_EXPERTS
    w_r = jnp.pad(jnp.concatenate([w_rg, w_re], axis=1), ((0, 0), (0, LANES - n_rout)))
    logits, xn_packed = _rms_router(h, ln_g, w_r)
    gl = logits[:, :N_GROUPS]
    g_sel = jnp.argmax(gl, axis=-1)
    p_group = jnp.take_along_axis(jax.nn.softmax(gl, axis=-1), g_sel[:, None], axis=-1)
    el = logits[:, N_GROUPS:n_rout].reshape(T, N_GROUPS, EXPERTS_PER_GROUP)
    el = jnp.take_along_axis(el, g_sel[:, None, None], axis=1)[:, 0]
    top_v, top_i = lax.top_k(el, TOP_K)
    gate = p_group * jax.nn.softmax(top_v, axis=-1)
    expert = (g_sel[:, None] * EXPERTS_PER_GROUP + top_i).reshape(-1).astype(jnp.int32)
    n_assign = T * TOP_K
    counts = jnp.sum(expert[:, None] == jnp.arange(N_EXPERTS, dtype=jnp.int32)[None, :], axis=0).astype(jnp.int32)
    padded = (counts + R - 1) // R * R
    p_end = jnp.cumsum(padded)
    p_start = p_end - padded
    n_blk = -(-n_assign // R) + N_EXPERTS
    blk_start = jnp.arange(n_blk, dtype=jnp.int32) * R
    blk_valid = (blk_start < p_end[-1]).astype(jnp.int32)
    last_valid = jnp.maximum(p_end[-1] // R - 1, 0)
    blk_expert = jnp.minimum(jnp.searchsorted(p_end, blk_start, side='right'), N_EXPERTS - 1).astype(jnp.int32)
    blk_expert = jnp.where(blk_valid > 0, blk_expert, blk_expert[last_valid])
    blk_first = jnp.concatenate([jnp.ones((1,), jnp.int32), (blk_expert[1:] != blk_expert[:-1]).astype(jnp.int32)])
    blk_rows = jnp.clip((p_start + counts)[blk_expert] - blk_start, 0, R).astype(jnp.int32) * blk_valid
    n_slots = n_blk * R
    n_pad = n_slots - n_assign
    pad_end = jnp.cumsum(padded - counts)
    pad_key = jnp.sum(jnp.arange(n_pad, dtype=jnp.int32)[:, None] >= pad_end[None, :], axis=1).astype(jnp.int32)
    keys = jnp.concatenate([expert, pad_key])
    ids = jnp.concatenate([jnp.arange(n_assign, dtype=jnp.int32), jnp.full((n_pad,), -1, jnp.int32)])
    gates = jnp.concatenate([gate.reshape(-1), jnp.zeros((n_pad,), F32)])
    _, slot_id, slot_gate = lax.sort((keys, ids, gates), num_keys=1, is_stable=True)
    slot_tok = jnp.maximum(slot_id, 0) // TOP_K
    _, slot_of = lax.sort((slot_id, jnp.arange(n_slots, dtype=jnp.int32)), num_keys=1)
    pos = slot_of[n_pad:]

    ys = _moe_experts(xn_packed, slot_tok, slot_gate, blk_expert, blk_first, blk_rows, w_gate, w_up, w_down, layer)
    pos2 = pos.reshape(T, TOP_K)
    rows_of = lambda idx: ys.at[idx].get(mode="promise_in_bounds")
    return _moe_combine(h, rows_of(pos2[:, 0]), rows_of(pos2[:, 1]))


def _mem_kv(mem, g, w_kv, layer, k_gain):
    M = mem.shape[0]
    (mn,) = _rmsnorm(mem, g, (BF16,))
    kv = _matmul(mn, w_kv, layer)
    k = kv[:, :MEM_WIDTH].reshape(M, MEM_HEADS, MEM_HEAD_DIM)
    k = k * lax.rsqrt(jnp.mean(k * k, axis=-1, keepdims=True) + NORM_EPS) * k_gain
    return k.reshape(M, MEM_WIDTH).astype(BF16), kv[:, MEM_WIDTH:].astype(BF16)


def _pad_cols(w, n):
    return jnp.pad(w, ((0, 0), (0, n - w.shape[1])))


def _a_layer_weights(w_in, mu, w_vdown, mu_vres):
    D = w_in.shape[0]
    c = np.cumsum([RWKV_WIDTH, RWKV_WIDTH, RWKV_WIDTH, LORA_DECAY, LORA_AAA, LORA_GATE, MEM_WIDTH])
    rkv, xw, xa, xg, qm = (slice(0, c[2]), slice(c[2], c[3]), slice(c[3], c[4]), slice(c[4], c[5]), slice(c[5], c[6]))
    if w_vdown is None:
        w_vdown = jnp.zeros((D, LORA_MV), w_in.dtype)
        mu_vres = jnp.zeros((LORA_MV,), mu.dtype)
    tail = A_IN_PAD - A_OFF_XV
    w = jnp.concatenate([w_in[:, rkv], w_in[:, qm], _pad_cols(w_in[:, xg], A_XG_PAD), w_in[:, xw], w_in[:, xa],
                         _pad_cols(w_vdown, tail)], axis=1)
    m = jnp.concatenate([mu[rkv], mu[qm], jnp.pad(mu[xg], (0, A_XG_PAD - LORA_GATE)), mu[xw], mu[xa],
                         jnp.pad(mu_vres, (0, tail - LORA_MV))])
    return w.astype(BF16), m


def _rwkv_layer(h, mk, mv, mq_gain, ln1, w_in, mu, w_vdown, mu_vres, y_first, w0, w_up, a0, a_up, g_up,
                v0, v_up, k_k, k_a, r_k, lnx_g, lnx_b, w_out, layer):
    (xn,) = _rmsnorm(h, ln1, (BF16,))
    w_pad, mu_pad = _a_layer_weights(w_in, mu, w_vdown, mu_vres)
    y = _matmul_shift(xn, w_pad, mu_pad)
    pad_rows = lambda w, n: jnp.pad(w, ((0, n - w.shape[0]), (0, 0))).astype(BF16)
    has_vres = y_first is not None
    vecs = jnp.stack([w0, a0, v0 if has_vres else jnp.zeros_like(w0), k_k, k_a, r_k.reshape(-1), lnx_g, lnx_b])
    mix_out = _rwkv_mix(y, y_first, w_up.astype(BF16), a_up.astype(BF16), pad_rows(g_up, A_XG_PAD),
                        pad_rows(v_up, A_XV_PAD) if has_vres else None, vecs)
    mem_out = _mem_attend(y, A_OFF_QM // MEM_WIDTH, mk, mv, mq_gain)
    return _matmul2_res(mix_out, mem_out, w_out, layer, h), y


def _dilated_layer(h, mk, mv, mq_gain, ln1, w_in, q_gain, ks, vs, w_out, layer):
    (xn,) = _rmsnorm(h, ln1, (BF16,))
    y = _matmul(xn, w_in, layer)
    parts = [_dilated_group(y, gi, dil, ks, vs, q_gain) for gi, (_, dil) in enumerate(DIL_PATTERNS)]
    att = _merge_groups(parts)
    mem_out = _mem_attend(y, DIL_Q_WIDTH // MEM_WIDTH, mk, mv, mq_gain)
    return _matmul2_res(att, mem_out, w_out, layer, h)


def _shared_kv(h, g, w_kv, k_gain):
    T = h.shape[0]
    (xn,) = _rmsnorm(h, g, (BF16,))
    kv = _matmul(xn, w_kv[None], 0)
    k = kv[:, :KV_WIDTH].reshape(T, DIL_HEADS, ATT_HEAD_DIM)
    k = k * lax.rsqrt(jnp.mean(k * k, axis=-1, keepdims=True) + NORM_EPS) * k_gain
    return k.reshape(T, KV_WIDTH), kv[:, KV_WIDTH:]


def kernel(x, mem, a_ln1, a_w_in, a_w_vdown, a_mu, a_mu_vres, a_w0, a_w_up, a_a0, a_a_up, a_g_up, a_v0, a_v_up, a_k_k, a_k_a, a_r_k, a_lnx_g, a_lnx_b, a_w_out, b_ln1, b_w_in, b_q_norm, b_w_out, s_kv_norm, s_w_kv, s_k_norm, m_norm, m_w_kv, m_q_norm, m_k_norm, moe_ln, moe_router_group, moe_router_expert, moe_w_gate, moe_w_up, moe_w_down):
    Bsz, S, D = x.shape
    assert Bsz == 1 and S % (max(d for _, d in DIL_PATTERNS) * ATT_BLOCK) == 0
    depth = moe_ln.shape[0]
    h = x.reshape(S, D)
    mem2 = mem.reshape(mem.shape[1], D)
    y_first = None
    shared = None
    for l in range(depth):
        mk, mv = _mem_kv(mem2, m_norm[l], m_w_kv, l, m_k_norm[l])
        if l < N_A:
            i = l
            has_vres = i > 0
            h, y_i = _rwkv_layer(
                h, mk, mv, m_q_norm[l], a_ln1[i], a_w_in[i], a_mu[i],
                a_w_vdown[i - 1] if has_vres else None, a_mu_vres[i - 1] if has_vres else None,
                y_first, a_w0[i], a_w_up[i], a_a0[i], a_a_up[i], a_g_up[i],
                a_v0[i - 1] if has_vres else None, a_v_up[i - 1] if has_vres else None,
                a_k_k[i], a_k_a[i], a_r_k[i],
                a_lnx_g[i], a_lnx_b[i], a_w_out, i)
            if i == 0:
                y_first = y_i
        else:
            j = l - N_A
            if shared is None:
                shared = _shared_kv(h, s_kv_norm, s_w_kv, s_k_norm)
            h = _dilated_layer(h, mk, mv, m_q_norm[l], b_ln1[j], b_w_in, b_q_norm[j], shared[0], shared[1], b_w_out, j)
        h = _hier_moe(h, moe_ln[l], moe_router_group[l], moe_router_expert[l], moe_w_gate, moe_w_up, moe_w_down, l)
    return h.reshape(Bsz, S, D)
```

```python
import functools

import jax
import jax.numpy as jnp
import numpy as np
from jax import lax
from jax.experimental import pallas as pl
from jax.experimental.pallas import tpu as pltpu

F32 = jnp.float32
BF16 = jnp.bfloat16

D_MODEL = 4096
RWKV_HEAD = 64
RWKV_WIDTH = 3072
LORA_DECAY = 128
LORA_AAA = 128
LORA_MV = 96
LORA_GATE = 480
LNX_EPS = 64e-5
MEM_HEADS = 4
MEM_WIDTH = 1024
MEM_HEAD_DIM = 256
ATT_HEAD_DIM = 128
DIL_PATTERNS = ((128, 1), (512, 4), (2048, 16))
N_DIL_GROUPS = 3
DIL_HEADS = 8
DIL_Q_WIDTH = 3072
KV_WIDTH = 1024
ATT_BLOCK = 128
N_GROUPS = 4
EXPERTS_PER_GROUP = 8
N_EXPERTS = 32
TOP_K = 2
D_FF = 384
NORM_EPS = 1e-6
N_A = 2

LANES = 128
SUBLANES = 8
VMEM_LIMIT_BYTES = 56 * 1024 * 1024

A_MAIN = 3 * RWKV_WIDTH
S_OFF_QM = 0
S_OFF_XG = MEM_WIDTH
A_XG_PAD = 512
S_OFF_XW = S_OFF_XG + A_XG_PAD
S_OFF_XA = S_OFF_XW + LORA_DECAY
S_OFF_XV = S_OFF_XA + LORA_AAA
A_XV_PAD = 128
A_SIDE = 2048

SCAN_CHUNK = 64
SCAN_TBLOCK = 128
MOE_ROWS = 256

_NN = (((1,), (0,)), ((), ()))
_NT = (((1,), (1,)), ((), ()))


def _cparams(*sem):
    return pltpu.CompilerParams(dimension_semantics=sem, vmem_limit_bytes=VMEM_LIMIT_BYTES)


def _rms_kernel(x_ref, g_ref, *o_refs):
    x = x_ref[...]
    y = x * lax.rsqrt(jnp.mean(x * x, axis=-1, keepdims=True) + NORM_EPS) * g_ref[...]
    for o_ref in o_refs:
        o_ref[...] = y.astype(o_ref.dtype)


def _rmsnorm(x, g, dtypes, tm=256):
    T, D = x.shape
    tm = min(tm, T)
    outs = pl.pallas_call(
        _rms_kernel,
        grid=(T // tm,),
        in_specs=[pl.BlockSpec((tm, D), lambda i: (i, 0)), pl.BlockSpec((1, D), lambda i: (0, 0))],
        out_specs=[pl.BlockSpec((tm, D), lambda i: (i, 0)) for _ in dtypes],
        out_shape=[jax.ShapeDtypeStruct((T, D), dt) for dt in dtypes],
        compiler_params=_cparams("parallel"),
        name="rmsnorm",
    )(x, g.reshape(1, D))
    return outs


def _pack_halves(x16):
    bits = lax.bitcast_convert_type(x16.astype(F32), jnp.uint32)
    half = bits.shape[1] // 2
    return (bits[:, :half] >> 16) | bits[:, half:]


def _unpack_halves(words):
    lo = [lax.bitcast_convert_type(w << 16, F32) for w in words]
    hi = [lax.bitcast_convert_type(w & jnp.uint32(0xFFFF0000), F32) for w in words]
    return jnp.concatenate(lo + hi, axis=1).astype(BF16)


def _rms_router_kernel(x_ref, g_ref, w_ref, logit_ref, packed_ref):
    x = x_ref[...]
    y = x * lax.rsqrt(jnp.mean(x * x, axis=-1, keepdims=True) + NORM_EPS) * g_ref[...]
    yh = y.astype(BF16)
    yl = (y - yh.astype(F32)).astype(BF16)
    w = w_ref[...]
    wh = w.astype(BF16)
    wl = (w - wh.astype(F32)).astype(BF16)
    d = lambda a, b: jnp.dot(a, b, preferred_element_type=F32)
    logit_ref[...] = d(yh, wh) + (d(yh, wl) + d(yl, wh))
    packed = _pack_halves(yh)
    tm = packed.shape[0]
    n_tiles = packed.shape[1] // LANES
    for c in range(n_tiles):
        packed_ref[pl.ds(c, tm, stride=n_tiles), :] = packed[:, c * LANES:(c + 1) * LANES]


def _rms_router(x, g, w_r, tm=512):
    T, D = x.shape
    N = w_r.shape[1]
    n_tiles = D // 2 // LANES
    return pl.pallas_call(
        _rms_router_kernel,
        grid=(T // tm,),
        in_specs=[pl.BlockSpec((tm, D), lambda i: (i, 0)), pl.BlockSpec((1, D), lambda i: (0, 0)),
                  pl.BlockSpec((D, N), lambda i: (0, 0))],
        out_specs=[pl.BlockSpec((tm, N), lambda i: (i, 0)), pl.BlockSpec((tm * n_tiles, LANES), lambda i: (i, 0))],
        out_shape=[jax.ShapeDtypeStruct((T, N), F32), jax.ShapeDtypeStruct((T * n_tiles, LANES), jnp.uint32)],
        compiler_params=_cparams("parallel"),
        name="rms_router",
    )(x, g.reshape(1, D), w_r)


def _mm_kernel(x_ref, w_ref, o_ref):
    o_ref[...] = jnp.dot(x_ref[...], w_ref[...].astype(BF16), preferred_element_type=F32)


def _mm_shift_kernel(x_ref, xp_ref, w_ref, mu_ref, o_ref):
    w = w_ref[...].astype(BF16)
    y = jnp.dot(x_ref[...], w, preferred_element_type=F32)
    tail = jnp.dot(xp_ref[...], w, preferred_element_type=F32)
    last = jnp.where(pl.program_id(0) > 0, tail[tail.shape[0] - 1:, :], 0.0)
    row = lax.broadcasted_iota(jnp.int32, y.shape, 0)
    y_prev = jnp.where(row == 0, last, pltpu.roll(y, 1, axis=0))
    o_ref[...] = y + mu_ref[...] * (y_prev - y)


def _matmul_shift(x, w, layer, mu, tm=1024, tn=512):
    M, K = x.shape
    N = mu.shape[0]
    tm = min(tm, M)
    tp = 2 * SUBLANES
    assert M % tm == 0 and N % tn == 0 and tm % tp == 0
    return pl.pallas_call(
        _mm_shift_kernel,
        grid=(M // tm, N // tn),
        in_specs=[pl.BlockSpec((tm, K), lambda i, j: (i, 0)),
                  pl.BlockSpec((tp, K), lambda i, j: (jnp.maximum(i * (tm // tp) - 1, 0), 0)),
                  pl.BlockSpec((None, K, tn), lambda i, j: (layer, 0, j)),
                  pl.BlockSpec((1, tn), lambda i, j: (0, j))],
        out_specs=pl.BlockSpec((tm, tn), lambda i, j: (i, j)),
        out_shape=jax.ShapeDtypeStruct((M, N), F32),
        compiler_params=_cparams("parallel", "parallel"),
        name="matmul_shift",
    )(x, x, w, mu.reshape(1, N))


def _mm2_res_kernel(x1_ref, x2_ref, w1_ref, w2_ref, r_ref, o_ref):
    acc = jnp.dot(x1_ref[...], w1_ref[...].astype(BF16), preferred_element_type=F32)
    o_ref[...] = r_ref[...] + (acc + jnp.dot(x2_ref[...], w2_ref[...].astype(BF16), preferred_element_type=F32))


def _matmul2_res(x1, x2, w, layer, res, tm=1024, tn=512):
    M, K1 = x1.shape
    K2 = x2.shape[1]
    N = w.shape[2]
    tm = min(tm, M)
    assert M % tm == 0 and N % tn == 0 and K1 % K2 == 0 and w.shape[1] == K1 + K2
    return pl.pallas_call(
        _mm2_res_kernel,
        grid=(M // tm, N // tn),
        in_specs=[pl.BlockSpec((tm, K1), lambda i, j: (i, 0)), pl.BlockSpec((tm, K2), lambda i, j: (i, 0)),
                  pl.BlockSpec((None, K1, tn), lambda i, j: (layer, 0, j)),
                  pl.BlockSpec((None, K2, tn), lambda i, j: (layer, K1 // K2, j)),
                  pl.BlockSpec((tm, tn), lambda i, j: (i, j))],
        out_specs=pl.BlockSpec((tm, tn), lambda i, j: (i, j)),
        out_shape=jax.ShapeDtypeStruct((M, N), F32),
        compiler_params=_cparams("parallel", "parallel"),
        name="matmul_out",
    )(x1, x2, w, w, res)


def _matmul(x, w, layer, tm=1024, tn=512):
    M, K = x.shape
    N = w.shape[2]
    tm = min(tm, M)
    tn = min(tn, N)
    assert M % tm == 0 and N % tn == 0
    return pl.pallas_call(
        _mm_kernel,
        grid=(M // tm, N // tn),
        in_specs=[pl.BlockSpec((tm, K), lambda i, j: (i, 0)), pl.BlockSpec((None, K, tn), lambda i, j: (layer, 0, j))],
        out_specs=pl.BlockSpec((tm, tn), lambda i, j: (i, j)),
        out_shape=jax.ShapeDtypeStruct((M, N), F32),
        compiler_params=_cparams("parallel", "parallel"),
        name="matmul",
    )(x, w)


def _split(x):
    hi = x.astype(BF16)
    lo = (x - hi.astype(F32)).astype(BF16)
    return hi, lo


def _dot1(a, b, dims=_NN):
    return lax.dot_general(a.astype(BF16), b.astype(BF16), dims, preferred_element_type=F32)


def _round_robin(gens):
    outs = [None] * len(gens)
    active = list(range(len(gens)))
    while active:
        for i in list(active):
            try:
                next(gens[i])
            except StopIteration as stop:
                outs[i] = stop.value
                active.remove(i)
    return outs


def _rwkv_mix_kernel(*refs, n_chunks, n_pairs, has_vres):
    if has_vres:
        (r_ref, k_ref, v_ref, xg_ref, xw_ref, xa_ref, xv_ref, vf_ref, wup_ref, aup_ref, gup_ref, vup_ref, vec_ref,
         o_ref, s_ref, lw_s, k_s, v_s, a_s, b_s, y_s) = refs
    else:
        (r_ref, k_ref, v_ref, xg_ref, xw_ref, xa_ref, wup_ref, aup_ref, gup_ref, vec_ref,
         o_ref, s_ref, lw_s, k_s, v_s, a_s, b_s, y_s) = refs
    C = SCAN_CHUNK
    H2 = 2 * C

    @pl.when(pl.program_id(0) == 0)
    def _():
        s_ref[...] = jnp.zeros_like(s_ref)

    lane = lax.broadcasted_iota(jnp.int32, (C, LANES), 1)
    head0 = lane < RWKV_HEAD
    trow = lax.broadcasted_iota(jnp.int32, (C, LANES), 0)
    tcol = lane & (C - 1)
    lane2 = lax.broadcasted_iota(jnp.int32, (H2, LANES), 1)
    row2 = lax.broadcasted_iota(jnp.int32, (H2, LANES), 0)
    blockdiag = (row2 < C) == (lane2 < RWKV_HEAD)
    head_ones = jnp.where(blockdiag, 1.0, 0.0).astype(BF16)
    tri_r = lax.broadcasted_iota(jnp.int32, (C, C), 0)
    tri_c = lax.broadcasted_iota(jnp.int32, (C, C), 1)
    ltri = jnp.where(tri_r >= tri_c, 1.0, 0.0).astype(BF16)
    arow = lax.broadcasted_iota(jnp.int32, (H2, 2 * H2), 0)
    acol = lax.broadcasted_iota(jnp.int32, (H2, 2 * H2), 1)
    a_t = arow & (C - 1)
    a_j = acol & (C - 1)
    amask = a_t + jnp.where(arow < C, 0, 1) > a_j
    n_levels = C.bit_length() - 1
    eye2 = jnp.where(trow == tcol, 1.0, 0.0)
    dd = lambda x, y: jnp.dot(x, y, preferred_element_type=F32)
    tiles = [slice(p * LANES, (p + 1) * LANES) for p in range(n_pairs)]
    vec = lambda i, ls: vec_ref[i:i + 1, ls]

    def head_sum(x):
        hi, lo = _split(x)
        return dd(hi, head_ones) + dd(lo, head_ones)

    def level_mask(lvl):
        tb_, jb_ = trow >> lvl, tcol >> lvl
        return (tb_ - jb_) * (tb_ & 1) == 1

    def stack2(x):
        z = jnp.zeros_like(x)
        return jnp.concatenate([jnp.where(head0, x, z), jnp.where(head0, z, x)], axis=0)

    xw = jnp.tanh(xw_ref[...]).astype(BF16)
    xa = xa_ref[...].astype(BF16)
    if has_vres:
        xv = xv_ref[...].astype(BF16)
    for ls in tiles:
        z = vec(_V_W0, ls) + dd(xw, wup_ref[:, ls])
        lw_s[:, ls] = jax.nn.sigmoid(z) * (-float(np.exp(-0.5)))
        a = jax.nn.sigmoid(vec(_V_A0, ls) + dd(xa, aup_ref[:, ls]))
        k = k_ref[:, ls]
        kk = k * vec(_V_KK, ls)
        kk = kk * lax.rsqrt(jnp.maximum(head_sum(kk * kk), 1e-24))
        v = v_ref[:, ls]
        if has_vres:
            mix = jax.nn.sigmoid(vec(_V_V0, ls) + dd(xv, vup_ref[:, ls]))
            v = v + (vf_ref[:, ls] - v) * mix
        k_s[:, ls] = k * (1.0 + (a - 1.0) * vec(_V_KA, ls))
        v_s[:, ls] = v
        a_s[:, ls] = -kk
        b_s[:, ls] = kk * a

    def pair_chunk(r, lw, k, v, a, b, s):
        l1 = lw.astype(BF16)
        rem = lw - l1.astype(F32)
        l2 = rem.astype(BF16)
        l3 = (rem - l2.astype(F32)).astype(BF16)
        cum = dd(ltri, l1) + (dd(ltri, l2) + dd(ltri, l3))
        yield
        p_inc = jnp.exp(cum)
        p_exc = jnp.exp(cum - lw)
        p_inv = jnp.exp(-cum)
        at = a * p_exc
        rt = r * p_inc
        bt = b * p_inv
        kt = k * p_inv
        p_tot = p_inc[C - 1:C, :]

        x_ar = jnp.concatenate([at, rt], axis=0).astype(BF16)
        w_bk = jnp.concatenate([stack2(bt), stack2(kt)], axis=0)
        amat = jnp.where(amask, _dot1(x_ar, w_bk, _NT), 0.0)
        xs = _dot1(x_ar, s, _NT)
        yield

        n_ab = amat[:C, :H2]
        a_ak = amat[:C, H2:]
        a_r = amat[C:, :]
        v2 = stack2(v)
        rhs = xs[:C] + _dot1(a_ak, v2)
        yield

        tinv = eye2 + jnp.where(level_mask(0), n_ab, 0.0)
        for lvl in range(1, n_levels):
            nt = _dot1(jnp.where(level_mask(lvl), n_ab, 0.0), stack2(tinv))
            yield
            tinv = tinv + _dot1(tinv, stack2(nt))
            yield
        u = _dot1(tinv, stack2(rhs))
        yield

        uv = jnp.concatenate([stack2(u), v2], axis=0)
        y = xs[C:] + _dot1(a_r, uv)
        yield

        uv_t = jnp.concatenate([u, v], axis=0).T
        bk = jnp.concatenate([bt, kt], axis=0)
        ds = _dot1(uv_t, bk)
        return y, (s + jnp.where(blockdiag, ds, 0.0)) * p_tot

    def chunk(ci, carry):
        sl = pl.ds(pl.multiple_of(ci * C, C), C)
        ins = [tuple(ref[sl, ls] for ref in (r_ref, lw_s, k_s, v_s, a_s, b_s)) + (s_ref[p],)
               for p, ls in enumerate(tiles)]
        outs = _round_robin([pair_chunk(*args) for args in ins])
        for p, ls in enumerate(tiles):
            y_s[sl, ls] = outs[p][0]
            s_ref[p] = outs[p][1]
        return carry

    lax.fori_loop(0, n_chunks, chunk, 0)

    xg = jax.nn.sigmoid(xg_ref[...]).astype(BF16)
    inv_n = 1.0 / RWKV_HEAD
    for ls in tiles:
        y = y_s[:, ls]
        d = y - head_sum(y) * inv_n
        var = head_sum(d * d) * inv_n
        yn = d * lax.rsqrt(var + LNX_EPS) * vec(_V_LNG, ls) + vec(_V_LNB, ls)
        bonus = head_sum(r_ref[:, ls] * k_s[:, ls] * vec(_V_RK, ls)) * v_s[:, ls]
        o_ref[:, ls] = ((yn + bonus) * dd(xg, gup_ref[:, ls])).astype(BF16)


_V_W0, _V_A0, _V_V0, _V_KK, _V_KA, _V_RK, _V_LNG, _V_LNB = range(8)


def _rwkv_mix(y, y_side, y_first, w_up, a_up, g_up, v_up, vecs):
    T = y.shape[0]
    W = RWKV_WIDTH
    tb = min(SCAN_TBLOCK, T)
    has_vres = y_first is not None
    assert T % tb == 0 and tb % SCAN_CHUNK == 0
    wide = lambda blk: pl.BlockSpec((tb, W), lambda t: (t, blk))
    narrow = lambda off, width: pl.BlockSpec((tb, width), lambda t: (t, off // width))
    full = lambda arr: pl.BlockSpec(arr.shape, lambda t: (0, 0))
    in_specs = [wide(0), wide(1), wide(2),
                narrow(S_OFF_XG, A_XG_PAD), narrow(S_OFF_XW, LORA_DECAY), narrow(S_OFF_XA, LORA_AAA)]
    args = [y, y, y, y_side, y_side, y_side]
    if has_vres:
        in_specs += [narrow(S_OFF_XV, A_XV_PAD), wide(2)]
        args += [y_side, y_first]
    weights = [w_up, a_up, g_up] + ([v_up] if has_vres else []) + [vecs]
    in_specs += [full(w) for w in weights]
    args += weights
    return pl.pallas_call(
        functools.partial(_rwkv_mix_kernel, n_chunks=tb // SCAN_CHUNK, n_pairs=W // LANES, has_vres=has_vres),
        grid=(T // tb,),
        in_specs=in_specs,
        out_specs=pl.BlockSpec((tb, W), lambda t: (t, 0)),
        out_shape=jax.ShapeDtypeStruct((T, W), BF16),
        scratch_shapes=[pltpu.VMEM((W // LANES, LANES, LANES), F32)] + [pltpu.VMEM((tb, W), F32)] * 6,
        compiler_params=_cparams("arbitrary"),
        name="rwkv7_mix",
    )(*args)


def _mem_attn_kernel(q_ref, k_ref, v_ref, g_ref, o_ref):
    scale = MEM_HEAD_DIM ** -0.5
    for h in range(MEM_HEADS):
        hs = slice(h * MEM_HEAD_DIM, (h + 1) * MEM_HEAD_DIM)
        q = q_ref[:, hs]
        qn = q * lax.rsqrt(jnp.mean(q * q, axis=-1, keepdims=True) + NORM_EPS) * g_ref[...]
        s = lax.dot_general(qn.astype(BF16), k_ref[:, hs], _NT, preferred_element_type=F32) * scale
        m = jnp.max(s, axis=-1, keepdims=True)
        p = jnp.exp(s - m)
        l = jnp.sum(p, axis=-1, keepdims=True)
        o = jnp.dot(p.astype(BF16), v_ref[:, hs], preferred_element_type=F32)
        o_ref[:, hs] = (o / l).astype(o_ref.dtype)


def _mem_attend(y, col_block, mk, mv, q_gain, tm=512):
    T = y.shape[0]
    M = mk.shape[0]
    tm = min(tm, T)
    return pl.pallas_call(
        _mem_attn_kernel,
        grid=(T // tm,),
        in_specs=[
            pl.BlockSpec((tm, MEM_WIDTH), lambda i: (i, col_block)),
            pl.BlockSpec((M, MEM_WIDTH), lambda i: (0, 0)),
            pl.BlockSpec((M, MEM_WIDTH), lambda i: (0, 0)),
            pl.BlockSpec((1, MEM_HEAD_DIM), lambda i: (0, 0)),
        ],
        out_specs=pl.BlockSpec((tm, MEM_WIDTH), lambda i: (i, 0)),
        out_shape=jax.ShapeDtypeStruct((T, MEM_WIDTH), BF16),
        compiler_params=_cparams("parallel"),
        name="mem_attend",
    )(y, mk, mv, q_gain.reshape(1, MEM_HEAD_DIM))


def _dil_attn_kernel(sl_ref, q_ref, kp_ref, kc_ref, vp_ref, vc_ref, g_ref, o_ref, lse_ref, *, dil, heads):
    n = pl.program_id(0)
    h0 = pl.program_id(1) * heads
    B = ATT_BLOCK
    n_back = B
    scale = ATT_HEAD_DIM ** -0.5
    qi = lax.broadcasted_iota(jnp.int32, (B, 2 * B), 0)
    ki = lax.broadcasted_iota(jnp.int32, (B, 2 * B), 1)
    j = B + qi - ki
    valid = (j >= 0) & (j <= n_back) & ((ki >= B) | (n > 0))
    dist = (j * dil).astype(F32)
    for c in range(dil):
        rows = pl.ds(c, B, stride=dil) if dil > 1 else slice(None)
        for h in range(heads):
            hs = slice(h * ATT_HEAD_DIM, (h + 1) * ATT_HEAD_DIM)
            q = q_ref[rows, hs]
            qn = q * lax.rsqrt(jnp.mean(q * q, axis=-1, keepdims=True) + NORM_EPS) * g_ref[...]
            kcat = jnp.concatenate([kp_ref[rows, hs], kc_ref[rows, hs]], axis=0).astype(BF16)
            vcat = jnp.concatenate([vp_ref[rows, hs], vc_ref[rows, hs]], axis=0).astype(BF16)
            s = lax.dot_general(qn.astype(BF16), kcat, _NT, preferred_element_type=F32) * scale
            s = jnp.where(valid, s - sl_ref[h0 + h] * dist, -1e30)
            m = jnp.max(s, axis=-1, keepdims=True)
            p = jnp.exp(s - m)
            l = jnp.sum(p, axis=-1, keepdims=True)
            o = jnp.dot(p.astype(BF16), vcat, preferred_element_type=F32)
            o_ref[rows, hs] = o / l
            lse_ref[rows, hs] = jnp.broadcast_to(m + jnp.log(l), (B, ATT_HEAD_DIM))


def _dilated_group(y, gi, dil, k, v, q_gain):
    T, W = y.shape
    n_heads_total = N_DIL_GROUPS * DIL_HEADS
    slopes = jnp.asarray([2.0 ** (-8.0 * (gi * DIL_HEADS + h + 1) / n_heads_total) for h in range(DIL_HEADS)], F32)
    rows = ATT_BLOCK * dil
    cols = KV_WIDTH if dil == 1 else ATT_HEAD_DIM
    ncb = KV_WIDTH // cols
    blk = (rows, cols)
    cur = lambda n, hb: (n, hb)
    prev = lambda n, hb: (jnp.maximum(n - 1, 0), hb)
    o, lse = pl.pallas_call(
        functools.partial(_dil_attn_kernel, dil=dil, heads=cols // ATT_HEAD_DIM),
        grid=(T // rows, ncb),
        in_specs=[
            pl.BlockSpec(memory_space=pltpu.SMEM),
            pl.BlockSpec(blk, lambda n, hb: (n, gi * ncb + hb)),
            pl.BlockSpec(blk, prev), pl.BlockSpec(blk, cur),
            pl.BlockSpec(blk, prev), pl.BlockSpec(blk, cur),
            pl.BlockSpec((1, ATT_HEAD_DIM), lambda n, hb: (0, 0)),
        ],
        out_specs=[pl.BlockSpec(blk, cur), pl.BlockSpec(blk, cur)],
        out_shape=[jax.ShapeDtypeStruct((T, KV_WIDTH), F32)] * 2,
        compiler_params=_cparams("parallel", "parallel"),
        name=f"dilated_attn_d{dil}",
    )(slopes, y, k, k, v, v, q_gain.reshape(1, ATT_HEAD_DIM))
    return o, lse


def _merge_kernel(o0, l0, o1, l1, o2, l2, out_ref):
    a0, a1, a2 = l0[...], l1[...], l2[...]
    m = jnp.maximum(jnp.maximum(a0, a1), a2)
    w0, w1, w2 = jnp.exp(a0 - m), jnp.exp(a1 - m), jnp.exp(a2 - m)
    out_ref[...] = ((w0 * o0[...] + w1 * o1[...] + w2 * o2[...]) / (w0 + w1 + w2)).astype(out_ref.dtype)


def _merge_groups(parts, tm=512):
    T, W = parts[0][0].shape
    tm = min(tm, T)
    spec = pl.BlockSpec((tm, W), lambda i: (i, 0))
    flat = [t for pr in parts for t in pr]
    return pl.pallas_call(
        _merge_kernel,
        grid=(T // tm,),
        in_specs=[spec] * 6,
        out_specs=spec,
        out_shape=jax.ShapeDtypeStruct((T, W), BF16),
        compiler_params=_cparams("parallel"),
        name="dilated_merge",
    )(*flat)


def _moe_up_kernel(tok_ref, be_ref, bf_ref, bv_ref, x_hbm, wg_ref, wu_ref, o_ref, xbuf, sem, x16, wt16):
    i = pl.program_id(0)
    n_blk = pl.num_programs(0)
    R = MOE_ROWS
    n_tiles = xbuf.shape[1] // R
    unroll = SUBLANES

    def row_copy(tok, slot, r):
        src = x_hbm.at[pl.ds(pl.multiple_of(tok * n_tiles, n_tiles), n_tiles)]
        dst = xbuf.at[slot, pl.ds(pl.multiple_of(r * n_tiles, n_tiles), n_tiles)]
        return pltpu.make_async_copy(src, dst, sem.at[slot])

    def trips(blk):
        return (bv_ref[blk] + (unroll - 1)) // unroll

    def start_gather(blk, slot):
        def body(g, carry):
            for u in range(unroll):
                r = g * unroll + u
                row_copy(tok_ref[blk * R + r], slot, r).start()
            return carry
        lax.fori_loop(0, trips(blk), body, 0)

    @pl.when(i == 0)
    def _():
        xbuf[...] = jnp.zeros_like(xbuf)
        start_gather(0, 0)

    nxt = jnp.minimum(i + 1, n_blk - 1)

    @pl.when((i + 1 < n_blk) & (bv_ref[nxt] > 0))
    def _():
        start_gather(i + 1, (i + 1) & 1)

    @pl.when(bf_ref[i] > 0)
    def _():
        F = wg_ref.shape[1]
        wt16[:F, :] = wg_ref[...].T.astype(BF16)
        wt16[F:, :] = wu_ref[...].T.astype(BF16)

    def unpack(slot):
        def wait_rows(g, carry):
            for u in range(unroll):
                row_copy(0, slot, g * unroll + u).wait()
            return carry
        lax.fori_loop(0, trips(i), wait_rows, 0)
        x16[...] = _unpack_halves([xbuf[slot, pl.ds(c, R, stride=n_tiles), :] for c in range(n_tiles)])

    for slot in range(2):
        pl.when((bv_ref[i] > 0) & ((i & 1) == slot))(functools.partial(unpack, slot))

    @pl.when(bv_ref[i] > 0)
    def _():
        F = o_ref.shape[0]
        ht = lax.dot_general(wt16[...], x16[...], _NT, preferred_element_type=F32)
        hg = ht[:F]
        o_ref[...] = (hg * jax.nn.sigmoid(hg) * ht[F:]).astype(BF16)

    @pl.when(bv_ref[i] == 0)
    def _():
        o_ref[...] = jnp.zeros_like(o_ref)


def _moe_down_kernel(be_ref, bf_ref, bv_ref, h_ref, wd_ref, g_ref, o_ref, wdt16):
    i = pl.program_id(0)

    @pl.when(bf_ref[i] > 0)
    def _():
        wdt16[...] = wd_ref[...].T.astype(BF16)

    @pl.when(bv_ref[i] > 0)
    def _():
        yt = jnp.dot(wdt16[...], h_ref[...], preferred_element_type=F32) * g_ref[...]
        o_ref[...] = _pack_halves(yt.T.astype(BF16))

    @pl.when(bv_ref[i] == 0)
    def _():
        o_ref[...] = jnp.zeros_like(o_ref)


def _moe_experts(xn, slot_tok, slot_gate, blk_expert, blk_first, blk_valid, w_gate, w_up, w_down, layer):
    D = w_gate.shape[2]
    NR = slot_tok.shape[0]
    R = MOE_ROWS
    n_blk = NR // R
    F = w_gate.shape[3]
    hb = pl.pallas_call(
        _moe_up_kernel,
        grid_spec=pltpu.PrefetchScalarGridSpec(
            num_scalar_prefetch=4,
            grid=(n_blk,),
            in_specs=[pl.BlockSpec(memory_space=pl.ANY),
                      pl.BlockSpec((None, None, D, F), lambda i, tok, be, bf, bv: (layer, be[i], 0, 0)),
                      pl.BlockSpec((None, None, D, F), lambda i, tok, be, bf, bv: (layer, be[i], 0, 0))],
            out_specs=pl.BlockSpec((F, R), lambda i, tok, be, bf, bv: (0, i)),
            scratch_shapes=[pltpu.VMEM((2, R * (D // 2 // LANES), LANES), jnp.uint32), pltpu.SemaphoreType.DMA((2,)),
                            pltpu.VMEM((R, D), BF16), pltpu.VMEM((2 * F, D), BF16)],
        ),
        out_shape=jax.ShapeDtypeStruct((F, NR), BF16),
        compiler_params=_cparams("arbitrary"),
        name="moe_up",
    )(slot_tok, blk_expert, blk_first, blk_valid, xn, w_gate, w_up)
    row = lambda i, be, bf, bv: (i, 0)
    wsel = lambda i, be, bf, bv: (layer, be[i], 0, 0)
    return pl.pallas_call(
        _moe_down_kernel,
        grid_spec=pltpu.PrefetchScalarGridSpec(
            num_scalar_prefetch=3,
            grid=(n_blk,),
            in_specs=[pl.BlockSpec((F, R), lambda i, be, bf, bv: (0, i)), pl.BlockSpec((None, None, F, D), wsel),
                      pl.BlockSpec((1, R), lambda i, be, bf, bv: (0, i))],
            out_specs=pl.BlockSpec((R, D // 2), row),
            scratch_shapes=[pltpu.VMEM((D, F), BF16)],
        ),
        out_shape=jax.ShapeDtypeStruct((NR, D // 2), jnp.uint32),
        compiler_params=_cparams("arbitrary"),
        name="moe_down",
    )(blk_expert, blk_first, blk_valid, hb, w_down, slot_gate.reshape(1, NR))


def _moe_combine_kernel(h_ref, a_ref, b_ref, o_ref):
    half = a_ref.shape[1]
    a = a_ref[...]
    b = b_ref[...]
    top = jnp.uint32(0xFFFF0000)
    f32 = lambda w: lax.bitcast_convert_type(w, F32)
    o_ref[:, :half] = h_ref[:, :half] + (f32(a << 16) + f32(b << 16))
    o_ref[:, half:] = h_ref[:, half:] + (f32(a & top) + f32(b & top))


def _moe_combine(h, ya, yb, tm=256):
    T, D = h.shape
    packed = pl.BlockSpec((tm, D // 2), lambda i: (i, 0))
    return pl.pallas_call(
        _moe_combine_kernel,
        grid=(T // tm,),
        in_specs=[pl.BlockSpec((tm, D), lambda i: (i, 0)), packed, packed],
        out_specs=pl.BlockSpec((tm, D), lambda i: (i, 0)),
        out_shape=jax.ShapeDtypeStruct((T, D), F32),
        compiler_params=_cparams("parallel"),
        name="moe_combine",
    )(h, ya, yb)


def _hier_moe(h, ln_g, w_rg, w_re, w_gate, w_up, w_down, layer):
    T, D = h.shape
    R = MOE_ROWS
    n_rout = N_GROUPS + N_EXPERTS
    w_r = jnp.pad(jnp.concatenate([w_rg, w_re], axis=1), ((0, 0), (0, LANES - n_rout)))
    logits, xn_packed = _rms_router(h, ln_g, w_r)
    gl = logits[:, :N_GROUPS]
    g_sel = jnp.argmax(gl, axis=-1)
    p_group = jnp.take_along_axis(jax.nn.softmax(gl, axis=-1), g_sel[:, None], axis=-1)
    el = logits[:, N_GROUPS:n_rout].reshape(T, N_GROUPS, EXPERTS_PER_GROUP)
    el = jnp.take_along_axis(el, g_sel[:, None, None], axis=1)[:, 0]
    top_v, top_i = lax.top_k(el, TOP_K)
    gate = p_group * jax.nn.softmax(top_v, axis=-1)
    expert = (g_sel[:, None] * EXPERTS_PER_GROUP + top_i).reshape(-1).astype(jnp.int32)
    n_assign = T * TOP_K
    counts = jnp.sum(expert[:, None] == jnp.arange(N_EXPERTS, dtype=jnp.int32)[None, :], axis=0).astype(jnp.int32)
    padded = (counts + R - 1) // R * R
    p_end = jnp.cumsum(padded)
    p_start = p_end - padded
    n_blk = -(-n_assign // R) + N_EXPERTS
    blk_start = jnp.arange(n_blk, dtype=jnp.int32) * R
    blk_valid = (blk_start < p_end[-1]).astype(jnp.int32)
    last_valid = jnp.maximum(p_end[-1] // R - 1, 0)
    blk_expert = jnp.minimum(jnp.searchsorted(p_end, blk_start, side='right'), N_EXPERTS - 1).astype(jnp.int32)
    blk_expert = jnp.where(blk_valid > 0, blk_expert, blk_expert[last_valid])
    blk_first = jnp.concatenate([jnp.ones((1,), jnp.int32), (blk_expert[1:] != blk_expert[:-1]).astype(jnp.int32)])
    blk_rows = jnp.clip((p_start + counts)[blk_expert] - blk_start, 0, R).astype(jnp.int32) * blk_valid
    n_slots = n_blk * R
    n_pad = n_slots - n_assign
    pad_end = jnp.cumsum(padded - counts)
    pad_key = jnp.sum(jnp.arange(n_pad, dtype=jnp.int32)[:, None] >= pad_end[None, :], axis=1).astype(jnp.int32)
    keys = jnp.concatenate([expert, pad_key])
    ids = jnp.concatenate([jnp.arange(n_assign, dtype=jnp.int32), jnp.full((n_pad,), -1, jnp.int32)])
    gates = jnp.concatenate([gate.reshape(-1), jnp.zeros((n_pad,), F32)])
    _, slot_id, slot_gate = lax.sort((keys, ids, gates), num_keys=1, is_stable=True)
    slot_tok = jnp.maximum(slot_id, 0) // TOP_K
    _, slot_of = lax.sort((slot_id, jnp.arange(n_slots, dtype=jnp.int32)), num_keys=1)
    pos = slot_of[n_pad:]

    ys = _moe_experts(xn_packed, slot_tok, slot_gate, blk_expert, blk_first, blk_rows, w_gate, w_up, w_down, layer)
    pos2 = pos.reshape(T, TOP_K)
    rows_of = lambda idx: ys.at[idx].get(mode="promise_in_bounds")
    return _moe_combine(h, rows_of(pos2[:, 0]), rows_of(pos2[:, 1]))


def _mem_kv(mem, g, w_kv, layer, k_gain):
    M = mem.shape[0]
    (mn,) = _rmsnorm(mem, g, (BF16,))
    kv = _matmul(mn, w_kv, layer)
    k = kv[:, :MEM_WIDTH].reshape(M, MEM_HEADS, MEM_HEAD_DIM)
    k = k * lax.rsqrt(jnp.mean(k * k, axis=-1, keepdims=True) + NORM_EPS) * k_gain
    return k.reshape(M, MEM_WIDTH).astype(BF16), kv[:, MEM_WIDTH:].astype(BF16)


def _pad_cols(w, n):
    return jnp.pad(w, ((0, 0), (0, n - w.shape[1])))


def _a_side_weights(w_in, mu, w_vdown, mu_vres):
    D = w_in.shape[0]
    c = np.cumsum([A_MAIN, LORA_DECAY, LORA_AAA, LORA_GATE, MEM_WIDTH])
    xw, xa, xg, qm = slice(c[0], c[1]), slice(c[1], c[2]), slice(c[2], c[3]), slice(c[3], c[4])
    if w_vdown is None:
        w_vdown = jnp.zeros((D, LORA_MV), w_in.dtype)
        mu_vres = jnp.zeros((LORA_MV,), mu.dtype)
    tail = A_SIDE - S_OFF_XV
    w = jnp.concatenate([w_in[:, qm], _pad_cols(w_in[:, xg], A_XG_PAD), w_in[:, xw], w_in[:, xa],
                         _pad_cols(w_vdown, tail)], axis=1)
    m = jnp.concatenate([mu[qm], jnp.pad(mu[xg], (0, A_XG_PAD - LORA_GATE)), mu[xw], mu[xa],
                         jnp.pad(mu_vres, (0, tail - LORA_MV))])
    return w.astype(BF16), m


def _rwkv_layer(h, mk, mv, mq_gain, ln1, w_in, mu, w_vdown, mu_vres, y_first, w0, w_up, a0, a_up, g_up,
                v0, v_up, k_k, k_a, r_k, lnx_g, lnx_b, w_out, layer):
    (xn,) = _rmsnorm(h, ln1, (BF16,))
    y = _matmul_shift(xn, w_in, layer, mu[layer, :A_MAIN])
    w_side, mu_side = _a_side_weights(w_in[layer], mu[layer], w_vdown, mu_vres)
    y_side = _matmul_shift(xn, w_side[None], 0, mu_side)
    pad_rows = lambda w, n: jnp.pad(w, ((0, n - w.shape[0]), (0, 0))).astype(BF16)
    has_vres = y_first is not None
    vecs = jnp.stack([w0, a0, v0 if has_vres else jnp.zeros_like(w0), k_k, k_a, r_k.reshape(-1), lnx_g, lnx_b])
    mix_out = _rwkv_mix(y, y_side, y_first, w_up.astype(BF16), a_up.astype(BF16), pad_rows(g_up, A_XG_PAD),
                        pad_rows(v_up, A_XV_PAD) if has_vres else None, vecs)
    mem_out = _mem_attend(y_side, S_OFF_QM // MEM_WIDTH, mk, mv, mq_gain)
    return _matmul2_res(mix_out, mem_out, w_out, layer, h), y


def _dilated_layer(h, mk, mv, mq_gain, ln1, w_in, q_gain, ks, vs, w_out, layer):
    (xn,) = _rmsnorm(h, ln1, (BF16,))
    y = _matmul(xn, w_in, layer)
    parts = [_dilated_group(y, gi, dil, ks, vs, q_gain) for gi, (_, dil) in enumerate(DIL_PATTERNS)]
    att = _merge_groups(parts)
    mem_out = _mem_attend(y, DIL_Q_WIDTH // MEM_WIDTH, mk, mv, mq_gain)
    return _matmul2_res(att, mem_out, w_out, layer, h)


def _shared_kv(h, g, w_kv, k_gain):
    T = h.shape[0]
    (xn,) = _rmsnorm(h, g, (BF16,))
    kv = _matmul(xn, w_kv[None], 0)
    k = kv[:, :KV_WIDTH].reshape(T, DIL_HEADS, ATT_HEAD_DIM)
    k = k * lax.rsqrt(jnp.mean(k * k, axis=-1, keepdims=True) + NORM_EPS) * k_gain
    return k.reshape(T, KV_WIDTH), kv[:, KV_WIDTH:]


def kernel(x, mem, a_ln1, a_w_in, a_w_vdown, a_mu, a_mu_vres, a_w0, a_w_up, a_a0, a_a_up, a_g_up, a_v0, a_v_up, a_k_k, a_k_a, a_r_k, a_lnx_g, a_lnx_b, a_w_out, b_ln1, b_w_in, b_q_norm, b_w_out, s_kv_norm, s_w_kv, s_k_norm, m_norm, m_w_kv, m_q_norm, m_k_norm, moe_ln, moe_router_group, moe_router_expert, moe_w_gate, moe_w_up, moe_w_down):
    Bsz, S, D = x.shape
    assert Bsz == 1 and S % (max(d for _, d in DIL_PATTERNS) * ATT_BLOCK) == 0
    depth = moe_ln.shape[0]
    h = x.reshape(S, D)
    mem2 = mem.reshape(mem.shape[1], D)
    y_first = None
    shared = None
    for l in range(depth):
        mk, mv = _mem_kv(mem2, m_norm[l], m_w_kv, l, m_k_norm[l])
        if l < N_A:
            i = l
            has_vres = i > 0
            h, y_i = _rwkv_layer(
                h, mk, mv, m_q_norm[l], a_ln1[i], a_w_in, a_mu,
                a_w_vdown[i - 1] if has_vres else None, a_mu_vres[i - 1] if has_vres else None,
                y_first, a_w0[i], a_w_up[i], a_a0[i], a_a_up[i], a_g_up[i],
                a_v0[i - 1] if has_vres else None, a_v_up[i - 1] if has_vres else None,
                a_k_k[i], a_k_a[i], a_r_k[i],
                a_lnx_g[i], a_lnx_b[i], a_w_out, i)
            if i == 0:
                y_first = y_i
        else:
            j = l - N_A
            if shared is None:
                shared = _shared_kv(h, s_kv_norm, s_w_kv, s_k_norm)
            h = _dilated_layer(h, mk, mv, m_q_norm[l], b_ln1[j], b_w_in, b_q_norm[j], shared[0], shared[1], b_w_out, j)
        h = _hier_moe(h, moe_ln[l], moe_router_group[l], moe_router_expert[l], moe_w_gate, moe_w_up, moe_w_down, l)
    return h.reshape(Bsz, S, D)
```

```python
import functools

import jax
import jax.numpy as jnp
import numpy as np
from jax import lax
from jax.experimental import pallas as pl
from jax.experimental.pallas import tpu as pltpu

F32 = jnp.float32
BF16 = jnp.bfloat16

D_MODEL = 4096
RWKV_HEAD = 64
RWKV_WIDTH = 3072
LORA_DECAY = 128
LORA_AAA = 128
LORA_MV = 96
LORA_GATE = 480
LNX_EPS = 64e-5
MEM_HEADS = 4
MEM_WIDTH = 1024
MEM_HEAD_DIM = 256
ATT_HEAD_DIM = 128
DIL_PATTERNS = ((128, 1), (512, 4), (2048, 16))
N_DIL_GROUPS = 3
DIL_HEADS = 8
DIL_Q_WIDTH = 3072
KV_WIDTH = 1024
ATT_BLOCK = 128
N_GROUPS = 4
EXPERTS_PER_GROUP = 8
N_EXPERTS = 32
TOP_K = 2
D_FF = 384
NORM_EPS = 1e-6
N_A = 2

LANES = 128
SUBLANES = 8
VMEM_LIMIT_BYTES = 56 * 1024 * 1024

A_OFF_R = 0
A_OFF_K = RWKV_WIDTH
A_OFF_V = 2 * RWKV_WIDTH
A_OFF_QM = 3 * RWKV_WIDTH
A_OFF_XG = A_OFF_QM + MEM_WIDTH
A_XG_PAD = 512
A_OFF_XW = A_OFF_XG + A_XG_PAD
A_OFF_XA = A_OFF_XW + LORA_DECAY
A_OFF_XV = A_OFF_XA + LORA_AAA
A_XV_PAD = 128
A_IN_PAD = 11264

SCAN_CHUNK = 64
SCAN_TBLOCK = 128
MOE_ROWS = 256

_NN = (((1,), (0,)), ((), ()))
_NT = (((1,), (1,)), ((), ()))


def _cparams(*sem):
    return pltpu.CompilerParams(dimension_semantics=sem, vmem_limit_bytes=VMEM_LIMIT_BYTES)


def _rms_kernel(x_ref, g_ref, *o_refs):
    x = x_ref[...]
    y = x * lax.rsqrt(jnp.mean(x * x, axis=-1, keepdims=True) + NORM_EPS) * g_ref[...]
    for o_ref in o_refs:
        o_ref[...] = y.astype(o_ref.dtype)


def _rmsnorm(x, g, dtypes, tm=256):
    T, D = x.shape
    tm = min(tm, T)
    outs = pl.pallas_call(
        _rms_kernel,
        grid=(T // tm,),
        in_specs=[pl.BlockSpec((tm, D), lambda i: (i, 0)), pl.BlockSpec((1, D), lambda i: (0, 0))],
        out_specs=[pl.BlockSpec((tm, D), lambda i: (i, 0)) for _ in dtypes],
        out_shape=[jax.ShapeDtypeStruct((T, D), dt) for dt in dtypes],
        compiler_params=_cparams("parallel"),
        name="rmsnorm",
    )(x, g.reshape(1, D))
    return outs


def _pack_halves(x16):
    bits = lax.bitcast_convert_type(x16.astype(F32), jnp.uint32)
    half = bits.shape[1] // 2
    return (bits[:, :half] >> 16) | bits[:, half:]


def _unpack_halves(words):
    lo = [lax.bitcast_convert_type(w << 16, F32) for w in words]
    hi = [lax.bitcast_convert_type(w & jnp.uint32(0xFFFF0000), F32) for w in words]
    return jnp.concatenate(lo + hi, axis=1).astype(BF16)


def _rms_router_kernel(x_ref, g_ref, w_ref, logit_ref, packed_ref):
    x = x_ref[...]
    y = x * lax.rsqrt(jnp.mean(x * x, axis=-1, keepdims=True) + NORM_EPS) * g_ref[...]
    yh = y.astype(BF16)
    yl = (y - yh.astype(F32)).astype(BF16)
    w = w_ref[...]
    wh = w.astype(BF16)
    wl = (w - wh.astype(F32)).astype(BF16)
    d = lambda a, b: jnp.dot(a, b, preferred_element_type=F32)
    logit_ref[...] = d(yh, wh) + (d(yh, wl) + d(yl, wh))
    packed = _pack_halves(yh)
    tm = packed.shape[0]
    n_tiles = packed.shape[1] // LANES
    for c in range(n_tiles):
        packed_ref[pl.ds(c, tm, stride=n_tiles), :] = packed[:, c * LANES:(c + 1) * LANES]


def _rms_router(x, g, w_r, tm=512):
    T, D = x.shape
    N = w_r.shape[1]
    n_tiles = D // 2 // LANES
    return pl.pallas_call(
        _rms_router_kernel,
        grid=(T // tm,),
        in_specs=[pl.BlockSpec((tm, D), lambda i: (i, 0)), pl.BlockSpec((1, D), lambda i: (0, 0)),
                  pl.BlockSpec((D, N), lambda i: (0, 0))],
        out_specs=[pl.BlockSpec((tm, N), lambda i: (i, 0)), pl.BlockSpec((tm * n_tiles, LANES), lambda i: (i, 0))],
        out_shape=[jax.ShapeDtypeStruct((T, N), F32), jax.ShapeDtypeStruct((T * n_tiles, LANES), jnp.uint32)],
        compiler_params=_cparams("parallel"),
        name="rms_router",
    )(x, g.reshape(1, D), w_r)


def _mm_kernel(x_ref, w_ref, o_ref):
    o_ref[...] = jnp.dot(x_ref[...], w_ref[...].astype(BF16), preferred_element_type=F32)


def _mm_shift_kernel(x_ref, xp_ref, w_ref, mu_ref, o_ref):
    w = w_ref[...]
    y = jnp.dot(x_ref[...], w, preferred_element_type=F32)
    tail = jnp.dot(xp_ref[...], w, preferred_element_type=F32)
    last = jnp.where(pl.program_id(0) > 0, tail[tail.shape[0] - 1:, :], 0.0)
    row = lax.broadcasted_iota(jnp.int32, y.shape, 0)
    y_prev = jnp.where(row == 0, last, pltpu.roll(y, 1, axis=0))
    o_ref[...] = y + mu_ref[...] * (y_prev - y)


def _matmul_shift(x, w, mu, tm=1024, tn=512):
    M, K = x.shape
    N = w.shape[1]
    tm = min(tm, M)
    tp = 2 * SUBLANES
    assert M % tm == 0 and N % tn == 0 and tm % tp == 0
    return pl.pallas_call(
        _mm_shift_kernel,
        grid=(M // tm, N // tn),
        in_specs=[pl.BlockSpec((tm, K), lambda i, j: (i, 0)),
                  pl.BlockSpec((tp, K), lambda i, j: (jnp.maximum(i * (tm // tp) - 1, 0), 0)),
                  pl.BlockSpec((K, tn), lambda i, j: (0, j)),
                  pl.BlockSpec((1, tn), lambda i, j: (0, j))],
        out_specs=pl.BlockSpec((tm, tn), lambda i, j: (i, j)),
        out_shape=jax.ShapeDtypeStruct((M, N), F32),
        compiler_params=_cparams("parallel", "parallel"),
        name="matmul_shift",
    )(x, x, w, mu.reshape(1, N))


def _mm2_res_kernel(x1_ref, x2_ref, w1_ref, w2_ref, r_ref, o_ref):
    acc = jnp.dot(x1_ref[...], w1_ref[...].astype(BF16), preferred_element_type=F32)
    o_ref[...] = r_ref[...] + (acc + jnp.dot(x2_ref[...], w2_ref[...].astype(BF16), preferred_element_type=F32))


def _matmul2_res(x1, x2, w, layer, res, tm=1024, tn=512):
    M, K1 = x1.shape
    K2 = x2.shape[1]
    N = w.shape[2]
    tm = min(tm, M)
    assert M % tm == 0 and N % tn == 0 and K1 % K2 == 0 and w.shape[1] == K1 + K2
    return pl.pallas_call(
        _mm2_res_kernel,
        grid=(M // tm, N // tn),
        in_specs=[pl.BlockSpec((tm, K1), lambda i, j: (i, 0)), pl.BlockSpec((tm, K2), lambda i, j: (i, 0)),
                  pl.BlockSpec((None, K1, tn), lambda i, j: (layer, 0, j)),
                  pl.BlockSpec((None, K2, tn), lambda i, j: (layer, K1 // K2, j)),
                  pl.BlockSpec((tm, tn), lambda i, j: (i, j))],
        out_specs=pl.BlockSpec((tm, tn), lambda i, j: (i, j)),
        out_shape=jax.ShapeDtypeStruct((M, N), F32),
        compiler_params=_cparams("parallel", "parallel"),
        name="matmul_out",
    )(x1, x2, w, w, res)


def _matmul(x, w, layer, tm=1024, tn=512):
    M, K = x.shape
    N = w.shape[2]
    tm = min(tm, M)
    tn = min(tn, N)
    assert M % tm == 0 and N % tn == 0
    return pl.pallas_call(
        _mm_kernel,
        grid=(M // tm, N // tn),
        in_specs=[pl.BlockSpec((tm, K), lambda i, j: (i, 0)), pl.BlockSpec((None, K, tn), lambda i, j: (layer, 0, j))],
        out_specs=pl.BlockSpec((tm, tn), lambda i, j: (i, j)),
        out_shape=jax.ShapeDtypeStruct((M, N), F32),
        compiler_params=_cparams("parallel", "parallel"),
        name="matmul",
    )(x, w)


def _split(x):
    hi = x.astype(BF16)
    lo = (x - hi.astype(F32)).astype(BF16)
    return hi, lo


def _dot1(a, b, dims=_NN):
    return lax.dot_general(a.astype(BF16), b.astype(BF16), dims, preferred_element_type=F32)


def _round_robin(gens):
    outs = [None] * len(gens)
    active = list(range(len(gens)))
    while active:
        for i in list(active):
            try:
                next(gens[i])
            except StopIteration as stop:
                outs[i] = stop.value
                active.remove(i)
    return outs


def _rwkv_mix_kernel(*refs, n_chunks, n_pairs, has_vres):
    if has_vres:
        (r_ref, k_ref, v_ref, xg_ref, xw_ref, xa_ref, xv_ref, vf_ref, wup_ref, aup_ref, gup_ref, vup_ref, vec_ref,
         o_ref, s_ref, lw_s, k_s, v_s, a_s, b_s, y_s) = refs
    else:
        (r_ref, k_ref, v_ref, xg_ref, xw_ref, xa_ref, wup_ref, aup_ref, gup_ref, vec_ref,
         o_ref, s_ref, lw_s, k_s, v_s, a_s, b_s, y_s) = refs
    C = SCAN_CHUNK
    H2 = 2 * C

    @pl.when(pl.program_id(0) == 0)
    def _():
        s_ref[...] = jnp.zeros_like(s_ref)

    lane = lax.broadcasted_iota(jnp.int32, (C, LANES), 1)
    head0 = lane < RWKV_HEAD
    trow = lax.broadcasted_iota(jnp.int32, (C, LANES), 0)
    tcol = lane & (C - 1)
    lane2 = lax.broadcasted_iota(jnp.int32, (H2, LANES), 1)
    row2 = lax.broadcasted_iota(jnp.int32, (H2, LANES), 0)
    blockdiag = (row2 < C) == (lane2 < RWKV_HEAD)
    head_ones = jnp.where(blockdiag, 1.0, 0.0).astype(BF16)
    tri_r = lax.broadcasted_iota(jnp.int32, (C, C), 0)
    tri_c = lax.broadcasted_iota(jnp.int32, (C, C), 1)
    ltri = jnp.where(tri_r >= tri_c, 1.0, 0.0).astype(BF16)
    arow = lax.broadcasted_iota(jnp.int32, (H2, 2 * H2), 0)
    acol = lax.broadcasted_iota(jnp.int32, (H2, 2 * H2), 1)
    a_t = arow & (C - 1)
    a_j = acol & (C - 1)
    amask = a_t + jnp.where(arow < C, 0, 1) > a_j
    n_levels = C.bit_length() - 1
    eye2 = jnp.where(trow == tcol, 1.0, 0.0)
    dd = lambda x, y: jnp.dot(x, y, preferred_element_type=F32)
    tiles = [slice(p * LANES, (p + 1) * LANES) for p in range(n_pairs)]
    vec = lambda i, ls: vec_ref[i:i + 1, ls]

    def head_sum(x):
        hi, lo = _split(x)
        return dd(hi, head_ones) + dd(lo, head_ones)

    def level_mask(lvl):
        tb_, jb_ = trow >> lvl, tcol >> lvl
        return (tb_ - jb_) * (tb_ & 1) == 1

    def stack2(x):
        z = jnp.zeros_like(x)
        return jnp.concatenate([jnp.where(head0, x, z), jnp.where(head0, z, x)], axis=0)

    xw = jnp.tanh(xw_ref[...]).astype(BF16)
    xa = xa_ref[...].astype(BF16)
    if has_vres:
        xv = xv_ref[...].astype(BF16)
    for ls in tiles:
        z = vec(_V_W0, ls) + dd(xw, wup_ref[:, ls])
        lw_s[:, ls] = jax.nn.sigmoid(z) * (-float(np.exp(-0.5)))
        a = jax.nn.sigmoid(vec(_V_A0, ls) + dd(xa, aup_ref[:, ls]))
        k = k_ref[:, ls]
        kk = k * vec(_V_KK, ls)
        kk = kk * lax.rsqrt(jnp.maximum(head_sum(kk * kk), 1e-24))
        v = v_ref[:, ls]
        if has_vres:
            mix = jax.nn.sigmoid(vec(_V_V0, ls) + dd(xv, vup_ref[:, ls]))
            v = v + (vf_ref[:, ls] - v) * mix
        k_s[:, ls] = k * (1.0 + (a - 1.0) * vec(_V_KA, ls))
        v_s[:, ls] = v
        a_s[:, ls] = -kk
        b_s[:, ls] = kk * a

    def pair_chunk(r, lw, k, v, a, b, s):
        l1 = lw.astype(BF16)
        rem = lw - l1.astype(F32)
        l2 = rem.astype(BF16)
        l3 = (rem - l2.astype(F32)).astype(BF16)
        cum = dd(ltri, l1) + (dd(ltri, l2) + dd(ltri, l3))
        yield
        p_inc = jnp.exp(cum)
        p_exc = jnp.exp(cum - lw)
        p_inv = jnp.exp(-cum)
        at = a * p_exc
        rt = r * p_inc
        bt = b * p_inv
        kt = k * p_inv
        p_tot = p_inc[C - 1:C, :]

        x_ar = jnp.concatenate([at, rt], axis=0).astype(BF16)
        w_bk = jnp.concatenate([stack2(bt), stack2(kt)], axis=0)
        amat = jnp.where(amask, _dot1(x_ar, w_bk, _NT), 0.0)
        xs = _dot1(x_ar, s, _NT)
        yield

        n_ab = amat[:C, :H2]
        a_ak = amat[:C, H2:]
        a_r = amat[C:, :]
        v2 = stack2(v)
        rhs = xs[:C] + _dot1(a_ak, v2)
        yield

        tinv = eye2 + jnp.where(level_mask(0), n_ab, 0.0)
        for lvl in range(1, n_levels):
            nt = _dot1(jnp.where(level_mask(lvl), n_ab, 0.0), stack2(tinv))
            yield
            tinv = tinv + _dot1(tinv, stack2(nt))
            yield
        u = _dot1(tinv, stack2(rhs))
        yield

        uv = jnp.concatenate([stack2(u), v2], axis=0)
        y = xs[C:] + _dot1(a_r, uv)
        yield

        uv_t = jnp.concatenate([u, v], axis=0).T
        bk = jnp.concatenate([bt, kt], axis=0)
        ds = _dot1(uv_t, bk)
        return y, (s + jnp.where(blockdiag, ds, 0.0)) * p_tot

    def chunk(ci, carry):
        sl = pl.ds(pl.multiple_of(ci * C, C), C)
        ins = [tuple(ref[sl, ls] for ref in (r_ref, lw_s, k_s, v_s, a_s, b_s)) + (s_ref[p],)
               for p, ls in enumerate(tiles)]
        outs = _round_robin([pair_chunk(*args) for args in ins])
        for p, ls in enumerate(tiles):
            y_s[sl, ls] = outs[p][0]
            s_ref[p] = outs[p][1]
        return carry

    lax.fori_loop(0, n_chunks, chunk, 0)

    xg = jax.nn.sigmoid(xg_ref[...]).astype(BF16)
    inv_n = 1.0 / RWKV_HEAD
    for ls in tiles:
        y = y_s[:, ls]
        d = y - head_sum(y) * inv_n
        var = head_sum(d * d) * inv_n
        yn = d * lax.rsqrt(var + LNX_EPS) * vec(_V_LNG, ls) + vec(_V_LNB, ls)
        bonus = head_sum(r_ref[:, ls] * k_s[:, ls] * vec(_V_RK, ls)) * v_s[:, ls]
        o_ref[:, ls] = ((yn + bonus) * dd(xg, gup_ref[:, ls])).astype(BF16)


_V_W0, _V_A0, _V_V0, _V_KK, _V_KA, _V_RK, _V_LNG, _V_LNB = range(8)


def _rwkv_mix(y, y_first, w_up, a_up, g_up, v_up, vecs):
    T = y.shape[0]
    W = RWKV_WIDTH
    tb = min(SCAN_TBLOCK, T)
    has_vres = y_first is not None
    assert T % tb == 0 and tb % SCAN_CHUNK == 0
    wide = lambda blk: pl.BlockSpec((tb, W), lambda t: (t, blk))
    narrow = lambda off, width: pl.BlockSpec((tb, width), lambda t: (t, off // width))
    full = lambda arr: pl.BlockSpec(arr.shape, lambda t: (0, 0))
    in_specs = [wide(A_OFF_R // W), wide(A_OFF_K // W), wide(A_OFF_V // W),
                narrow(A_OFF_XG, A_XG_PAD), narrow(A_OFF_XW, LORA_DECAY), narrow(A_OFF_XA, LORA_AAA)]
    args = [y, y, y, y, y, y]
    if has_vres:
        in_specs += [narrow(A_OFF_XV, A_XV_PAD), wide(A_OFF_V // W)]
        args += [y, y_first]
    weights = [w_up, a_up, g_up] + ([v_up] if has_vres else []) + [vecs]
    in_specs += [full(w) for w in weights]
    args += weights
    return pl.pallas_call(
        functools.partial(_rwkv_mix_kernel, n_chunks=tb // SCAN_CHUNK, n_pairs=W // LANES, has_vres=has_vres),
        grid=(T // tb,),
        in_specs=in_specs,
        out_specs=pl.BlockSpec((tb, W), lambda t: (t, 0)),
        out_shape=jax.ShapeDtypeStruct((T, W), BF16),
        scratch_shapes=[pltpu.VMEM((W // LANES, LANES, LANES), F32)] + [pltpu.VMEM((tb, W), F32)] * 6,
        compiler_params=_cparams("arbitrary"),
        name="rwkv7_mix",
    )(*args)


def _mem_attn_kernel(q_ref, k_ref, v_ref, g_ref, o_ref):
    scale = MEM_HEAD_DIM ** -0.5
    for h in range(MEM_HEADS):
        hs = slice(h * MEM_HEAD_DIM, (h + 1) * MEM_HEAD_DIM)
        q = q_ref[:, hs]
        qn = q * lax.rsqrt(jnp.mean(q * q, axis=-1, keepdims=True) + NORM_EPS) * g_ref[...]
        s = lax.dot_general(qn.astype(BF16), k_ref[:, hs], _NT, preferred_element_type=F32) * scale
        m = jnp.max(s, axis=-1, keepdims=True)
        p = jnp.exp(s - m)
        l = jnp.sum(p, axis=-1, keepdims=True)
        o = jnp.dot(p.astype(BF16), v_ref[:, hs], preferred_element_type=F32)
        o_ref[:, hs] = (o / l).astype(o_ref.dtype)


def _mem_attend(y, col_block, mk, mv, q_gain, tm=512):
    T = y.shape[0]
    M = mk.shape[0]
    tm = min(tm, T)
    return pl.pallas_call(
        _mem_attn_kernel,
        grid=(T // tm,),
        in_specs=[
            pl.BlockSpec((tm, MEM_WIDTH), lambda i: (i, col_block)),
            pl.BlockSpec((M, MEM_WIDTH), lambda i: (0, 0)),
            pl.BlockSpec((M, MEM_WIDTH), lambda i: (0, 0)),
            pl.BlockSpec((1, MEM_HEAD_DIM), lambda i: (0, 0)),
        ],
        out_specs=pl.BlockSpec((tm, MEM_WIDTH), lambda i: (i, 0)),
        out_shape=jax.ShapeDtypeStruct((T, MEM_WIDTH), BF16),
        compiler_params=_cparams("parallel"),
        name="mem_attend",
    )(y, mk, mv, q_gain.reshape(1, MEM_HEAD_DIM))


def _dil_attn_kernel(sl_ref, q_ref, kp_ref, kc_ref, vp_ref, vc_ref, g_ref, o_ref, lse_ref, *, dil, heads):
    n = pl.program_id(0)
    h0 = pl.program_id(1) * heads
    B = ATT_BLOCK
    n_back = B
    scale = ATT_HEAD_DIM ** -0.5
    qi = lax.broadcasted_iota(jnp.int32, (B, 2 * B), 0)
    ki = lax.broadcasted_iota(jnp.int32, (B, 2 * B), 1)
    j = B + qi - ki
    valid = (j >= 0) & (j <= n_back) & ((ki >= B) | (n > 0))
    dist = (j * dil).astype(F32)
    for c in range(dil):
        rows = pl.ds(c, B, stride=dil) if dil > 1 else slice(None)
        for h in range(heads):
            hs = slice(h * ATT_HEAD_DIM, (h + 1) * ATT_HEAD_DIM)
            q = q_ref[rows, hs]
            qn = q * lax.rsqrt(jnp.mean(q * q, axis=-1, keepdims=True) + NORM_EPS) * g_ref[...]
            kcat = jnp.concatenate([kp_ref[rows, hs], kc_ref[rows, hs]], axis=0).astype(BF16)
            vcat = jnp.concatenate([vp_ref[rows, hs], vc_ref[rows, hs]], axis=0).astype(BF16)
            s = lax.dot_general(qn.astype(BF16), kcat, _NT, preferred_element_type=F32) * scale
            s = jnp.where(valid, s - sl_ref[h0 + h] * dist, -1e30)
            m = jnp.max(s, axis=-1, keepdims=True)
            p = jnp.exp(s - m)
            l = jnp.sum(p, axis=-1, keepdims=True)
            o = jnp.dot(p.astype(BF16), vcat, preferred_element_type=F32)
            o_ref[rows, hs] = o / l
            lse_ref[rows, hs] = jnp.broadcast_to(m + jnp.log(l), (B, ATT_HEAD_DIM))


def _dilated_group(y, gi, dil, k, v, q_gain):
    T, W = y.shape
    n_heads_total = N_DIL_GROUPS * DIL_HEADS
    slopes = jnp.asarray([2.0 ** (-8.0 * (gi * DIL_HEADS + h + 1) / n_heads_total) for h in range(DIL_HEADS)], F32)
    rows = ATT_BLOCK * dil
    cols = KV_WIDTH if dil == 1 else ATT_HEAD_DIM
    ncb = KV_WIDTH // cols
    blk = (rows, cols)
    cur = lambda n, hb: (n, hb)
    prev = lambda n, hb: (jnp.maximum(n - 1, 0), hb)
    o, lse = pl.pallas_call(
        functools.partial(_dil_attn_kernel, dil=dil, heads=cols // ATT_HEAD_DIM),
        grid=(T // rows, ncb),
        in_specs=[
            pl.BlockSpec(memory_space=pltpu.SMEM),
            pl.BlockSpec(blk, lambda n, hb: (n, gi * ncb + hb)),
            pl.BlockSpec(blk, prev), pl.BlockSpec(blk, cur),
            pl.BlockSpec(blk, prev), pl.BlockSpec(blk, cur),
            pl.BlockSpec((1, ATT_HEAD_DIM), lambda n, hb: (0, 0)),
        ],
        out_specs=[pl.BlockSpec(blk, cur), pl.BlockSpec(blk, cur)],
        out_shape=[jax.ShapeDtypeStruct((T, KV_WIDTH), F32)] * 2,
        compiler_params=_cparams("parallel", "parallel"),
        name=f"dilated_attn_d{dil}",
    )(slopes, y, k, k, v, v, q_gain.reshape(1, ATT_HEAD_DIM))
    return o, lse


def _merge_kernel(o0, l0, o1, l1, o2, l2, out_ref):
    a0, a1, a2 = l0[...], l1[...], l2[...]
    m = jnp.maximum(jnp.maximum(a0, a1), a2)
    w0, w1, w2 = jnp.exp(a0 - m), jnp.exp(a1 - m), jnp.exp(a2 - m)
    out_ref[...] = ((w0 * o0[...] + w1 * o1[...] + w2 * o2[...]) / (w0 + w1 + w2)).astype(out_ref.dtype)


def _merge_groups(parts, tm=512):
    T, W = parts[0][0].shape
    tm = min(tm, T)
    spec = pl.BlockSpec((tm, W), lambda i: (i, 0))
    flat = [t for pr in parts for t in pr]
    return pl.pallas_call(
        _merge_kernel,
        grid=(T // tm,),
        in_specs=[spec] * 6,
        out_specs=spec,
        out_shape=jax.ShapeDtypeStruct((T, W), BF16),
        compiler_params=_cparams("parallel"),
        name="dilated_merge",
    )(*flat)


def _moe_up_kernel(tok_ref, be_ref, bf_ref, bv_ref, x_hbm, wg_ref, wu_ref, o_ref, xbuf, sem, x16, wt16):
    i = pl.program_id(0)
    n_blk = pl.num_programs(0)
    R = MOE_ROWS
    n_tiles = xbuf.shape[1] // R
    unroll = SUBLANES

    def row_copy(tok, slot, r):
        src = x_hbm.at[pl.ds(pl.multiple_of(tok * n_tiles, n_tiles), n_tiles)]
        dst = xbuf.at[slot, pl.ds(pl.multiple_of(r * n_tiles, n_tiles), n_tiles)]
        return pltpu.make_async_copy(src, dst, sem.at[slot])

    def trips(blk):
        return (bv_ref[blk] + (unroll - 1)) // unroll

    def start_gather(blk, slot):
        def body(g, carry):
            for u in range(unroll):
                r = g * unroll + u
                row_copy(tok_ref[blk * R + r], slot, r).start(priority=u % 2)
            return carry
        lax.fori_loop(0, trips(blk), body, 0)

    @pl.when(i == 0)
    def _():
        xbuf[...] = jnp.zeros_like(xbuf)
        start_gather(0, 0)

    nxt = jnp.minimum(i + 1, n_blk - 1)

    @pl.when((i + 1 < n_blk) & (bv_ref[nxt] > 0))
    def _():
        start_gather(i + 1, (i + 1) & 1)

    @pl.when(bf_ref[i] > 0)
    def _():
        F = wg_ref.shape[1]
        wt16[:F, :] = wg_ref[...].T.astype(BF16)
        wt16[F:, :] = wu_ref[...].T.astype(BF16)

    def unpack(slot):
        def wait_rows(g, carry):
            for u in range(unroll):
                row_copy(0, slot, g * unroll + u).wait()
            return carry
        lax.fori_loop(0, trips(i), wait_rows, 0)
        x16[...] = _unpack_halves([xbuf[slot, pl.ds(c, R, stride=n_tiles), :] for c in range(n_tiles)])

    for slot in range(2):
        pl.when((bv_ref[i] > 0) & ((i & 1) == slot))(functools.partial(unpack, slot))

    @pl.when(bv_ref[i] > 0)
    def _():
        F = o_ref.shape[0]
        ht = lax.dot_general(wt16[...], x16[...], _NT, preferred_element_type=F32)
        hg = ht[:F]
        o_ref[...] = (hg * jax.nn.sigmoid(hg) * ht[F:]).astype(BF16)

    @pl.when(bv_ref[i] == 0)
    def _():
        o_ref[...] = jnp.zeros_like(o_ref)


def _moe_down_kernel(be_ref, bf_ref, bv_ref, h_ref, wd_ref, g_ref, o_ref, wdt16):
    i = pl.program_id(0)

    @pl.when(bf_ref[i] > 0)
    def _():
        wdt16[...] = wd_ref[...].T.astype(BF16)

    @pl.when(bv_ref[i] > 0)
    def _():
        yt = jnp.dot(wdt16[...], h_ref[...], preferred_element_type=F32) * g_ref[...]
        o_ref[...] = _pack_halves(yt.T.astype(BF16))

    @pl.when(bv_ref[i] == 0)
    def _():
        o_ref[...] = jnp.zeros_like(o_ref)


def _moe_experts(xn, slot_tok, slot_gate, blk_expert, blk_first, blk_valid, w_gate, w_up, w_down, layer):
    D = w_gate.shape[2]
    NR = slot_tok.shape[0]
    R = MOE_ROWS
    n_blk = NR // R
    F = w_gate.shape[3]
    hb = pl.pallas_call(
        _moe_up_kernel,
        grid_spec=pltpu.PrefetchScalarGridSpec(
            num_scalar_prefetch=4,
            grid=(n_blk,),
            in_specs=[pl.BlockSpec(memory_space=pl.ANY),
                      pl.BlockSpec((None, None, D, F), lambda i, tok, be, bf, bv: (layer, be[i], 0, 0)),
                      pl.BlockSpec((None, None, D, F), lambda i, tok, be, bf, bv: (layer, be[i], 0, 0))],
            out_specs=pl.BlockSpec((F, R), lambda i, tok, be, bf, bv: (0, i)),
            scratch_shapes=[pltpu.VMEM((2, R * (D // 2 // LANES), LANES), jnp.uint32), pltpu.SemaphoreType.DMA((2,)),
                            pltpu.VMEM((R, D), BF16), pltpu.VMEM((2 * F, D), BF16)],
        ),
        out_shape=jax.ShapeDtypeStruct((F, NR), BF16),
        compiler_params=_cparams("arbitrary"),
        name="moe_up",
    )(slot_tok, blk_expert, blk_first, blk_valid, xn, w_gate, w_up)
    row = lambda i, be, bf, bv: (i, 0)
    wsel = lambda i, be, bf, bv: (layer, be[i], 0, 0)
    return pl.pallas_call(
        _moe_down_kernel,
        grid_spec=pltpu.PrefetchScalarGridSpec(
            num_scalar_prefetch=3,
            grid=(n_blk,),
            in_specs=[pl.BlockSpec((F, R), lambda i, be, bf, bv: (0, i)), pl.BlockSpec((None, None, F, D), wsel),
                      pl.BlockSpec((1, R), lambda i, be, bf, bv: (0, i))],
            out_specs=pl.BlockSpec((R, D // 2), row),
            scratch_shapes=[pltpu.VMEM((D, F), BF16)],
        ),
        out_shape=jax.ShapeDtypeStruct((NR, D // 2), jnp.uint32),
        compiler_params=_cparams("arbitrary"),
        name="moe_down",
    )(blk_expert, blk_first, blk_valid, hb, w_down, slot_gate.reshape(1, NR))


def _moe_combine_kernel(h_ref, a_ref, b_ref, o_ref):
    half = a_ref.shape[1]
    a = a_ref[...]
    b = b_ref[...]
    top = jnp.uint32(0xFFFF0000)
    f32 = lambda w: lax.bitcast_convert_type(w, F32)
    o_ref[:, :half] = h_ref[:, :half] + (f32(a << 16) + f32(b << 16))
    o_ref[:, half:] = h_ref[:, half:] + (f32(a & top) + f32(b & top))


def _moe_combine(h, ya, yb, tm=256):
    T, D = h.shape
    packed = pl.BlockSpec((tm, D // 2), lambda i: (i, 0))
    return pl.pallas_call(
        _moe_combine_kernel,
        grid=(T // tm,),
        in_specs=[pl.BlockSpec((tm, D), lambda i: (i, 0)), packed, packed],
        out_specs=pl.BlockSpec((tm, D), lambda i: (i, 0)),
        out_shape=jax.ShapeDtypeStruct((T, D), F32),
        compiler_params=_cparams("parallel"),
        name="moe_combine",
    )(h, ya, yb)


def _hier_moe(h, ln_g, w_rg, w_re, w_gate, w_up, w_down, layer):
    T, D = h.shape
    R = MOE_ROWS
    n_rout = N_GROUPS + N_EXPERTS
    w_r = jnp.pad(jnp.concatenate([w_rg, w_re], axis=1), ((0, 0), (0, LANES - n_rout)))
    logits, xn_packed = _rms_router(h, ln_g, w_r)
    gl = logits[:, :N_GROUPS]
    g_sel = jnp.argmax(gl, axis=-1)
    p_group = jnp.take_along_axis(jax.nn.softmax(gl, axis=-1), g_sel[:, None], axis=-1)
    el = logits[:, N_GROUPS:n_rout].reshape(T, N_GROUPS, EXPERTS_PER_GROUP)
    el = jnp.take_along_axis(el, g_sel[:, None, None], axis=1)[:, 0]
    top_v, top_i = lax.top_k(el, TOP_K)
    gate = p_group * jax.nn.softmax(top_v, axis=-1)
    expert = (g_sel[:, None] * EXPERTS_PER_GROUP + top_i).reshape(-1).astype(jnp.int32)
    n_assign = T * TOP_K
    counts = jnp.sum(expert[:, None] == jnp.arange(N_EXPERTS, dtype=jnp.int32)[None, :], axis=0).astype(jnp.int32)
    padded = (counts + R - 1) // R * R
    p_end = jnp.cumsum(padded)
    p_start = p_end - padded
    n_blk = -(-n_assign // R) + N_EXPERTS
    blk_start = jnp.arange(n_blk, dtype=jnp.int32) * R
    blk_valid = (blk_start < p_end[-1]).astype(jnp.int32)
    last_valid = jnp.maximum(p_end[-1] // R - 1, 0)
    blk_expert = jnp.minimum(jnp.searchsorted(p_end, blk_start, side='right'), N_EXPERTS - 1).astype(jnp.int32)
    blk_expert = jnp.where(blk_valid > 0, blk_expert, blk_expert[last_valid])
    blk_first = jnp.concatenate([jnp.ones((1,), jnp.int32), (blk_expert[1:] != blk_expert[:-1]).astype(jnp.int32)])
    blk_rows = jnp.clip((p_start + counts)[blk_expert] - blk_start, 0, R).astype(jnp.int32) * blk_valid
    n_slots = n_blk * R
    n_pad = n_slots - n_assign
    pad_end = jnp.cumsum(padded - counts)
    pad_key = jnp.sum(jnp.arange(n_pad, dtype=jnp.int32)[:, None] >= pad_end[None, :], axis=1).astype(jnp.int32)
    keys = jnp.concatenate([expert, pad_key])
    ids = jnp.concatenate([jnp.arange(n_assign, dtype=jnp.int32), jnp.full((n_pad,), -1, jnp.int32)])
    gates = jnp.concatenate([gate.reshape(-1), jnp.zeros((n_pad,), F32)])
    _, slot_id, slot_gate = lax.sort((keys, ids, gates), num_keys=1, is_stable=True)
    slot_tok = jnp.maximum(slot_id, 0) // TOP_K
    _, slot_of = lax.sort((slot_id, jnp.arange(n_slots, dtype=jnp.int32)), num_keys=1)
    pos = slot_of[n_pad:]

    ys = _moe_experts(xn_packed, slot_tok, slot_gate, blk_expert, blk_first, blk_rows, w_gate, w_up, w_down, layer)
    pos2 = pos.reshape(T, TOP_K)
    rows_of = lambda idx: ys.at[idx].get(mode="promise_in_bounds")
    return _moe_combine(h, rows_of(pos2[:, 0]), rows_of(pos2[:, 1]))


def _mem_kv(mem, g, w_kv, layer, k_gain):
    M = mem.shape[0]
    (mn,) = _rmsnorm(mem, g, (BF16,))
    kv = _matmul(mn, w_kv, layer)
    k = kv[:, :MEM_WIDTH].reshape(M, MEM_HEADS, MEM_HEAD_DIM)
    k = k * lax.rsqrt(jnp.mean(k * k, axis=-1, keepdims=True) + NORM_EPS) * k_gain
    return k.reshape(M, MEM_WIDTH).astype(BF16), kv[:, MEM_WIDTH:].astype(BF16)


def _pad_cols(w, n):
    return jnp.pad(w, ((0, 0), (0, n - w.shape[1])))


def _a_layer_weights(w_in, mu, w_vdown, mu_vres):
    D = w_in.shape[0]
    c = np.cumsum([RWKV_WIDTH, RWKV_WIDTH, RWKV_WIDTH, LORA_DECAY, LORA_AAA, LORA_GATE, MEM_WIDTH])
    rkv, xw, xa, xg, qm = (slice(0, c[2]), slice(c[2], c[3]), slice(c[3], c[4]), slice(c[4], c[5]), slice(c[5], c[6]))
    if w_vdown is None:
        w_vdown = jnp.zeros((D, LORA_MV), w_in.dtype)
        mu_vres = jnp.zeros((LORA_MV,), mu.dtype)
    tail = A_IN_PAD - A_OFF_XV
    w = jnp.concatenate([w_in[:, rkv], w_in[:, qm], _pad_cols(w_in[:, xg], A_XG_PAD), w_in[:, xw], w_in[:, xa],
                         _pad_cols(w_vdown, tail)], axis=1)
    m = jnp.concatenate([mu[rkv], mu[qm], jnp.pad(mu[xg], (0, A_XG_PAD - LORA_GATE)), mu[xw], mu[xa],
                         jnp.pad(mu_vres, (0, tail - LORA_MV))])
    return w.astype(BF16), m


def _rwkv_layer(h, mk, mv, mq_gain, ln1, w_in, mu, w_vdown, mu_vres, y_first, w0, w_up, a0, a_up, g_up,
                v0, v_up, k_k, k_a, r_k, lnx_g, lnx_b, w_out, layer):
    (xn,) = _rmsnorm(h, ln1, (BF16,))
    w_pad, mu_pad = _a_layer_weights(w_in, mu, w_vdown, mu_vres)
    y = _matmul_shift(xn, w_pad, mu_pad)
    pad_rows = lambda w, n: jnp.pad(w, ((0, n - w.shape[0]), (0, 0))).astype(BF16)
    has_vres = y_first is not None
    vecs = jnp.stack([w0, a0, v0 if has_vres else jnp.zeros_like(w0), k_k, k_a, r_k.reshape(-1), lnx_g, lnx_b])
    mix_out = _rwkv_mix(y, y_first, w_up.astype(BF16), a_up.astype(BF16), pad_rows(g_up, A_XG_PAD),
                        pad_rows(v_up, A_XV_PAD) if has_vres else None, vecs)
    mem_out = _mem_attend(y, A_OFF_QM // MEM_WIDTH, mk, mv, mq_gain)
    return _matmul2_res(mix_out, mem_out, w_out, layer, h), y


def _dilated_layer(h, mk, mv, mq_gain, ln1, w_in, q_gain, ks, vs, w_out, layer):
    (xn,) = _rmsnorm(h, ln1, (BF16,))
    y = _matmul(xn, w_in, layer)
    parts = [_dilated_group(y, gi, dil, ks, vs, q_gain) for gi, (_, dil) in enumerate(DIL_PATTERNS)]
    att = _merge_groups(parts)
    mem_out = _mem_attend(y, DIL_Q_WIDTH // MEM_WIDTH, mk, mv, mq_gain)
    return _matmul2_res(att, mem_out, w_out, layer, h)


def _shared_kv(h, g, w_kv, k_gain):
    T = h.shape[0]
    (xn,) = _rmsnorm(h, g, (BF16,))
    kv = _matmul(xn, w_kv[None], 0)
    k = kv[:, :KV_WIDTH].reshape(T, DIL_HEADS, ATT_HEAD_DIM)
    k = k * lax.rsqrt(jnp.mean(k * k, axis=-1, keepdims=True) + NORM_EPS) * k_gain
    return k.reshape(T, KV_WIDTH), kv[:, KV_WIDTH:]


def kernel(x, mem, a_ln1, a_w_in, a_w_vdown, a_mu, a_mu_vres, a_w0, a_w_up, a_a0, a_a_up, a_g_up, a_v0, a_v_up, a_k_k, a_k_a, a_r_k, a_lnx_g, a_lnx_b, a_w_out, b_ln1, b_w_in, b_q_norm, b_w_out, s_kv_norm, s_w_kv, s_k_norm, m_norm, m_w_kv, m_q_norm, m_k_norm, moe_ln, moe_router_group, moe_router_expert, moe_w_gate, moe_w_up, moe_w_down):
    Bsz, S, D = x.shape
    assert Bsz == 1 and S % (max(d for _, d in DIL_PATTERNS) * ATT_BLOCK) == 0
    depth = moe_ln.shape[0]
    h = x.reshape(S, D)
    mem2 = mem.reshape(mem.shape[1], D)
    y_first = None
    shared = None
    for l in range(depth):
        mk, mv = _mem_kv(mem2, m_norm[l], m_w_kv, l, m_k_norm[l])
        if l < N_A:
            i = l
            has_vres = i > 0
            h, y_i = _rwkv_layer(
                h, mk, mv, m_q_norm[l], a_ln1[i], a_w_in[i], a_mu[i],
                a_w_vdown[i - 1] if has_vres else None, a_mu_vres[i - 1] if has_vres else None,
                y_first, a_w0[i], a_w_up[i], a_a0[i], a_a_up[i], a_g_up[i],
                a_v0[i - 1] if has_vres else None, a_v_up[i - 1] if has_vres else None,
                a_k_k[i], a_k_a[i], a_r_k[i],
                a_lnx_g[i], a_lnx_b[i], a_w_out, i)
            if i == 0:
                y_first = y_i
        else:
            j = l - N_A
            if shared is None:
                shared = _shared_kv(h, s_kv_norm, s_w_kv, s_k_norm)
            h = _dilated_layer(h, mk, mv, m_q_norm[l], b_ln1[j], b_w_in, b_q_norm[j], shared[0], shared[1], b_w_out, j)
        h = _hier_moe(h, moe_ln[l], moe_router_group[l], moe_router_expert[l], moe_w_gate, moe_w_up, moe_w_down, l)
    return h.reshape(Bsz, S, D)
```
